```python
import math
import jax
import jax.numpy as jnp
from jax import lax
import numpy as np

D_MODEL = 1024
BATCH = 4
SEQ = 4096
DEPTH = 4
DEC_BATCH = 32
DEC_SEQ = 8
PAST_LEN = 8192
PAGE_SIZE = 128

HEAD_DIM = 64
SB_HEADS = 8
NSA_HEADS = 8
NSA_KV_GROUPS = 2
NSA_REP = NSA_HEADS // NSA_KV_GROUPS
CMP_BLOCK = 32
SEL_BLOCK = 64
SEL_RATIO = SEL_BLOCK // CMP_BLOCK
N_SEL = 16
WINDOW = 512
DIFF_HEADS = 8
DIFF_KV_HEADS = 4
DIFF_REP = DIFF_HEADS // DIFF_KV_HEADS
N_BUCKETS = 32
MAX_DISTANCE = 128
REL_HEADS = 8
N_EXPERTS = 32
TOP_K = 4
EXPERT_FF = D_MODEL
SWIGLU_ALPHA = 1.702
SWIGLU_LIMIT = 7.0
MOE_BLOCK = 128
Q_BLOCK = 128
LN_EPS = 1e-5
N_EVEN = (DEPTH + 1) // 2
N_ODD = DEPTH // 2
DEEPNORM_ALPHA = (2.0 * DEPTH) ** 0.25
DEEPNORM_BETA = (8.0 * DEPTH) ** -0.25
SB_W = SB_HEADS * HEAD_DIM
NSA_Q_W = NSA_HEADS * HEAD_DIM
NSA_KV_W = NSA_KV_GROUPS * HEAD_DIM
NSA_GATE_W = NSA_HEADS * 3
EVEN_IN = 3 * SB_W + NSA_Q_W + 6 * NSA_KV_W + NSA_GATE_W
EVEN_MIX = SB_W + NSA_Q_W
DIFF_Q_W = DIFF_HEADS * 2 * HEAD_DIM
DIFF_KV_W = DIFF_KV_HEADS * 2 * HEAD_DIM
ODD_IN = DIFF_Q_W + 2 * DIFF_KV_W
ODD_MIX = DIFF_HEADS * 2 * HEAD_DIM
NEG_INF = -1e30
FORCE = 1e9

kernel_name = 'sb_nsa_diffattn_moe_deepnorm_step'


def layer_norm(x, g, b):
    xf = x.astype(jnp.float32)
    mu = jnp.mean(xf, axis=-1, keepdims=True)
    var = jnp.mean(jnp.square(xf - mu), axis=-1, keepdims=True)
    return ((xf - mu) * lax.rsqrt(var + LN_EPS) * g + b).astype(x.dtype)


def rel_bucket(dist):
    n = jnp.maximum(dist, 0)
    exact = N_BUCKETS // 2
    nf = jnp.maximum(n, 1).astype(jnp.float32)
    large = exact + (jnp.log(nf / exact) / math.log(MAX_DISTANCE / exact)
                     * (N_BUCKETS - exact)).astype(jnp.int32)
    return jnp.where(n < exact, n, jnp.minimum(large, N_BUCKETS - 1))


def masked_softmax(s, valid):
    s = jnp.where(valid, s.astype(jnp.float32), NEG_INF)
    m = jnp.max(s, axis=-1, keepdims=True)
    e = jnp.where(valid, jnp.exp(s - m), 0.0)
    return e / jnp.maximum(jnp.sum(e, axis=-1, keepdims=True), 1e-30)


def gather_pages(cache, page_table, layer):
    p = cache[page_table, layer]
    return p.reshape((p.shape[0], p.shape[1] * p.shape[2]) + p.shape[3:])


def to_query_blocks(a):
    B, T = a.shape[:2]
    return a.reshape((B, T // Q_BLOCK, Q_BLOCK) + a.shape[2:]).swapaxes(0, 1)


def from_query_blocks(o):
    o = o.swapaxes(0, 1)
    return o.reshape((o.shape[0], o.shape[1] * o.shape[2]) + o.shape[3:])


def stick_breaking(q, k, v, q_pos, k_pos):
    z = jnp.einsum('bqhd,bkhd->bhqk', q, k).astype(jnp.float32) * (HEAD_DIM ** -0.5)
    valid = k_pos[None, :] < q_pos[:, None]
    log_keep = jnp.where(valid, jax.nn.log_sigmoid(-z), 0.0)
    later = lax.cumsum(log_keep, axis=3, reverse=True) - log_keep
    w = jnp.where(valid, jnp.exp(jax.nn.log_sigmoid(z) + later), 0.0)
    o = jnp.einsum('bhqk,bkhd->bqhd', w.astype(v.dtype), v)
    return o.reshape(o.shape[0], o.shape[1], -1)


def compress(rows, pe, w):
    B, T, G, Dh = rows.shape
    nc = T // CMP_BLOCK
    blk = rows[:, :nc * CMP_BLOCK].reshape(B, nc, CMP_BLOCK, G, Dh) + pe[:, None, :]
    blk = blk.transpose(0, 1, 3, 2, 4).reshape(B, nc, G, CMP_BLOCK * Dh)
    return blk @ w


def to_sel_blocks(rows):
    B, T, G, Dh = rows.shape
    ns = -(-T // SEL_BLOCK)
    rows = jnp.pad(rows, ((0, 0), (0, ns * SEL_BLOCK - T), (0, 0), (0, 0)))
    return rows.reshape(B, ns, SEL_BLOCK, G, Dh).transpose(0, 3, 1, 2, 4)


def nsa_attention(q, gates, q_pos, kc, vc, ks_blk, vs_blk, kw, vw, kw_pos, rel_bias):
    B, Qn = q.shape[:2]
    G = NSA_KV_GROUPS
    scale = HEAD_DIM ** -0.5
    rel_g = rel_bias.astype(jnp.float32).reshape(N_BUCKETS, G, NSA_REP)
    nc = kc.shape[1]
    c_end = jnp.arange(nc) * CMP_BLOCK + (CMP_BLOCK - 1)
    dist_c = q_pos[:, None] - c_end[None, :]
    bias_c = rel_g[rel_bucket(dist_c)].transpose(2, 3, 0, 1)
    s_c = jnp.einsum('bqgrd,bngd->bgrqn', q, kc).astype(jnp.float32) * scale + bias_c
    p_c = masked_softmax(s_c, dist_c >= 0)
    o_c = jnp.einsum('bgrqn,bngd->bqgrd', p_c.astype(vc.dtype), vc)
    ns = ks_blk.shape[2]
    imp = jnp.sum(p_c, axis=2)
    imp = jnp.pad(imp, ((0, 0), (0, 0), (0, 0), (0, SEL_RATIO * ns - nc)))
    imp = imp.reshape(B, G, Qn, ns, SEL_RATIO).sum(-1)
    blk = jnp.arange(ns)
    forced = (blk[None, :] == (q_pos // SEL_BLOCK)[:, None]) | (blk[None, :] == 0)
    future = blk[None, :] * SEL_BLOCK > q_pos[:, None]
    score = jnp.where(forced, FORCE, jnp.where(future, -FORCE, imp))
    n_sel = min(N_SEL, ns)
    _, idx = lax.top_k(score, n_sel)
    b_ix = jnp.arange(B)[:, None, None, None]
    g_ix = jnp.arange(G)[None, :, None, None]
    k_sel = ks_blk[b_ix, g_ix, idx].reshape(B, G, Qn, n_sel * SEL_BLOCK, HEAD_DIM)
    v_sel = vs_blk[b_ix, g_ix, idx].reshape(B, G, Qn, n_sel * SEL_BLOCK, HEAD_DIM)
    k_pos = (idx[..., None] * SEL_BLOCK + jnp.arange(SEL_BLOCK)).reshape(B, G, Qn, n_sel * SEL_BLOCK)
    dist_s = q_pos[None, None, :, None] - k_pos
    bias_s = jnp.moveaxis(rel_g[rel_bucket(dist_s), g_ix], -1, 2)
    s_s = jnp.einsum('bqgrd,bgqkd->bgrqk', q, k_sel).astype(jnp.float32) * scale + bias_s
    p_s = masked_softmax(s_s, (dist_s >= 0)[:, :, None])
    o_s = jnp.einsum('bgrqk,bgqkd->bqgrd', p_s.astype(v_sel.dtype), v_sel)
    dist_w = q_pos[:, None] - kw_pos[None, :]
    bias_w = rel_g[rel_bucket(dist_w)].transpose(2, 3, 0, 1)
    s_w = jnp.einsum('bqgrd,bkgd->bgrqk', q, kw).astype(jnp.float32) * scale + bias_w
    valid_w = (dist_w >= 0) & (dist_w < WINDOW) & (kw_pos[None, :] >= 0)
    p_w = masked_softmax(s_w, valid_w)
    o_w = jnp.einsum('bgrqk,bkgd->bqgrd', p_w.astype(vw.dtype), vw)
    g = gates.astype(o_c.dtype)
    out = g[..., 0:1] * o_c + g[..., 1:2] * o_s + g[..., 2:3] * o_w
    return out.reshape(B, Qn, NSA_HEADS * HEAD_DIM)


def even_project(x, w_in):
    B, T, _ = x.shape
    h = x @ w_in
    sb = h[..., :3 * SB_W].reshape(B, T, 3, SB_HEADS, HEAD_DIM)
    o = 3 * SB_W
    nq = h[..., o:o + NSA_Q_W].reshape(B, T, NSA_KV_GROUPS, NSA_REP, HEAD_DIM)
    o += NSA_Q_W
    nkv = h[..., o:o + 6 * NSA_KV_W].reshape(B, T, 6, NSA_KV_GROUPS, HEAD_DIM)
    o += 6 * NSA_KV_W
    gates = jax.nn.sigmoid(h[..., o:].astype(jnp.float32)).reshape(B, T, NSA_KV_GROUPS, NSA_REP, 3)
    return sb[:, :, 0], sb[:, :, 1], sb[:, :, 2], nq, nkv, gates


def even_mixer_prompt(x, w_in, cmp_pe, cmp_wk, cmp_wv, w_out, rel_bias):
    B, T, _ = x.shape
    sb_q, sb_k, sb_v, nq, nkv, gates = even_project(x, w_in)
    kc = compress(nkv[:, :, 0], cmp_pe, cmp_wk)
    vc = compress(nkv[:, :, 1], cmp_pe, cmp_wv)
    ks_blk = to_sel_blocks(nkv[:, :, 2])
    vs_blk = to_sel_blocks(nkv[:, :, 3])
    pad = ((0, 0), (WINDOW, 0), (0, 0), (0, 0))
    kw_pad = jnp.pad(nkv[:, :, 4], pad)
    vw_pad = jnp.pad(nkv[:, :, 5], pad)
    pos = jnp.arange(T)
    nb = T // Q_BLOCK

    def block(args):
        sq, nq_i, g_i, i = args
        q0 = i * Q_BLOCK
        q_pos = q0 + jnp.arange(Q_BLOCK)
        o_sb = stick_breaking(sq, sb_k, sb_v, q_pos, pos)
        kw = lax.dynamic_slice_in_dim(kw_pad, q0, WINDOW + Q_BLOCK, axis=1)
        vw = lax.dynamic_slice_in_dim(vw_pad, q0, WINDOW + Q_BLOCK, axis=1)
        kw_pos = q0 - WINDOW + jnp.arange(WINDOW + Q_BLOCK)
        o_nsa = nsa_attention(nq_i, g_i, q_pos, kc, vc, ks_blk, vs_blk, kw, vw, kw_pos, rel_bias)
        return jnp.concatenate([o_sb, o_nsa], axis=-1)

    o = lax.map(block, (to_query_blocks(sb_q), to_query_blocks(nq), to_query_blocks(gates), jnp.arange(nb)))
    out = from_query_blocks(o) @ w_out
    sb_rows = jnp.stack([sb_k, sb_v], axis=2)
    nsa_rows = nkv[:, :, :4]
    win_rows = nkv[:, T - min(WINDOW, T):, 4:]
    return out, sb_rows, nsa_rows, win_rows


def even_mixer_sample(x, past_sb, past_nsa, win_buf, w_in, cmp_pe, cmp_wk, cmp_wv, w_out, rel_bias):
    B, S, _ = x.shape
    P = past_sb.shape[1]
    WK = win_buf.shape[1]
    sb_q, sb_k, sb_v, nq, nkv, gates = even_project(x, w_in)
    q_pos = P + jnp.arange(S)
    sb_rows = jnp.stack([sb_k, sb_v], axis=2)
    sb_all = jnp.concatenate([past_sb, sb_rows], axis=1)
    o_sb = stick_breaking(sb_q, sb_all[:, :, 0], sb_all[:, :, 1], q_pos, jnp.arange(P + S))
    nsa_rows = nkv[:, :, :4]
    nsa_all = jnp.concatenate([past_nsa, nsa_rows], axis=1)
    kc = compress(nsa_all[:, :, 0], cmp_pe, cmp_wk)
    vc = compress(nsa_all[:, :, 1], cmp_pe, cmp_wv)
    ks_blk = to_sel_blocks(nsa_all[:, :, 2])
    vs_blk = to_sel_blocks(nsa_all[:, :, 3])
    win_all = jnp.concatenate([win_buf, nkv[:, :, 4:]], axis=1)
    kw_pos = P - WK + jnp.arange(WK + S)
    o_nsa = nsa_attention(nq, gates, q_pos, kc, vc, ks_blk, vs_blk,
                          win_all[:, :, 0], win_all[:, :, 1], kw_pos, rel_bias)
    out = jnp.concatenate([o_sb, o_nsa], axis=-1) @ w_out
    return out, sb_rows, nsa_rows, win_all[:, max(0, WK + S - WINDOW):]


def diff_attention(q, k, v, q_pos, k_pos, lam, lam_init, subln_g, rel_bias):
    B, Qn = q.shape[:2]
    Kn = k.shape[1]
    s = jnp.einsum('bqgrcd,bkgcd->bgrcqk', q, k).astype(jnp.float32) * (HEAD_DIM ** -0.5)
    bias = rel_bias.astype(jnp.float32)[rel_bucket(q_pos[:, None] - k_pos[None, :])]
    bias = bias.reshape(Qn, Kn, DIFF_KV_HEADS, DIFF_REP).transpose(2, 3, 0, 1)[:, :, None]
    p = masked_softmax(s + bias, k_pos[None, :] <= q_pos[:, None])
    a = p[:, :, :, 0] - lam * p[:, :, :, 1]
    o = jnp.einsum('bgrqk,bkge->bqgre', a.astype(v.dtype), v).astype(jnp.float32)
    o = o * lax.rsqrt(jnp.mean(jnp.square(o), axis=-1, keepdims=True) + LN_EPS) * subln_g * (1.0 - lam_init)
    return o.reshape(B, Qn, -1).astype(v.dtype)


def diff_project(x, w_in):
    B, T, _ = x.shape
    h = x @ w_in
    q = h[..., :DIFF_Q_W].reshape(B, T, DIFF_KV_HEADS, DIFF_REP, 2, HEAD_DIM)
    kv = h[..., DIFF_Q_W:].reshape(B, T, 2, DIFF_KV_HEADS, 2 * HEAD_DIM)
    return q, kv


def diff_mixer_prompt(x, w_in, lam, lam_init, subln_g, w_out, rel_bias):
    B, T, _ = x.shape
    q, kv = diff_project(x, w_in)
    k = kv[:, :, 0].reshape(B, T, DIFF_KV_HEADS, 2, HEAD_DIM)
    v = kv[:, :, 1]
    pos = jnp.arange(T)

    def block(args):
        q_i, i = args
        return diff_attention(q_i, k, v, i * Q_BLOCK + jnp.arange(Q_BLOCK), pos, lam, lam_init, subln_g, rel_bias)

    o = lax.map(block, (to_query_blocks(q), jnp.arange(T // Q_BLOCK)))
    return from_query_blocks(o) @ w_out, kv


def diff_mixer_sample(x, past_kv, w_in, lam, lam_init, subln_g, w_out, rel_bias):
    B, S, _ = x.shape
    P = past_kv.shape[1]
    q, kv = diff_project(x, w_in)
    kv_all = jnp.concatenate([past_kv, kv], axis=1)
    k = kv_all[:, :, 0].reshape(B, P + S, DIFF_KV_HEADS, 2, HEAD_DIM)
    o = diff_attention(q, k, kv_all[:, :, 1], P + jnp.arange(S), jnp.arange(P + S),
                       lam, lam_init, subln_g, rel_bias)
    return o @ w_out, kv


def moe(x, w_router, b_router, w_up, b_up, w_down, b_down):
    N, D = x.shape
    logits = x.astype(jnp.float32) @ w_router.astype(jnp.float32) + b_router.astype(jnp.float32)
    top_val, top_idx = lax.top_k(logits, TOP_K)
    gate = jax.nn.softmax(top_val, axis=-1)
    n_assign = N * TOP_K
    flat_e = top_idx.reshape(-1)
    order = jnp.argsort(flat_e).astype(jnp.int32)
    sorted_e = flat_e[order]
    counts = jnp.bincount(flat_e, length=N_EXPERTS)
    padded = (counts + MOE_BLOCK - 1) // MOE_BLOCK * MOE_BLOCK
    pad_end = jnp.cumsum(padded)
    pad_start = pad_end - padded
    grp_start = jnp.cumsum(counts) - counts
    slot = pad_start[sorted_e] + jnp.arange(n_assign) - grp_start[sorted_e]
    n_blocks = -(-n_assign // MOE_BLOCK) + N_EXPERTS
    n_slots = n_blocks * MOE_BLOCK
    slot_assign = jnp.full((n_slots,), n_assign, jnp.int32).at[slot].set(order)
    slot_tok = slot_assign // TOP_K
    slot_gate = jnp.concatenate([gate.reshape(-1), jnp.zeros((1,), jnp.float32)])[slot_assign]
    blk_exp = jnp.minimum(jnp.searchsorted(pad_end, jnp.arange(n_blocks) * MOE_BLOCK, side='right'),
                          N_EXPERTS - 1)
    x_pad = jnp.concatenate([x, jnp.zeros((1, D), x.dtype)], axis=0)
    xb = x_pad[slot_tok].reshape(n_blocks, MOE_BLOCK, D)

    def expert_block(args):
        xe, e = args
        h = xe @ w_up[e] + b_up[e]
        h_glu = jnp.minimum(h[:, 0::2], SWIGLU_LIMIT)
        h_lin = jnp.clip(h[:, 1::2], -SWIGLU_LIMIT, SWIGLU_LIMIT)
        a = h_glu * jax.nn.sigmoid(SWIGLU_ALPHA * h_glu) * (h_lin + 1.0)
        return a @ w_down[e] + b_down[e]

    yb = lax.map(expert_block, (xb, blk_exp)).reshape(n_slots, D)
    out = jnp.zeros((N + 1, D), yb.dtype).at[slot_tok].add(yb * slot_gate[:, None].astype(yb.dtype))
    return out[:N].astype(x.dtype)


def setup_inputs(seed: int = 0) -> dict:
    key = jax.random.key(seed)
    keys = jax.random.split(key, 32)
    f32 = jnp.float32

    def nrm(i, shape, scale):
        return jax.random.normal(keys[i], shape, f32) * scale

    n_pages = PAST_LEN // PAGE_SIZE
    n_used = DEC_BATCH * n_pages
    n_pool = n_used + (n_used + 3) // 4
    page_table = jax.random.permutation(keys[0], n_pool)[:n_used].reshape(DEC_BATCH, n_pages).astype(jnp.int32)
    win_keep = min(WINDOW, PAST_LEN)
    return {
        'x_prompt': nrm(1, (BATCH, SEQ, D_MODEL), 1.0),
        'x_sample': nrm(2, (DEC_BATCH, DEC_SEQ, D_MODEL), 1.0),
        'cache_sb_kv': nrm(3, (n_pool, N_EVEN, PAGE_SIZE, 2, SB_HEADS, HEAD_DIM), 1.0),
        'cache_nsa_kv': nrm(4, (n_pool, N_EVEN, PAGE_SIZE, 4, NSA_KV_GROUPS, HEAD_DIM), 1.0),
        'cache_diff_kv': nrm(5, (n_pool, N_ODD, PAGE_SIZE, 2, DIFF_KV_HEADS, 2 * HEAD_DIM), 1.0),
        'state_nsa_win': nrm(6, (DEC_BATCH, N_EVEN, win_keep, 2, NSA_KV_GROUPS, HEAD_DIM), 1.0),
        'page_table': page_table,
        'rel_bias': nrm(7, (N_BUCKETS, REL_HEADS), 0.1),
        'even_w_in': nrm(8, (N_EVEN, D_MODEL, EVEN_IN), D_MODEL ** -0.5),
        'even_cmp_pe': nrm(9, (N_EVEN, CMP_BLOCK, HEAD_DIM), 0.1),
        'even_cmp_wk': nrm(10, (N_EVEN, CMP_BLOCK * HEAD_DIM, HEAD_DIM), (CMP_BLOCK * HEAD_DIM) ** -0.5),
        'even_cmp_wv': nrm(11, (N_EVEN, CMP_BLOCK * HEAD_DIM, HEAD_DIM), (CMP_BLOCK * HEAD_DIM) ** -0.5),
        'even_w_out': nrm(12, (N_EVEN, EVEN_MIX, D_MODEL), DEEPNORM_BETA * EVEN_MIX ** -0.5),
        'odd_w_in': nrm(13, (N_ODD, D_MODEL, ODD_IN), D_MODEL ** -0.5),
        'odd_lambda': nrm(14, (N_ODD, 4, HEAD_DIM), 0.1),
        'odd_subln_g': 1.0 + nrm(15, (N_ODD, 2 * HEAD_DIM), 0.01),
        'odd_w_out': nrm(16, (N_ODD, ODD_MIX, D_MODEL), DEEPNORM_BETA * ODD_MIX ** -0.5),
        'ln_mix_g': 1.0 + nrm(17, (DEPTH, D_MODEL), 0.01),
        'ln_mix_b': nrm(18, (DEPTH, D_MODEL), 0.01),
        'ln_ffn_g': 1.0 + nrm(19, (DEPTH, D_MODEL), 0.01),
        'ln_ffn_b': nrm(20, (DEPTH, D_MODEL), 0.01),
        'moe_w_router': nrm(21, (DEPTH, D_MODEL, N_EXPERTS), D_MODEL ** -0.5),
        'moe_b_router': nrm(22, (DEPTH, N_EXPERTS), 0.01),
        'moe_w_up': nrm(23, (DEPTH, N_EXPERTS, D_MODEL, 2 * EXPERT_FF), D_MODEL ** -0.5),
        'moe_b_up': nrm(24, (DEPTH, N_EXPERTS, 2 * EXPERT_FF), 0.01),
        'moe_w_down': nrm(25, (DEPTH, N_EXPERTS, EXPERT_FF, D_MODEL), DEEPNORM_BETA * EXPERT_FF ** -0.5),
        'moe_b_down': nrm(26, (DEPTH, N_EXPERTS, D_MODEL), 0.01),
    }


def reference(x_prompt, x_sample, cache_sb_kv, cache_nsa_kv, cache_diff_kv, state_nsa_win, page_table,
              rel_bias, even_w_in, even_cmp_pe, even_cmp_wk, even_cmp_wv, even_w_out,
              odd_w_in, odd_lambda, odd_subln_g, odd_w_out,
              ln_mix_g, ln_mix_b, ln_ffn_g, ln_ffn_b,
              moe_w_router, moe_b_router, moe_w_up, moe_b_up, moe_w_down, moe_b_down):
    B, T, D = x_prompt.shape
    DB, S, _ = x_sample.shape
    xp, xs = x_prompt, x_sample
    sb_p, sb_s, nsa_p, nsa_s, win_p, win_s, diff_p, diff_s = [], [], [], [], [], [], [], []
    for l in range(DEPTH):
        j = l // 2
        if l % 2 == 0:
            mp, r_sb, r_nsa, r_win = even_mixer_prompt(
                xp, even_w_in[j], even_cmp_pe[j], even_cmp_wk[j], even_cmp_wv[j], even_w_out[j], rel_bias)
            ms, s_sb, s_nsa, s_win = even_mixer_sample(
                xs, gather_pages(cache_sb_kv, page_table, j), gather_pages(cache_nsa_kv, page_table, j),
                state_nsa_win[:, j], even_w_in[j], even_cmp_pe[j], even_cmp_wk[j], even_cmp_wv[j],
                even_w_out[j], rel_bias)
            sb_p.append(r_sb)
            sb_s.append(s_sb)
            nsa_p.append(r_nsa)
            nsa_s.append(s_nsa)
            win_p.append(r_win)
            win_s.append(s_win)
        else:
            lam_init = 0.8 - 0.6 * math.exp(-0.3 * l)
            lv = odd_lambda[j].astype(jnp.float32)
            lam = jnp.exp(jnp.sum(lv[0] * lv[1])) - jnp.exp(jnp.sum(lv[2] * lv[3])) + lam_init
            mp, r_diff = diff_mixer_prompt(xp, odd_w_in[j], lam, lam_init, odd_subln_g[j], odd_w_out[j], rel_bias)
            ms, s_diff = diff_mixer_sample(xs, gather_pages(cache_diff_kv, page_table, j), odd_w_in[j],
                                           lam, lam_init, odd_subln_g[j], odd_w_out[j], rel_bias)
            diff_p.append(r_diff)
            diff_s.append(s_diff)
        xp = layer_norm(DEEPNORM_ALPHA * xp + mp, ln_mix_g[l], ln_mix_b[l])
        xs = layer_norm(DEEPNORM_ALPHA * xs + ms, ln_mix_g[l], ln_mix_b[l])
        fp = moe(xp.reshape(B * T, D), moe_w_router[l], moe_b_router[l], moe_w_up[l], moe_b_up[l],
                 moe_w_down[l], moe_b_down[l]).reshape(B, T, D)
        fs = moe(xs.reshape(DB * S, D), moe_w_router[l], moe_b_router[l], moe_w_up[l], moe_b_up[l],
                 moe_w_down[l], moe_b_down[l]).reshape(DB, S, D)
        xp = layer_norm(DEEPNORM_ALPHA * xp + fp, ln_ffn_g[l], ln_ffn_b[l])
        xs = layer_norm(DEEPNORM_ALPHA * xs + fs, ln_ffn_g[l], ln_ffn_b[l])
    new_sb_prompt = jnp.stack(sb_p, axis=1)
    new_sb_sample = jnp.stack(sb_s, axis=1)
    new_nsa_prompt = jnp.stack(nsa_p, axis=1)
    new_nsa_sample = jnp.stack(nsa_s, axis=1)
    new_diff_prompt = jnp.stack(diff_p, axis=1)
    new_diff_sample = jnp.stack(diff_s, axis=1)
    new_win_prompt = jnp.stack(win_p, axis=1)
    new_win_sample = jnp.stack(win_s, axis=1)
    return (xp, xs, new_sb_prompt, new_sb_sample, new_nsa_prompt, new_nsa_sample,
            new_diff_prompt, new_diff_sample, new_win_prompt, new_win_sample)
```

```python
import functools
import math

import numpy as np
import jax
import jax.numpy as jnp
from jax import lax
from jax.experimental import pallas as pl
from jax.experimental.pallas import tpu as pltpu

F32 = jnp.float32
BF16 = jnp.bfloat16

D_MODEL = 1024
HEAD_DIM = 64
SB_HEADS = 8
NSA_HEADS = 8
NSA_GROUPS = 2
NSA_REP = NSA_HEADS // NSA_GROUPS
CMP_BLOCK = 32
SEL_BLOCK = 64
N_SEL = 16
WINDOW = 512
DIFF_HEADS = 8
DIFF_KV_HEADS = 4
DIFF_REP = DIFF_HEADS // DIFF_KV_HEADS
N_BUCKETS = 32
MAX_DISTANCE = 128
N_EXPERTS = 32
TOP_K = 4
SWIGLU_ALPHA = 1.702
SWIGLU_LIMIT = 7.0
LN_EPS = 1e-5
PAGE = 128
NEG = -1e30
FORCE = 1e9
ATT_SCALE = HEAD_DIM ** -0.5

SB_W = SB_HEADS * HEAD_DIM
NSA_Q_W = NSA_HEADS * HEAD_DIM
NSA_KV_W = NSA_GROUPS * HEAD_DIM
NSA_GATE_W = NSA_HEADS * 3
EVEN_IN = 3 * SB_W + NSA_Q_W + 6 * NSA_KV_W + NSA_GATE_W
EVEN_IN_PAD = -(-EVEN_IN // 128) * 128
DIFF_Q_W = DIFF_HEADS * 2 * HEAD_DIM
DIFF_KV_W = DIFF_KV_HEADS * 2 * HEAD_DIM
ODD_IN = DIFF_Q_W + 2 * DIFF_KV_W

LANES = 128
SUBLANES = 8
VMEM_LIMIT = 52 * 1024 * 1024

QB = 128
KT = 512
KPAD = KT - QB
MOE_CH = 128
DEC_S = 8


def _dot(a, b):
    return jnp.dot(a, b, preferred_element_type=F32)


def _dot_nt(a, b):
    return lax.dot_general(a, b, (((1,), (1,)), ((), ())), preferred_element_type=F32)


def _dot_tn(a, b):
    return lax.dot_general(a, b, (((0,), (0,)), ((), ())), preferred_element_type=F32)


def _iota(shape, dim):
    return lax.broadcasted_iota(jnp.int32, shape, dim)


def _params(sem, vmem=VMEM_LIMIT):
    return pltpu.CompilerParams(dimension_semantics=sem, vmem_limit_bytes=vmem)


def _bucket_np(dist):
    n = np.maximum(dist, 0)
    exact = N_BUCKETS // 2
    nf = np.maximum(n, 1).astype(np.float32)
    large = exact + (np.log(nf / np.float32(exact)) / np.float32(math.log(MAX_DISTANCE / exact))
                     * np.float32(N_BUCKETS - exact)).astype(np.int32)
    return np.where(n < exact, n, np.minimum(large, N_BUCKETS - 1)).astype(np.int32)


def _bias_table(rel_bias, dist, head_of_row=None):
    bucket = _bucket_np(dist)
    rb = rel_bias.astype(F32)
    if head_of_row is None:
        return jnp.transpose(rb[bucket], (2, 0, 1))
    return rb[bucket, np.asarray(head_of_row)[:, None]]


def _mm_kernel(x_ref, w_ref, o_ref):
    o_ref[...] = _dot(x_ref[...], w_ref[...])


def matmul(x, w, tm):
    m, k = x.shape
    n = w.shape[1]
    tm = min(tm, m)
    return pl.pallas_call(
        _mm_kernel, grid=(m // tm,),
        in_specs=[pl.BlockSpec((tm, k), lambda i: (i, 0)), pl.BlockSpec((k, n), lambda i: (0, 0))],
        out_specs=pl.BlockSpec((tm, n), lambda i: (i, 0)),
        out_shape=jax.ShapeDtypeStruct((m, n), F32),
        compiler_params=_params(("arbitrary",)), name="in_proj")(x, w)


def _softmax_step(s, valid, m, l, acc, v):
    if valid is not None:
        s = jnp.where(valid, s, NEG)
    m_new = jnp.maximum(m, jnp.max(s, axis=-1, keepdims=True))
    p = jnp.exp(s - m_new)
    if valid is not None:
        p = jnp.where(valid, p, 0.0)
    alpha = jnp.exp(m - m_new)
    l = alpha * l + jnp.sum(p, axis=-1, keepdims=True)
    acc = alpha * acc + _dot(p.astype(BF16), v)
    return m_new, l, acc


def _softmax_full(s, valid):
    if valid is not None:
        s = jnp.where(valid, s, NEG)
    m = jnp.max(s, axis=-1, keepdims=True)
    e = jnp.exp(s - m)
    if valid is not None:
        e = jnp.where(valid, e, 0.0)
    return e / jnp.maximum(jnp.sum(e, axis=-1, keepdims=True), 1e-30)


def _sb_step(z, valid, carry, acc, v, u):
    sp = jnp.maximum(z, 0.0) + jnp.log(1.0 + jnp.exp(-jnp.abs(z)))
    lk = -sp if valid is None else jnp.where(valid, -sp, 0.0)
    hi = lk.astype(BF16)
    lo = (lk - hi.astype(F32)).astype(BF16)
    later = _dot(hi, u) + _dot(lo, u) + carry
    w = jnp.exp(z - sp + later)
    if valid is not None:
        w = jnp.where(valid, w, 0.0)
    acc = acc + _dot(w.astype(BF16), v)
    carry = carry + jnp.sum(lk, axis=-1, keepdims=True)
    return carry, acc


def _later_matrix(n):
    j = np.arange(n)
    return jnp.asarray((j[:, None] > j[None, :]).astype(np.float32), dtype=BF16)


def _sb_prompt_kernel(q_ref, k_ref, v_ref, u_ref, o_ref):
    qi = pl.program_id(2)
    q0 = qi * QB
    q = (q_ref[0, 0].astype(F32) * ATT_SCALE).astype(BF16)
    u = u_ref[...]
    row = _iota((QB, KT), 0)
    col = _iota((QB, KT), 1)

    def tile(j, carry, acc, first):
        start = pl.multiple_of(q0 - j * KT, QB)
        k = k_ref[0, 0, pl.ds(start, KT), :]
        v = v_ref[0, 0, pl.ds(start, KT), :]
        z = _dot_nt(q, k)
        valid = col >= KPAD - start
        if first:
            valid = valid & (row + KPAD - col > 0)
        return _sb_step(z, valid, carry, acc, v, u)

    carry, acc = tile(0, jnp.zeros((QB, 1), F32), jnp.zeros((QB, HEAD_DIM), F32), True)
    carry, acc = lax.fori_loop(1, qi // (KT // QB) + 1,
                               lambda j, c: tile(j, c[0], c[1], False), (carry, acc))
    o_ref[0, 0] = acc.astype(o_ref.dtype)


def sb_prompt(q, k, v):
    b, h, t, _ = q.shape
    tp = k.shape[2]
    kv_spec = pl.BlockSpec((1, 1, tp, HEAD_DIM), lambda b_, h_, i: (b_, h_, 0, 0))
    return pl.pallas_call(
        _sb_prompt_kernel, grid=(b, h, t // QB),
        in_specs=[pl.BlockSpec((1, 1, QB, HEAD_DIM), lambda b_, h_, i: (b_, h_, i, 0)), kv_spec, kv_spec,
                  pl.BlockSpec((KT, KT), lambda b_, h_, i: (0, 0))],
        out_specs=pl.BlockSpec((1, 1, QB, HEAD_DIM), lambda b_, h_, i: (b_, h_, i, 0)),
        out_shape=jax.ShapeDtypeStruct((b, h, t, HEAD_DIM), BF16),
        compiler_params=_params(("arbitrary",) * 3), name="sb_prompt")(q, k, v, _later_matrix(KT))


def _diff_lambda(lam_ref, lam_init):
    lv = lam_ref[...].astype(F32)
    a = jnp.sum(lv[0:1] * lv[1:2], axis=-1, keepdims=True)
    b = jnp.sum(lv[2:3] * lv[3:4], axis=-1, keepdims=True)
    return jnp.exp(a) - jnp.exp(b) + lam_init


def _diff_finish(o1, l1, o2, l2, lam, sub_g, lam_init):
    a = o1 / jnp.maximum(l1, 1e-30) - lam * (o2 / jnp.maximum(l2, 1e-30))
    a = a * lax.rsqrt(jnp.mean(jnp.square(a), axis=-1, keepdims=True) + LN_EPS)
    return a * sub_g * (1.0 - lam_init)


def _diff_prompt_kernel(c31_ref, q_ref, k_ref, v_ref, b0_ref, lam_ref, sg_ref, o_ref, *, lam_init):
    g = pl.program_id(1)
    qi = pl.program_id(2)
    q0 = qi * QB
    row = _iota((QB, KT), 0)
    col = _iota((QB, KT), 1)
    streams = [(r, c) for r in range(DIFF_REP) for c in range(2)]
    qs = {(r, c): (q_ref[0, 0, r, c].astype(F32) * ATT_SCALE).astype(BF16) for r, c in streams}

    def tile(j, state, first):
        start = pl.multiple_of(q0 - j * KT, QB)
        v = v_ref[0, 0, pl.ds(start, KT), :]
        valid = col >= KPAD - start
        if first:
            valid = valid & (row + KPAD - col >= 0)
        out = []
        for idx, (r, c) in enumerate(streams):
            k = k_ref[0, 0, c, pl.ds(start, KT), :]
            s = _dot_nt(qs[(r, c)], k)
            s = s + (b0_ref[r] if first else c31_ref[g * DIFF_REP + r])
            out.append(_softmax_step(s, valid, *state[idx], v))
        return tuple(out)

    init = tuple((jnp.full((QB, 1), NEG, F32), jnp.zeros((QB, 1), F32),
                  jnp.zeros((QB, 2 * HEAD_DIM), F32)) for _ in streams)
    state = tile(0, init, True)
    state = lax.fori_loop(1, qi // (KT // QB) + 1, lambda j, st: tile(j, st, False), state)
    lam = _diff_lambda(lam_ref, lam_init)
    outs = []
    for r in range(DIFF_REP):
        (_, l1, o1), (_, l2, o2) = state[2 * r], state[2 * r + 1]
        outs.append(_diff_finish(o1, l1, o2, l2, lam, sg_ref[...], lam_init))
    o_ref[0] = jnp.concatenate(outs, axis=-1).astype(o_ref.dtype)


def diff_prompt(q, k, v, bias0, c31, lam_rows, sub_g, lam_init):
    b, gk, _, _, t, _ = q.shape
    tp = k.shape[3]
    return pl.pallas_call(
        functools.partial(_diff_prompt_kernel, lam_init=lam_init), grid=(b, gk, t // QB),
        in_specs=[pl.BlockSpec(memory_space=pltpu.SMEM),
                  pl.BlockSpec((1, 1, DIFF_REP, 2, QB, HEAD_DIM), lambda b_, g_, i: (b_, g_, 0, 0, i, 0)),
                  pl.BlockSpec((1, 1, 2, tp, HEAD_DIM), lambda b_, g_, i: (b_, g_, 0, 0, 0)),
                  pl.BlockSpec((1, 1, tp, 2 * HEAD_DIM), lambda b_, g_, i: (b_, g_, 0, 0)),
                  pl.BlockSpec((DIFF_REP, QB, KT), lambda b_, g_, i: (g_, 0, 0)),
                  pl.BlockSpec((4, HEAD_DIM), lambda b_, g_, i: (0, 0)),
                  pl.BlockSpec((1, 2 * HEAD_DIM), lambda b_, g_, i: (0, 0))],
        out_specs=pl.BlockSpec((1, QB, DIFF_REP * 2 * HEAD_DIM), lambda b_, g_, i: (b_, i, g_)),
        out_shape=jax.ShapeDtypeStruct((b, t, DIFF_HEADS * 2 * HEAD_DIM), BF16),
        compiler_params=_params(("arbitrary",) * 3), name="diff_prompt")(
            c31, q, k, v, bias0, lam_rows, sub_g)


def _select_blocks(score_ref, n_rows, n_keep):
    sc = score_ref[...]
    blk = _iota(sc.shape, 0)

    def body(i, rank):
        r = score_ref[pl.ds(i, 1), :]
        better = (r > sc) | ((r == sc) & (i < blk))
        return rank + jnp.where(better, 1.0, 0.0)

    rank = lax.fori_loop(0, n_rows, body, jnp.zeros(sc.shape, F32), unroll=8)
    return rank < n_keep


def _nsa_prompt_kernel(c31_ref, q_ref, kc_ref, vc_ref, ks_ref, vs_ref, kw_ref, vw_ref, bc_ref, b0_ref,
                       gate_ref, o_ref, impt_ref, score_ref, *, n_sel_blocks, n_keep):
    g = pl.program_id(1)
    qi = pl.program_id(2)
    q0 = qi * QB
    n_tiles = qi // (KT // QB) + 1
    nc = kc_ref.shape[2]
    ns = nc // 2
    heads = range(NSA_REP)
    qs = [(q_ref[0, 0, r].astype(F32) * ATT_SCALE).astype(BF16) for r in heads]

    kc = kc_ref[0, 0]
    vc = vc_ref[0, 0]
    qrow = _iota((QB, nc), 0) + q0
    valid_c = qrow - (_iota((QB, nc), 1) * CMP_BLOCK + (CMP_BLOCK - 1)) >= 0
    o_c = []
    imp = jnp.zeros((QB, nc), F32)
    for r in heads:
        p = _softmax_full(_dot_nt(qs[r], kc) + bc_ref[r], valid_c)
        o_c.append(_dot(p.astype(BF16), vc))
        imp = imp + p
    impt_ref[...] = imp.T
    imp_s = impt_ref[pl.ds(0, ns, stride=2), :] + impt_ref[pl.ds(1, ns, stride=2), :]
    blk = _iota((ns, QB), 0)
    qpos = _iota((ns, QB), 1) + q0
    forced = (blk == qpos // SEL_BLOCK) | (blk == 0)
    future = blk * SEL_BLOCK > qpos
    score = jnp.where(forced, FORCE, jnp.where(future, -FORCE, imp_s))
    score_ref[...] = jnp.where(blk < n_sel_blocks, score, -3e38)
    sel_t = jnp.where(_select_blocks(score_ref, ns, n_keep), 1.0, 0.0).astype(BF16)

    row = _iota((QB, KT), 0)
    col = _iota((QB, KT), 1)
    causal0 = row + KPAD - col >= 0
    blk_e = _iota((ns, KT), 0)
    col_e = _iota((ns, KT), 1) // SEL_BLOCK

    def branch(k_ref, v_ref, hi, valid_fn):
        def tile(j, state, first):
            start = pl.multiple_of(q0 - j * KT, QB)
            k = k_ref[0, 0, pl.ds(start, KT), :]
            v = v_ref[0, 0, pl.ds(start, KT), :]
            valid = valid_fn(j, start, first)
            out = []
            for r in heads:
                s = _dot_nt(qs[r], k)
                s = s + (b0_ref[r] if first else c31_ref[g * NSA_REP + r])
                out.append(_softmax_step(s, valid, *state[r], v))
            return tuple(out)

        init = tuple((jnp.full((QB, 1), NEG, F32), jnp.zeros((QB, 1), F32),
                      jnp.zeros((QB, HEAD_DIM), F32)) for _ in heads)
        state = tile(0, init, True)
        state = lax.fori_loop(1, hi, lambda j, st: tile(j, st, False), state)
        return [acc / jnp.maximum(l, 1e-30) for (_, l, acc) in state]

    def valid_sel(j, start, first):
        base = start // SEL_BLOCK - KPAD // SEL_BLOCK
        expand = jnp.where(blk_e == col_e + base, 1.0, 0.0).astype(BF16)
        chosen = _dot_tn(sel_t, expand) > 0.5
        return (chosen & causal0) if first else chosen

    def valid_win(j, start, first):
        ok = col >= KPAD - start
        if first:
            return ok & causal0
        return ok & (row + KPAD - col + j * KT < WINDOW)

    o_s = branch(ks_ref, vs_ref, n_tiles, valid_sel)
    o_w = branch(kw_ref, vw_ref, jnp.minimum(n_tiles, (WINDOW + QB - 1) // KT + 1), valid_win)

    gate = jax.nn.sigmoid(gate_ref[0, 0].astype(F32))
    outs = []
    for r in heads:
        g0, g1, g2 = (gate[:, 3 * r + i:3 * r + i + 1] for i in range(3))
        outs.append(g0 * o_c[r] + g1 * o_s[r] + g2 * o_w[r])
    o_ref[0] = jnp.concatenate(outs, axis=-1).astype(o_ref.dtype)


def nsa_prompt(q, kc, vc, ks, vs, kw, vw, bias_c, bias0, c31, gates, n_sel_blocks):
    b, g, _, t, _ = q.shape
    tp = ks.shape[2]
    nc = kc.shape[2]
    cmp_spec = pl.BlockSpec((1, 1, nc, HEAD_DIM), lambda b_, g_, i: (b_, g_, 0, 0))
    kv_spec = pl.BlockSpec((1, 1, tp, HEAD_DIM), lambda b_, g_, i: (b_, g_, 0, 0))
    kern = functools.partial(_nsa_prompt_kernel, n_sel_blocks=n_sel_blocks,
                             n_keep=min(N_SEL, n_sel_blocks))
    return pl.pallas_call(
        kern, grid=(b, g, t // QB),
        in_specs=[pl.BlockSpec(memory_space=pltpu.SMEM),
                  pl.BlockSpec((1, 1, NSA_REP, QB, HEAD_DIM), lambda b_, g_, i: (b_, g_, 0, i, 0)),
                  cmp_spec, cmp_spec, kv_spec, kv_spec, kv_spec, kv_spec,
                  pl.BlockSpec((NSA_REP, QB, nc), lambda b_, g_, i: (g_, i, 0)),
                  pl.BlockSpec((NSA_REP, QB, KT), lambda b_, g_, i: (g_, 0, 0)),
                  pl.BlockSpec((1, 1, QB, LANES), lambda b_, g_, i: (b_, g_, i, 0))],
        out_specs=pl.BlockSpec((1, QB, NSA_REP * HEAD_DIM), lambda b_, g_, i: (b_, i, g_)),
        out_shape=jax.ShapeDtypeStruct((b, t, NSA_HEADS * HEAD_DIM), BF16),
        scratch_shapes=[pltpu.VMEM((nc, QB), F32), pltpu.VMEM((nc // 2, QB), F32)],
        compiler_params=_params(("arbitrary",) * 3), name="nsa_prompt")(
            c31, q, kc, vc, ks, vs, kw, vw, bias_c, bias0, gates)


CMP_PP = 16
CMP_HALF = CMP_BLOCK // 2


def _compress_kernel(pt_ref, w_ref, pe_ref, *rest, pp):
    pages, o_ref, lo_ref, hi_ref = rest[:pp], rest[pp], rest[pp + 1], rest[pp + 2]
    width = 2 * HEAD_DIM
    acc = jnp.zeros((pp * SUBLANES, 2 * width), F32)
    for i in range(CMP_HALF):
        rows = [pages[p][pl.ds(i, SUBLANES, stride=CMP_HALF), :] + pe_ref[i] for p in range(pp)]
        acc = acc + _dot(jnp.concatenate(rows, axis=0).astype(BF16), w_ref[0, i])
    lo_ref[...] = acc[:, :width]
    hi_ref[...] = acc[:, width:]
    n = pp * SUBLANES // 2
    o_ref[0] = lo_ref[pl.ds(0, n, stride=2), :] + hi_ref[pl.ds(1, n, stride=2), :]


def compress_pages(page_table, src, layer, col_block, w_cat, pe_tiles):
    nseq, n_pages = page_table.shape
    pp = min(CMP_PP, n_pages)
    width = 2 * HEAD_DIM

    def page_spec(p):
        return pl.BlockSpec((None, None, PAGE, width),
                            lambda b, s, kv, pt: (pt[b, s * pp + p], layer, 0, col_block + kv))

    grid_spec = pltpu.PrefetchScalarGridSpec(
        num_scalar_prefetch=1, grid=(nseq, n_pages // pp, 2),
        in_specs=[pl.BlockSpec((1, CMP_HALF, width, 2 * width), lambda b, s, kv, pt: (kv, 0, 0, 0)),
                  pl.BlockSpec((CMP_HALF, SUBLANES, width), lambda b, s, kv, pt: (0, 0, 0))]
        + [page_spec(p) for p in range(pp)],
        out_specs=pl.BlockSpec((1, pp * 4, width), lambda b, s, kv, pt: (b, s, kv)),
        scratch_shapes=[pltpu.VMEM((pp * SUBLANES, width), F32), pltpu.VMEM((pp * SUBLANES, width), F32)])
    return pl.pallas_call(
        functools.partial(_compress_kernel, pp=pp), grid_spec=grid_spec,
        out_shape=jax.ShapeDtypeStruct((nseq, n_pages * 4, 2 * width), F32),
        compiler_params=_params(("arbitrary",) * 3), name="compress")(
            page_table, w_cat, pe_tiles, *([src] * pp))


def _compress_weights(pe, wk, wv):
    def cat(w):
        w = w.reshape(CMP_BLOCK, HEAD_DIM, HEAD_DIM)
        z = jnp.zeros_like(w)
        full = jnp.concatenate([jnp.concatenate([w, z], -1), jnp.concatenate([z, w], -1)], axis=1)
        return jnp.concatenate([full[:CMP_HALF], full[CMP_HALF:]], axis=-1)

    w_cat = jnp.stack([cat(wk), cat(wv)]).astype(BF16)
    pe2 = jnp.tile(pe.astype(F32), (1, 2))
    pe_tiles = jnp.stack([pe2[:CMP_HALF], pe2[CMP_HALF:]], axis=1)
    pe_tiles = jnp.tile(pe_tiles, (1, SUBLANES // 2, 1))
    return w_cat, pe_tiles


SB_PP = 8
DIFF_PP = 8
SEL_PP = 16


def _page_specs(pp, n_pages, width, col_block, layer, reverse=True):
    def spec(p):
        def index(b, s, pt):
            logical = n_pages - 1 - (s * pp + p) if reverse else s * pp + p
            return (pt[b, logical], layer, 0, col_block)
        return pl.BlockSpec((None, None, PAGE, width), index)
    return [spec(p) for p in range(pp)]


def _token_of_row(shape):
    return _iota(shape, 0) % DEC_S


def _sb_sample_kernel(pt_ref, q_ref, new_ref, u_ref, *rest, pp):
    pages, o_ref, carry_ref, acc_ref = rest[:pp], rest[pp], rest[pp + 1], rest[pp + 2]
    s = pl.program_id(1)
    q = q_ref[0]
    u = u_ref[...]
    rows = SB_HEADS * DEC_S

    @pl.when(s == 0)
    def _():
        k = new_ref[0, :, :SB_W]
        v = new_ref[0, :, SB_W:]
        valid = _iota((rows, PAGE), 1) < _token_of_row((rows, PAGE))
        carry, acc = _sb_step(_dot_nt(q, k), valid, jnp.zeros((rows, 1), F32),
                              jnp.zeros((rows, SB_W), F32), v, u)
        carry_ref[...] = carry
        acc_ref[...] = acc

    carry = carry_ref[...]
    acc = acc_ref[...]
    for p in range(pp):
        page = pages[p][...]
        k = page[:, :SB_W].astype(BF16)
        v = page[:, SB_W:].astype(BF16)
        carry, acc = _sb_step(_dot_nt(q, k), None, carry, acc, v, u)
    carry_ref[...] = carry
    acc_ref[...] = acc

    @pl.when(s == pl.num_programs(1) - 1)
    def _():
        a = acc_ref[...]
        o_ref[0] = jnp.concatenate(
            [a[h * DEC_S:(h + 1) * DEC_S, h * HEAD_DIM:(h + 1) * HEAD_DIM] for h in range(SB_HEADS)],
            axis=-1).astype(o_ref.dtype)


def sb_sample(page_table, qbd, new_kv, cache, layer):
    db, n_pages = page_table.shape
    pp = min(SB_PP, n_pages)
    rows = SB_HEADS * DEC_S
    grid_spec = pltpu.PrefetchScalarGridSpec(
        num_scalar_prefetch=1, grid=(db, n_pages // pp),
        in_specs=[pl.BlockSpec((1, rows, SB_W), lambda b, s, pt: (b, 0, 0)),
                  pl.BlockSpec((1, PAGE, 2 * SB_W), lambda b, s, pt: (b, 0, 0)),
                  pl.BlockSpec((PAGE, PAGE), lambda b, s, pt: (0, 0))]
        + _page_specs(pp, n_pages, 2 * SB_W, 0, layer),
        out_specs=pl.BlockSpec((1, DEC_S, SB_W), lambda b, s, pt: (b, 0, 0)),
        scratch_shapes=[pltpu.VMEM((rows, 1), F32), pltpu.VMEM((rows, SB_W), F32)])
    return pl.pallas_call(
        functools.partial(_sb_sample_kernel, pp=pp), grid_spec=grid_spec,
        out_shape=jax.ShapeDtypeStruct((db, DEC_S, SB_W), BF16),
        compiler_params=_params(("arbitrary",) * 2), name="sb_sample")(
            page_table, qbd, new_kv, _later_matrix(PAGE), *([cache] * pp))


def _diff_sample_kernel(pt_ref, q_ref, new_ref, cvec_ref, blast_ref, bnew_ref, lam_ref, sg_ref, *rest,
                        pp, lam_init):
    pages, o_ref, m_ref, l_ref, acc_ref = rest[:pp], rest[pp], rest[pp + 1], rest[pp + 2], rest[pp + 3]
    s = pl.program_id(1)
    q = q_ref[0]
    rows = DIFF_HEADS * 2 * DEC_S

    @pl.when(s == 0)
    def _():
        k = new_ref[0, :, :DIFF_KV_W]
        v = new_ref[0, :, DIFF_KV_W:]
        valid = _iota((rows, PAGE), 1) <= _token_of_row((rows, PAGE))
        m, l, acc = _softmax_step(_dot_nt(q, k) + bnew_ref[...], valid, jnp.full((rows, 1), NEG, F32),
                                  jnp.zeros((rows, 1), F32), jnp.zeros((rows, DIFF_KV_W), F32), v)
        m_ref[...] = m
        l_ref[...] = l
        acc_ref[...] = acc

    m, l, acc = m_ref[...], l_ref[...], acc_ref[...]
    for p in range(pp):
        page = pages[p][...]
        k = page[:, :DIFF_KV_W].astype(BF16)
        v = page[:, DIFF_KV_W:].astype(BF16)
        bias = cvec_ref[...]
        if p == 0:
            bias = jnp.where(s == 0, blast_ref[...], bias)
        m, l, acc = _softmax_step(_dot_nt(q, k) + bias, None, m, l, acc, v)
    m_ref[...] = m
    l_ref[...] = l
    acc_ref[...] = acc

    @pl.when(s == pl.num_programs(1) - 1)
    def _():
        a = acc_ref[...]
        ls = l_ref[...]
        lam = _diff_lambda(lam_ref, lam_init)
        width = 2 * HEAD_DIM
        outs = []
        for g in range(DIFF_KV_HEADS):
            for r in range(DIFF_REP):
                r0 = (g * DIFF_REP + r) * 2 * DEC_S
                o1 = a[r0:r0 + DEC_S, g * width:(g + 1) * width]
                o2 = a[r0 + DEC_S:r0 + 2 * DEC_S, g * width:(g + 1) * width]
                outs.append(_diff_finish(o1, ls[r0:r0 + DEC_S], o2, ls[r0 + DEC_S:r0 + 2 * DEC_S],
                                         lam, sg_ref[...], lam_init))
        o_ref[0] = jnp.concatenate(outs, axis=-1).astype(o_ref.dtype)


def diff_sample(page_table, qbd, new_kv, cvec, blast, bnew, lam_rows, sub_g, cache, layer, lam_init):
    db, n_pages = page_table.shape
    pp = min(DIFF_PP, n_pages)
    rows = DIFF_HEADS * 2 * DEC_S
    const = lambda shape: pl.BlockSpec(shape, lambda b, s, pt: (0,) * len(shape))
    grid_spec = pltpu.PrefetchScalarGridSpec(
        num_scalar_prefetch=1, grid=(db, n_pages // pp),
        in_specs=[pl.BlockSpec((1, rows, DIFF_KV_W), lambda b, s, pt: (b, 0, 0)),
                  pl.BlockSpec((1, PAGE, 2 * DIFF_KV_W), lambda b, s, pt: (b, 0, 0)),
                  const((rows, 1)), const((rows, PAGE)), const((rows, PAGE)),
                  const((4, HEAD_DIM)), const((1, 2 * HEAD_DIM))]
        + _page_specs(pp, n_pages, 2 * DIFF_KV_W, 0, layer),
        out_specs=pl.BlockSpec((1, DEC_S, DIFF_HEADS * 2 * HEAD_DIM), lambda b, s, pt: (b, 0, 0)),
        scratch_shapes=[pltpu.VMEM((rows, 1), F32), pltpu.VMEM((rows, 1), F32),
                        pltpu.VMEM((rows, DIFF_KV_W), F32)])
    return pl.pallas_call(
        functools.partial(_diff_sample_kernel, pp=pp, lam_init=lam_init), grid_spec=grid_spec,
        out_shape=jax.ShapeDtypeStruct((db, DEC_S, DIFF_HEADS * 2 * HEAD_DIM), BF16),
        compiler_params=_params(("arbitrary",) * 2), name="diff_sample")(
            page_table, qbd, new_kv, cvec, blast, bnew, lam_rows, sub_g, *([cache] * pp))


NSA_ROWS = NSA_HEADS * DEC_S
NSA_GT = NSA_GROUPS * DEC_S


def _nsa_diag(a):
    pieces = []
    for r in range(NSA_REP):
        for g in range(NSA_GROUPS):
            r0 = r * NSA_GT + g * DEC_S
            pieces.append(a[r0:r0 + DEC_S, g * HEAD_DIM:(g + 1) * HEAD_DIM])
    return jnp.concatenate(pieces, axis=0)


def _nsa_sample_cw_kernel(q_ref, kcvc_ref, bc_ref, win_ref, bw_ref, oc_ref, ow_ref, sel_ref,
                          impt_ref, score_ref, *, win_len, n_keep):
    q = q_ref[0]
    nc = kcvc_ref.shape[1]
    ns = nc // 2
    kc = kcvc_ref[0, :, :NSA_KV_W].astype(BF16)
    vc = kcvc_ref[0, :, NSA_KV_W:].astype(BF16)
    p = _softmax_full(_dot_nt(q, kc) + bc_ref[...], None)
    oc_ref[0] = _nsa_diag(_dot(p.astype(BF16), vc))
    imp = p[0:NSA_GT]
    for r in range(1, NSA_REP):
        imp = imp + p[r * NSA_GT:(r + 1) * NSA_GT]
    imp = jnp.concatenate([imp, jnp.zeros((LANES - NSA_GT, nc), F32)], axis=0)
    impt_ref[...] = imp.T
    imp_s = impt_ref[pl.ds(0, ns, stride=2), :] + impt_ref[pl.ds(1, ns, stride=2), :]
    blk = _iota((ns, LANES), 0)
    score_ref[...] = jnp.where(blk == 0, FORCE, imp_s)
    sel_t = jnp.where(_select_blocks(score_ref, ns, n_keep), 1.0, 0.0)
    sel_ref[0] = sel_t.T[:NSA_GT].astype(sel_ref.dtype)

    wl = win_ref.shape[1]
    kw = win_ref[0, :, :NSA_KV_W]
    vw = win_ref[0, :, NSA_KV_W:]
    tok = _token_of_row((NSA_ROWS, wl))
    col = _iota((NSA_ROWS, wl), 1)
    valid = (col > tok + (win_len - WINDOW)) & (col <= tok + win_len) & (col < win_len + DEC_S)
    pw = _softmax_full(_dot_nt(q, kw) + bw_ref[...], valid)
    ow_ref[0] = _nsa_diag(_dot(pw.astype(BF16), vw))


def nsa_sample_cw(qbd, kcvc, bias_c, win_all, bias_w, win_len, n_keep):
    db = qbd.shape[0]
    nc = kcvc.shape[1]
    wl = win_all.shape[1]
    const = lambda shape: pl.BlockSpec(shape, lambda b: (0,) * len(shape))
    return pl.pallas_call(
        functools.partial(_nsa_sample_cw_kernel, win_len=win_len, n_keep=n_keep), grid=(db,),
        in_specs=[pl.BlockSpec((1, NSA_ROWS, NSA_KV_W), lambda b: (b, 0, 0)),
                  pl.BlockSpec((1, nc, 2 * NSA_KV_W), lambda b: (b, 0, 0)),
                  const((NSA_ROWS, nc)),
                  pl.BlockSpec((1, wl, 2 * NSA_KV_W), lambda b: (b, 0, 0)),
                  const((NSA_ROWS, wl))],
        out_specs=[pl.BlockSpec((1, NSA_ROWS, HEAD_DIM), lambda b: (b, 0, 0)),
                   pl.BlockSpec((1, NSA_ROWS, HEAD_DIM), lambda b: (b, 0, 0)),
                   pl.BlockSpec((1, NSA_GT, nc // 2), lambda b: (b, 0, 0))],
        out_shape=[jax.ShapeDtypeStruct((db, NSA_ROWS, HEAD_DIM), F32),
                   jax.ShapeDtypeStruct((db, NSA_ROWS, HEAD_DIM), F32),
                   jax.ShapeDtypeStruct((db, NSA_GT, nc // 2), BF16)],
        scratch_shapes=[pltpu.VMEM((nc, LANES), F32), pltpu.VMEM((nc // 2, LANES), F32)],
        compiler_params=_params(("arbitrary",)), name="nsa_sample_cw")(qbd, kcvc, bias_c, win_all, bias_w)


def _nsa_sample_sel_kernel(pt_ref, q_ref, new_ref, sel_ref, cvec_ref, blast_ref, bnew_ref, oc_ref, ow_ref,
                           gate_ref, *rest, pp, n_pages):
    pages, o_ref, m_ref, l_ref, acc_ref = rest[:pp], rest[pp], rest[pp + 1], rest[pp + 2], rest[pp + 3]
    s = pl.program_id(1)
    q = q_ref[0]
    sel = sel_ref[0]
    ns = sel.shape[1]

    @pl.when(s == 0)
    def _():
        k = new_ref[0, :, :NSA_KV_W]
        v = new_ref[0, :, NSA_KV_W:]
        valid = _iota((NSA_ROWS, PAGE), 1) <= _token_of_row((NSA_ROWS, PAGE))
        m, l, acc = _softmax_step(_dot_nt(q, k) + bnew_ref[...], valid, jnp.full((NSA_ROWS, 1), NEG, F32),
                                  jnp.zeros((NSA_ROWS, 1), F32), jnp.zeros((NSA_ROWS, NSA_KV_W), F32), v)
        m_ref[...] = m
        l_ref[...] = l
        acc_ref[...] = acc

    m, l, acc = m_ref[...], l_ref[...], acc_ref[...]
    blk_e = _iota((ns, PAGE), 0)
    col_e = _iota((ns, PAGE), 1) // SEL_BLOCK
    for p in range(pp):
        logical = n_pages - 1 - (s * pp + p)
        page = pages[p][...]
        k = page[:, :NSA_KV_W].astype(BF16)
        v = page[:, NSA_KV_W:].astype(BF16)
        expand = jnp.where(blk_e == col_e + logical * (PAGE // SEL_BLOCK), 1.0, 0.0).astype(BF16)
        chosen = _dot(sel, expand) > 0.5
        valid = jnp.concatenate([chosen] * NSA_REP, axis=0)
        bias = cvec_ref[...]
        if p == 0:
            bias = jnp.where(s == 0, blast_ref[...], bias)
        m, l, acc = _softmax_step(_dot_nt(q, k) + bias, valid, m, l, acc, v)
    m_ref[...] = m
    l_ref[...] = l
    acc_ref[...] = acc

    @pl.when(s == pl.num_programs(1) - 1)
    def _():
        o_s = _nsa_diag(acc_ref[...] / jnp.maximum(l_ref[...], 1e-30))
        gate = jax.nn.sigmoid(gate_ref[0].astype(F32))
        o = gate[:, 0:1] * oc_ref[0] + gate[:, 1:2] * o_s + gate[:, 2:3] * ow_ref[0]
        pieces = []
        for g in range(NSA_GROUPS):
            for r in range(NSA_REP):
                r0 = r * NSA_GT + g * DEC_S
                pieces.append(o[r0:r0 + DEC_S])
        o_ref[0] = jnp.concatenate(pieces, axis=-1).astype(o_ref.dtype)


def nsa_sample_sel(page_table, qbd, new_kv, sel, cvec, blast, bnew, o_c, o_w, gates, cache, layer):
    db, n_pages = page_table.shape
    pp = min(SEL_PP, n_pages)
    ns = sel.shape[2]
    const = lambda shape: pl.BlockSpec(shape, lambda b, s, pt: (0,) * len(shape))
    per_seq = lambda shape: pl.BlockSpec((1,) + shape, lambda b, s, pt: (b,) + (0,) * len(shape))
    grid_spec = pltpu.PrefetchScalarGridSpec(
        num_scalar_prefetch=1, grid=(db, n_pages // pp),
        in_specs=[per_seq((NSA_ROWS, NSA_KV_W)), per_seq((PAGE, 2 * NSA_KV_W)), per_seq((NSA_GT, ns)),
                  const((NSA_ROWS, 1)), const((NSA_ROWS, PAGE)), const((NSA_ROWS, PAGE)),
                  per_seq((NSA_ROWS, HEAD_DIM)), per_seq((NSA_ROWS, HEAD_DIM)), per_seq((NSA_ROWS, LANES))]
        + _page_specs(pp, n_pages, 2 * NSA_KV_W, 1, layer),
        out_specs=pl.BlockSpec((1, DEC_S, NSA_HEADS * HEAD_DIM), lambda b, s, pt: (b, 0, 0)),
        scratch_shapes=[pltpu.VMEM((NSA_ROWS, 1), F32), pltpu.VMEM((NSA_ROWS, 1), F32),
                        pltpu.VMEM((NSA_ROWS, NSA_KV_W), F32)])
    return pl.pallas_call(
        functools.partial(_nsa_sample_sel_kernel, pp=pp, n_pages=n_pages), grid_spec=grid_spec,
        out_shape=jax.ShapeDtypeStruct((db, DEC_S, NSA_HEADS * HEAD_DIM), BF16),
        compiler_params=_params(("arbitrary",) * 2), name="nsa_sample_sel")(
            page_table, qbd, new_kv, sel, cvec, blast, bnew, o_c, o_w, gates, *([cache] * pp))


def _layer_norm(y, g, b):
    mu = jnp.mean(y, axis=-1, keepdims=True)
    d = y - mu
    var = jnp.mean(d * d, axis=-1, keepdims=True)
    return d * lax.rsqrt(var + LN_EPS) * g + b


def _split_bf16(x):
    hi = x.astype(BF16)
    return hi, (x - hi.astype(F32)).astype(BF16)


def _mix_router_kernel(a1_ref, a2_ref, w1_ref, w2_ref, x_ref, g_ref, b_ref, wr_ref, br_ref, u_ref,
                       xn_ref, xb_ref, rt_ref, gt_ref, cnt_ref, *, alpha):
    mix = _dot(a1_ref[...], w1_ref[...]) + _dot(a2_ref[...], w2_ref[...])
    xn = _layer_norm(alpha * x_ref[...] + mix, g_ref[...], b_ref[...])
    xn_ref[...] = xn
    xb_ref[...] = xn.astype(BF16)
    xh, xl = _split_bf16(xn)
    wh, wl = _split_bf16(wr_ref[...])
    logits = _dot_nt(wh, xh) + _dot_nt(wh, xl) + _dot_nt(wl, xh) + br_ref[...]
    e_iota = _iota(logits.shape, 0)
    work = logits
    vals, sels = [], []
    for _ in range(TOP_K):
        top = jnp.max(work, axis=0, keepdims=True)
        idx = jnp.min(jnp.where(work == top, e_iota, N_EXPERTS), axis=0, keepdims=True)
        sel = e_iota == idx
        vals.append(top)
        sels.append(sel)
        work = jnp.where(sel, -jnp.inf, work)
    ex = [jnp.exp(v - vals[0]) for v in vals]
    den = ex[0] + ex[1] + ex[2] + ex[3]
    gate = jnp.zeros(logits.shape, F32)
    chosen = sels[0]
    for k in range(TOP_K):
        gate = jnp.where(sels[k], ex[k] / den, gate)
        chosen = chosen | sels[k]
    cf = jnp.where(chosen, 1.0, 0.0)
    rank = _dot(cf.astype(BF16), u_ref[...])
    rt_ref[0] = jnp.where(chosen, rank, -1.0)
    gt_ref[0] = gate
    cnt = jnp.sum(cf, axis=1, keepdims=True)
    cnt_ref[0] = jnp.broadcast_to(cnt, (N_EXPERTS, LANES)).astype(jnp.int32)


def mix_router(a1, a2, w1, w2, x, ln_g, ln_b, w_router_t, b_router, tt, alpha):
    n = x.shape[0]
    half = a1.shape[1]
    j = np.arange(tt)
    before = jnp.asarray((j[:, None] < j[None, :]).astype(np.float32), dtype=BF16)
    tile = lambda w: pl.BlockSpec((tt, w), lambda i: (i, 0))
    const = lambda shape: pl.BlockSpec(shape, lambda i: (0,) * len(shape))
    route = pl.BlockSpec((1, N_EXPERTS, tt), lambda i: (i, 0, 0))
    return pl.pallas_call(
        functools.partial(_mix_router_kernel, alpha=alpha), grid=(n // tt,),
        in_specs=[tile(half), tile(half), const((half, D_MODEL)), const((half, D_MODEL)), tile(D_MODEL),
                  const((1, D_MODEL)), const((1, D_MODEL)), const((N_EXPERTS, D_MODEL)),
                  const((N_EXPERTS, 1)), const((tt, tt))],
        out_specs=[tile(D_MODEL), tile(D_MODEL), route, route,
                   pl.BlockSpec((1, N_EXPERTS, LANES), lambda i: (i, 0, 0))],
        out_shape=[jax.ShapeDtypeStruct((n, D_MODEL), F32), jax.ShapeDtypeStruct((n, D_MODEL), BF16),
                   jax.ShapeDtypeStruct((n // tt, N_EXPERTS, tt), F32),
                   jax.ShapeDtypeStruct((n // tt, N_EXPERTS, tt), F32),
                   jax.ShapeDtypeStruct((n // tt, N_EXPERTS, LANES), jnp.int32)],
        compiler_params=_params(("arbitrary",)), name="mix_router")(
            a1, a2, w1, w2, x, ln_g, ln_b, w_router_t, b_router, before)


def _moe_kernel(cnt_ref, xb_ref, xn_ref, rt_ref, gt_ref, wg_ref, wl_ref, bg_ref, bl_ref, wd_ref, bd_ref,
                lg_ref, lb_ref, y_ref, yb_ref, acc_ref, *, alpha):
    i = pl.program_id(0)
    e = pl.program_id(1)
    tt = xb_ref.shape[0]

    @pl.when(e == 0)
    def _():
        acc_ref[...] = jnp.zeros_like(acc_ref)

    n_chunks = (cnt_ref[i, e] + MOE_CH - 1) // MOE_CH
    slot = rt_ref[0, pl.ds(e, 1), :]
    gate = gt_ref[0, pl.ds(e, 1), :]

    def chunk(c, _):
        want = (_iota((MOE_CH, tt), 0) + c * MOE_CH).astype(F32)
        hit = slot == want
        xe = _dot(jnp.where(hit, 1.0, 0.0).astype(BF16), xb_ref[...]).astype(BF16)
        h_glu = jnp.minimum(_dot(xe, wg_ref[0]) + bg_ref[0], SWIGLU_LIMIT)
        h_lin = jnp.clip(_dot(xe, wl_ref[0]) + bl_ref[0], -SWIGLU_LIMIT, SWIGLU_LIMIT)
        a = h_glu * jax.nn.sigmoid(SWIGLU_ALPHA * h_glu) * (h_lin + 1.0)
        y = _dot(a.astype(BF16), wd_ref[0]) + bd_ref[0]
        back = jnp.where(hit, gate, 0.0).astype(BF16)
        acc_ref[...] += _dot_tn(back, y.astype(BF16))
        return 0

    lax.fori_loop(0, n_chunks, chunk, 0)

    @pl.when(e == pl.num_programs(1) - 1)
    def _():
        y = _layer_norm(alpha * xn_ref[...] + acc_ref[...], lg_ref[...], lb_ref[...])
        y_ref[...] = y
        yb_ref[...] = y.astype(BF16)


def moe_ln(cnt, xb, xn, rt, gt, w_glu, w_lin, b_glu, b_lin, w_down, b_down, ln_g, ln_b, tt, alpha):
    n = xn.shape[0]
    ff = w_glu.shape[2]
    tile = pl.BlockSpec((tt, D_MODEL), lambda i, e, c: (i, 0))
    route = pl.BlockSpec((1, N_EXPERTS, tt), lambda i, e, c: (i, 0, 0))
    per_e = lambda shape: pl.BlockSpec((1,) + shape, lambda i, e, c: (e, 0, 0))
    const = pl.BlockSpec((1, D_MODEL), lambda i, e, c: (0, 0))
    grid_spec = pltpu.PrefetchScalarGridSpec(
        num_scalar_prefetch=1, grid=(n // tt, N_EXPERTS),
        in_specs=[tile, tile, route, route, per_e((D_MODEL, ff)), per_e((D_MODEL, ff)), per_e((1, ff)),
                  per_e((1, ff)), per_e((ff, D_MODEL)), per_e((1, D_MODEL)), const, const],
        out_specs=[tile, tile],
        scratch_shapes=[pltpu.VMEM((tt, D_MODEL), F32)])
    return pl.pallas_call(
        functools.partial(_moe_kernel, alpha=alpha), grid_spec=grid_spec,
        out_shape=[jax.ShapeDtypeStruct((n, D_MODEL), F32), jax.ShapeDtypeStruct((n, D_MODEL), BF16)],
        compiler_params=_params(("arbitrary",) * 2), name="moe_ln")(
            cnt, xb, xn, rt, gt, w_glu, w_lin, b_glu, b_lin, w_down, b_down, ln_g, ln_b)


MOE_TILE = 1024


def _front_pad(a, axis):
    pad = [(0, 0)] * a.ndim
    pad[axis] = (KPAD, 0)
    return jnp.pad(a, pad)


def _pad_rows(a, rows):
    return jnp.pad(a, ((0, 0), (0, rows - a.shape[1]), (0, 0)))


def _channel_mixer(layer, a1, a2, w_out, x, p, tt):
    alpha = p["alpha"]
    half = a1.shape[1]
    w_out = w_out.astype(BF16)
    xn, xb, rt, gt, cnt = mix_router(
        a1, a2, w_out[:half], w_out[half:], x, p["ln_mix_g"][layer][None], p["ln_mix_b"][layer][None],
        p["w_router"][layer].T.astype(F32), p["b_router"][layer][:, None].astype(F32), tt, alpha)
    return moe_ln(cnt[:, :, 0], xb, xn, rt, gt, *p["experts"][layer],
                  p["ln_ffn_g"][layer][None], p["ln_ffn_b"][layer][None], tt, alpha)


def _even_prompt(h, b, t, cmp_w, bias_c, bias0, c31):
    h3 = h.reshape(b, t, -1)
    heads = lambda a, nh: a.reshape(b, t, nh, HEAD_DIM).transpose(0, 2, 1, 3).astype(BF16)
    o = 3 * SB_W
    o_sb = sb_prompt(heads(h3[..., :SB_W], SB_HEADS),
                     _front_pad(heads(h3[..., SB_W:2 * SB_W], SB_HEADS), 2),
                     _front_pad(heads(h3[..., 2 * SB_W:o], SB_HEADS), 2))
    o_sb = o_sb.transpose(0, 2, 1, 3).reshape(b * t, SB_W)
    nq = h3[..., o:o + NSA_Q_W].reshape(b, t, NSA_GROUPS, NSA_REP, HEAD_DIM).transpose(0, 2, 3, 1, 4)
    o += NSA_Q_W
    nkv = h3[..., o:o + 6 * NSA_KV_W].reshape(b, t, 6, NSA_GROUPS, HEAD_DIM)
    gates = h3[..., o + 6 * NSA_KV_W:o + 6 * NSA_KV_W + NSA_GATE_W]
    gates = gates.reshape(b, t, NSA_GROUPS, NSA_REP * 3).transpose(0, 2, 1, 3)
    gates = jnp.pad(gates, ((0, 0), (0, 0), (0, 0), (0, LANES - NSA_REP * 3)))
    n_pages = t // PAGE
    table = jnp.arange(b * n_pages, dtype=jnp.int32).reshape(b, n_pages)
    kcvc = compress_pages(table, h.reshape(b * n_pages, 1, PAGE, h.shape[-1]), 0, o // (2 * HEAD_DIM), *cmp_w)
    nc = t // CMP_BLOCK
    ncp = -(-nc // LANES) * LANES
    kcvc = kcvc.reshape(b, nc, 2, NSA_GROUPS, HEAD_DIM).transpose(2, 0, 3, 1, 4).astype(BF16)
    kcvc = jnp.pad(kcvc, ((0, 0), (0, 0), (0, 0), (0, ncp - nc), (0, 0)))
    grp = lambda i: _front_pad(nkv[:, :, i].transpose(0, 2, 1, 3).astype(BF16), 2)
    o_nsa = nsa_prompt(nq.astype(BF16), kcvc[0], kcvc[1], grp(2), grp(3), grp(4), grp(5),
                       bias_c, bias0, c31, gates, -(-t // SEL_BLOCK))
    sb_rows = h3[..., SB_W:3 * SB_W].reshape(b, t, 2, SB_HEADS, HEAD_DIM)
    keep = min(WINDOW, t)
    return (o_sb, o_nsa.reshape(b * t, NSA_Q_W), sb_rows, nkv[:, :, :4], nkv[:, t - keep:, 4:])


def _even_sample(h, db, page_table, cache_sb, cache_nsa, win_state, layer, cmp_w, tabs):
    h3 = h.reshape(db, DEC_S, -1)
    past = page_table.shape[1] * PAGE
    o = 3 * SB_W
    q = h3[..., :SB_W].reshape(db, DEC_S, SB_HEADS, HEAD_DIM).transpose(0, 2, 1, 3) * ATT_SCALE
    eye = jnp.eye(SB_HEADS, dtype=F32)
    qbd = (q[:, :, :, None, :] * eye[None, :, None, :, None]).reshape(db, SB_HEADS * DEC_S, SB_W)
    o_sb = sb_sample(page_table, qbd.astype(BF16), _pad_rows(h3[..., SB_W:o], PAGE).astype(BF16),
                     cache_sb, layer)
    q = h3[..., o:o + NSA_Q_W].reshape(db, DEC_S, NSA_GROUPS, NSA_REP, HEAD_DIM).transpose(0, 3, 2, 1, 4)
    eye = jnp.eye(NSA_GROUPS, dtype=F32)
    qbd = (q[:, :, :, :, None, :] * ATT_SCALE * eye[None, None, :, None, :, None])
    qbd = qbd.reshape(db, NSA_ROWS, NSA_KV_W).astype(BF16)
    o += NSA_Q_W
    nkv = h3[..., o:o + 6 * NSA_KV_W]
    gates = h3[..., o + 6 * NSA_KV_W:o + 6 * NSA_KV_W + NSA_GATE_W]
    gates = gates.reshape(db, DEC_S, NSA_GROUPS, NSA_REP, 3).transpose(0, 3, 2, 1, 4).reshape(db, NSA_ROWS, 3)
    gates = jnp.pad(gates, ((0, 0), (0, 0), (0, LANES - 3)))
    kcvc = compress_pages(page_table, cache_nsa, layer, 0, *cmp_w)
    win_len = win_state.shape[2]
    new_win = nkv[..., 4 * NSA_KV_W:]
    win_all = jnp.concatenate([win_state[:, layer].reshape(db, win_len, 2 * NSA_KV_W), new_win], axis=1)
    wl = tabs["bias_w"].shape[1]
    n_blocks = -(-(past + DEC_S) // SEL_BLOCK)
    o_c, o_w, sel = nsa_sample_cw(qbd, kcvc, tabs["bias_c"], _pad_rows(win_all, wl).astype(BF16),
                                  tabs["bias_w"], win_len, min(N_SEL, n_blocks) - 1)
    o_nsa = nsa_sample_sel(page_table, qbd, _pad_rows(nkv[..., 2 * NSA_KV_W:4 * NSA_KV_W], PAGE).astype(BF16),
                           sel, tabs["cvec"], tabs["blast"], tabs["bnew"], o_c, o_w, gates, cache_nsa, layer)
    sb_rows = h3[..., SB_W:3 * SB_W].reshape(db, DEC_S, 2, SB_HEADS, HEAD_DIM)
    nsa_rows = nkv[..., :4 * NSA_KV_W].reshape(db, DEC_S, 4, NSA_GROUPS, HEAD_DIM)
    win_rows = win_all[:, max(0, win_len + DEC_S - WINDOW):].reshape(db, -1, 2, NSA_GROUPS, HEAD_DIM)
    return (o_sb.reshape(db * DEC_S, SB_W), o_nsa.reshape(db * DEC_S, NSA_Q_W), sb_rows, nsa_rows, win_rows)


def _odd_prompt(h, b, t, bias0, c31, lam_rows, sub_g, lam_init):
    h3 = h.reshape(b, t, -1)
    q = h3[..., :DIFF_Q_W].reshape(b, t, DIFF_KV_HEADS, DIFF_REP, 2, HEAD_DIM).transpose(0, 2, 3, 4, 1, 5)
    kv = h3[..., DIFF_Q_W:].reshape(b, t, 2, DIFF_KV_HEADS, 2 * HEAD_DIM)
    k = kv[:, :, 0].reshape(b, t, DIFF_KV_HEADS, 2, HEAD_DIM).transpose(0, 2, 3, 1, 4)
    v = kv[:, :, 1].transpose(0, 2, 1, 3)
    o = diff_prompt(q.astype(BF16), _front_pad(k.astype(BF16), 3), _front_pad(v.astype(BF16), 2),
                    bias0, c31, lam_rows, sub_g, lam_init)
    return o.reshape(b * t, DIFF_Q_W), kv


def _odd_sample(h, db, page_table, cache, layer, tabs, lam_rows, sub_g, lam_init):
    h3 = h.reshape(db, DEC_S, -1)
    q = h3[..., :DIFF_Q_W].reshape(db, DEC_S, DIFF_KV_HEADS, DIFF_REP, 2, HEAD_DIM).transpose(0, 2, 3, 4, 1, 5)
    eye_g = jnp.eye(DIFF_KV_HEADS, dtype=F32)
    eye_c = jnp.eye(2, dtype=F32)
    qbd = (q[..., None, None, :] * ATT_SCALE * eye_g[None, :, None, None, None, :, None, None]
           * eye_c[None, None, None, :, None, None, :, None])
    qbd = qbd.reshape(db, DIFF_HEADS * 2 * DEC_S, DIFF_KV_W).astype(BF16)
    o = diff_sample(page_table, qbd, _pad_rows(h3[..., DIFF_Q_W:], PAGE).astype(BF16), tabs["cvec"],
                    tabs["blast"], tabs["bnew"], lam_rows, sub_g, cache, layer, lam_init)
    kv = h3[..., DIFF_Q_W:].reshape(db, DEC_S, 2, DIFF_KV_HEADS, 2 * HEAD_DIM)
    return o.reshape(db * DEC_S, DIFF_Q_W), kv


def _sample_tables(rel_bias, past, head_of_row, tok_of_row, with_cmp):
    tok = np.asarray(tok_of_row)
    key = np.arange(PAGE)
    tabs = {
        "cvec": rel_bias.astype(F32)[N_BUCKETS - 1][np.asarray(head_of_row)][:, None],
        "blast": _bias_table(rel_bias, tok[:, None] + PAGE - key[None, :], head_of_row),
        "bnew": _bias_table(rel_bias, tok[:, None] - key[None, :], head_of_row),
    }
    if with_cmp:
        nc = past // CMP_BLOCK
        c_end = np.arange(nc) * CMP_BLOCK + CMP_BLOCK - 1
        tabs["bias_c"] = _bias_table(rel_bias, past + tok[:, None] - c_end[None, :], head_of_row)
        win_len = min(WINDOW, past)
        wl = -(-(win_len + DEC_S) // LANES) * LANES
        tabs["bias_w"] = _bias_table(rel_bias, tok[:, None] + win_len - np.arange(wl)[None, :], head_of_row)
    return tabs


def kernel(x_prompt, x_sample, cache_sb_kv, cache_nsa_kv, cache_diff_kv, state_nsa_win, page_table, rel_bias,
           even_w_in, even_cmp_pe, even_cmp_wk, even_cmp_wv, even_w_out, odd_w_in, odd_lambda, odd_subln_g,
           odd_w_out, ln_mix_g, ln_mix_b, ln_ffn_g, ln_ffn_b, moe_w_router, moe_b_router, moe_w_up, moe_b_up,
           moe_w_down, moe_b_down):
    b, t, d = x_prompt.shape
    db, s, _ = x_sample.shape
    depth = ln_mix_g.shape[0]
    n_pool = cache_sb_kv.shape[0]
    past = page_table.shape[1] * PAGE
    assert s == DEC_S and d == D_MODEL and t % QB == 0 and (b * t) % MOE_TILE == 0
    assert past % (SEL_BLOCK * LANES) == 0 and state_nsa_win.shape[2] == WINDOW

    params = {
        "alpha": (2.0 * depth) ** 0.25,
        "ln_mix_g": ln_mix_g.astype(F32), "ln_mix_b": ln_mix_b.astype(F32),
        "ln_ffn_g": ln_ffn_g.astype(F32), "ln_ffn_b": ln_ffn_b.astype(F32),
        "w_router": moe_w_router, "b_router": moe_b_router,
        "experts": [(moe_w_up[l][:, :, 0::2].astype(BF16), moe_w_up[l][:, :, 1::2].astype(BF16),
                     moe_b_up[l][:, None, 0::2].astype(F32), moe_b_up[l][:, None, 1::2].astype(F32),
                     moe_w_down[l].astype(BF16), moe_b_down[l][:, None, :].astype(F32)) for l in range(depth)],
    }
    c31 = rel_bias.astype(F32)[N_BUCKETS - 1]
    bias0 = _bias_table(rel_bias, np.arange(QB)[:, None] + KPAD - np.arange(KT)[None, :])
    nc = t // CMP_BLOCK
    ncp = -(-nc // LANES) * LANES
    bias_c = _bias_table(rel_bias, np.arange(t)[:, None] - (np.arange(ncp)[None, :] * CMP_BLOCK + CMP_BLOCK - 1))
    nsa_rows = [(r, g, tk) for r in range(NSA_REP) for g in range(NSA_GROUPS) for tk in range(DEC_S)]
    nsa_tabs = _sample_tables(rel_bias, past, [g * NSA_REP + r for r, g, _ in nsa_rows],
                              [tk for _, _, tk in nsa_rows], True)
    diff_rows = [(g, r, tk) for g in range(DIFF_KV_HEADS) for r in range(DIFF_REP) for _ in range(2)
                 for tk in range(DEC_S)]
    diff_tabs = _sample_tables(rel_bias, past, [g * DIFF_REP + r for g, r, _ in diff_rows],
                               [tk for _, _, tk in diff_rows], False)

    cache_sb = cache_sb_kv.reshape(n_pool, -1, PAGE, 2 * SB_W)
    cache_nsa = cache_nsa_kv.reshape(n_pool, -1, PAGE, 4 * NSA_KV_W)
    cache_diff = cache_diff_kv.reshape(n_pool, -1, PAGE, 2 * DIFF_KV_W)

    xp = x_prompt.reshape(b * t, d).astype(F32)
    xs = x_sample.reshape(db * s, d).astype(F32)
    xpb, xsb = xp.astype(BF16), xs.astype(BF16)
    outs = {k: [] for k in ("sb_p", "sb_s", "nsa_p", "nsa_s", "win_p", "win_s", "diff_p", "diff_s")}
    for l in range(depth):
        j = l // 2
        if l % 2 == 0:
            w_in = jnp.pad(even_w_in[j], ((0, 0), (0, EVEN_IN_PAD - EVEN_IN))).astype(BF16)
            cmp_w = _compress_weights(even_cmp_pe[j], even_cmp_wk[j], even_cmp_wv[j])
            a1, a2, r_sb, r_nsa, r_win = _even_prompt(matmul(xpb, w_in, 512), b, t, cmp_w, bias_c, bias0, c31)
            s1, s2, s_sb, s_nsa, s_win = _even_sample(matmul(xsb, w_in, 512), db, page_table, cache_sb,
                                                      cache_nsa, state_nsa_win, j, cmp_w, nsa_tabs)
            w_out = even_w_out[j]
            for key, val in (("sb_p", r_sb), ("sb_s", s_sb), ("nsa_p", r_nsa), ("nsa_s", s_nsa),
                             ("win_p", r_win), ("win_s", s_win)):
                outs[key].append(val)
        else:
            lam_init = 0.8 - 0.6 * math.exp(-0.3 * l)
            w_in = odd_w_in[j].astype(BF16)
            lam_rows = odd_lambda[j].astype(F32)
            sub_g = odd_subln_g[j][None].astype(F32)
            ap, r_diff = _odd_prompt(matmul(xpb, w_in, 512), b, t, bias0, c31, lam_rows, sub_g, lam_init)
            as_, s_diff = _odd_sample(matmul(xsb, w_in, 512), db, page_table, cache_diff, j, diff_tabs,
                                      lam_rows, sub_g, lam_init)
            half = DIFF_Q_W // 2
            a1, a2, s1, s2 = ap[:, :half], ap[:, half:], as_[:, :half], as_[:, half:]
            w_out = odd_w_out[j]
            outs["diff_p"].append(r_diff)
            outs["diff_s"].append(s_diff)
        xp, xpb = _channel_mixer(l, a1, a2, w_out, xp, params, MOE_TILE)
        xs, xsb = _channel_mixer(l, s1, s2, w_out, xs, params, db * s)
    stack = lambda key: jnp.stack(outs[key], axis=1)
    return (xp.reshape(b, t, d), xs.reshape(db, s, d), stack("sb_p"), stack("sb_s"), stack("nsa_p"),
            stack("nsa_s"), stack("diff_p"), stack("diff_s"), stack("win_p"), stack("win_s"))
```

```python
import functools
import math

import numpy as np
import jax
import jax.numpy as jnp
from jax import lax
from jax.experimental import pallas as pl
from jax.experimental.pallas import tpu as pltpu

F32 = jnp.float32
BF16 = jnp.bfloat16

D_MODEL = 1024
HEAD_DIM = 64
SB_HEADS = 8
NSA_HEADS = 8
NSA_GROUPS = 2
NSA_REP = NSA_HEADS // NSA_GROUPS
CMP_BLOCK = 32
SEL_BLOCK = 64
N_SEL = 16
WINDOW = 512
DIFF_HEADS = 8
DIFF_KV_HEADS = 4
DIFF_REP = DIFF_HEADS // DIFF_KV_HEADS
N_BUCKETS = 32
MAX_DISTANCE = 128
N_EXPERTS = 32
TOP_K = 4
SWIGLU_ALPHA = 1.702
SWIGLU_LIMIT = 7.0
LN_EPS = 1e-5
PAGE = 128
NEG = -1e30
FORCE = 1e9
ATT_SCALE = HEAD_DIM ** -0.5

SB_W = SB_HEADS * HEAD_DIM
NSA_Q_W = NSA_HEADS * HEAD_DIM
NSA_KV_W = NSA_GROUPS * HEAD_DIM
NSA_GATE_W = NSA_HEADS * 3
EVEN_IN = 3 * SB_W + NSA_Q_W + 6 * NSA_KV_W + NSA_GATE_W
EVEN_IN_PAD = -(-EVEN_IN // 128) * 128
DIFF_Q_W = DIFF_HEADS * 2 * HEAD_DIM
DIFF_KV_W = DIFF_KV_HEADS * 2 * HEAD_DIM
ODD_IN = DIFF_Q_W + 2 * DIFF_KV_W

LANES = 128
SUBLANES = 8
VMEM_LIMIT = 52 * 1024 * 1024

QB = 128
KT = 512
KPAD = KT - QB
MOE_CH = 160
DEC_S = 8


def _dot(a, b):
    return jnp.dot(a, b, preferred_element_type=F32)


def _dot_nt(a, b):
    return lax.dot_general(a, b, (((1,), (1,)), ((), ())), preferred_element_type=F32)


def _dot_tn(a, b):
    return lax.dot_general(a, b, (((0,), (0,)), ((), ())), preferred_element_type=F32)


def _iota(shape, dim):
    return lax.broadcasted_iota(jnp.int32, shape, dim)


def _params(sem, vmem=VMEM_LIMIT):
    return pltpu.CompilerParams(dimension_semantics=sem, vmem_limit_bytes=vmem)


def _bucket_np(dist):
    n = np.maximum(dist, 0)
    exact = N_BUCKETS // 2
    nf = np.maximum(n, 1).astype(np.float32)
    large = exact + (np.log(nf / np.float32(exact)) / np.float32(math.log(MAX_DISTANCE / exact))
                     * np.float32(N_BUCKETS - exact)).astype(np.int32)
    return np.where(n < exact, n, np.minimum(large, N_BUCKETS - 1)).astype(np.int32)


def _bias_table(rel_bias, dist, head_of_row=None):
    bucket = _bucket_np(dist)
    rb = rel_bias.astype(F32)
    if head_of_row is None:
        return jnp.transpose(rb[bucket], (2, 0, 1))
    return rb[bucket, np.asarray(head_of_row)[:, None]]


def _mm_kernel(x_ref, w_ref, o_ref):
    o_ref[...] = _dot(x_ref[...], w_ref[...])


def matmul(x, w, tm):
    m, k = x.shape
    n = w.shape[1]
    tm = min(tm, m)
    return pl.pallas_call(
        _mm_kernel, grid=(m // tm,),
        in_specs=[pl.BlockSpec((tm, k), lambda i: (i, 0)), pl.BlockSpec((k, n), lambda i: (0, 0))],
        out_specs=pl.BlockSpec((tm, n), lambda i: (i, 0)),
        out_shape=jax.ShapeDtypeStruct((m, n), F32),
        compiler_params=_params(("arbitrary",)), name="in_proj")(x, w)


def _softmax_step(s, valid, m, l, acc, v):
    if valid is not None:
        s = jnp.where(valid, s, NEG)
    m_new = jnp.maximum(m, jnp.max(s, axis=-1, keepdims=True))
    p = jnp.exp(s - m_new)
    if valid is not None:
        p = jnp.where(valid, p, 0.0)
    alpha = jnp.exp(m - m_new)
    l = alpha * l + jnp.sum(p, axis=-1, keepdims=True)
    acc = alpha * acc + _dot(p.astype(BF16), v)
    return m_new, l, acc


def _softmax_full(s, valid):
    if valid is not None:
        s = jnp.where(valid, s, NEG)
    m = jnp.max(s, axis=-1, keepdims=True)
    e = jnp.exp(s - m)
    if valid is not None:
        e = jnp.where(valid, e, 0.0)
    return e / jnp.maximum(jnp.sum(e, axis=-1, keepdims=True), 1e-30)


def _sb_step(z, valid, carry, acc, v, u):
    sp = jnp.maximum(z, 0.0) + jnp.log(1.0 + jnp.exp(-jnp.abs(z)))
    lk = -sp if valid is None else jnp.where(valid, -sp, 0.0)
    hi = lk.astype(BF16)
    lo = (lk - hi.astype(F32)).astype(BF16)
    later = _dot(hi, u) + _dot(lo, u) + carry
    w = jnp.exp(z - sp + later)
    if valid is not None:
        w = jnp.where(valid, w, 0.0)
    acc = acc + _dot(w.astype(BF16), v)
    carry = carry + jnp.sum(lk, axis=-1, keepdims=True)
    return carry, acc


SB_DEAD = -104.0


def _later_matrix(n):
    j = np.arange(n)
    return jnp.asarray((j[:, None] > j[None, :]).astype(np.float32), dtype=BF16)


def _sb_prompt_kernel(q_ref, k_ref, v_ref, u_ref, o_ref):
    qi = pl.program_id(2)
    q0 = qi * QB
    q = (q_ref[0, 0].astype(F32) * ATT_SCALE).astype(BF16)
    u = u_ref[...]
    row = _iota((QB, KT), 0)
    col = _iota((QB, KT), 1)

    def tile(j, carry, acc, first):
        start = pl.multiple_of(q0 - j * KT, QB)
        k = k_ref[0, 0, pl.ds(start, KT), :]
        v = v_ref[0, 0, pl.ds(start, KT), :]
        z = _dot_nt(q, k)
        valid = col >= KPAD - start
        if first:
            valid = valid & (row + KPAD - col > 0)
        return _sb_step(z, valid, carry, acc, v, u)

    carry, acc = tile(0, jnp.zeros((QB, 1), F32), jnp.zeros((QB, HEAD_DIM), F32), True)
    n_tiles = qi // (KT // QB) + 1

    def more(state):
        j, carry, _ = state
        return (j < n_tiles) & (jnp.max(carry) > SB_DEAD)

    def step(state):
        j, carry, acc = state
        carry, acc = tile(j, carry, acc, False)
        return j + 1, carry, acc

    _, _, acc = lax.while_loop(more, step, (jnp.int32(1), carry, acc))
    o_ref[0, 0] = acc.astype(o_ref.dtype)


def sb_prompt(q, k, v):
    b, h, t, _ = q.shape
    tp = k.shape[2]
    kv_spec = pl.BlockSpec((1, 1, tp, HEAD_DIM), lambda b_, h_, i: (b_, h_, 0, 0))
    return pl.pallas_call(
        _sb_prompt_kernel, grid=(b, h, t // QB),
        in_specs=[pl.BlockSpec((1, 1, QB, HEAD_DIM), lambda b_, h_, i: (b_, h_, i, 0)), kv_spec, kv_spec,
                  pl.BlockSpec((KT, KT), lambda b_, h_, i: (0, 0))],
        out_specs=pl.BlockSpec((1, 1, QB, HEAD_DIM), lambda b_, h_, i: (b_, h_, i, 0)),
        out_shape=jax.ShapeDtypeStruct((b, h, t, HEAD_DIM), BF16),
        compiler_params=_params(("arbitrary",) * 3), name="sb_prompt")(q, k, v, _later_matrix(KT))


def _diff_lambda(lam_ref, lam_init):
    lv = lam_ref[...].astype(F32)
    a = jnp.sum(lv[0:1] * lv[1:2], axis=-1, keepdims=True)
    b = jnp.sum(lv[2:3] * lv[3:4], axis=-1, keepdims=True)
    return jnp.exp(a) - jnp.exp(b) + lam_init


def _diff_finish(o1, l1, o2, l2, lam, sub_g, lam_init):
    a = o1 / jnp.maximum(l1, 1e-30) - lam * (o2 / jnp.maximum(l2, 1e-30))
    a = a * lax.rsqrt(jnp.mean(jnp.square(a), axis=-1, keepdims=True) + LN_EPS)
    return a * sub_g * (1.0 - lam_init)


def _diff_prompt_kernel(c31_ref, q_ref, k_ref, v_ref, b0_ref, lam_ref, sg_ref, o_ref, *, lam_init):
    g = pl.program_id(1)
    qi = pl.program_id(2)
    q0 = qi * QB
    row = _iota((QB, KT), 0)
    col = _iota((QB, KT), 1)
    streams = [(r, c) for r in range(DIFF_REP) for c in range(2)]
    qs = {(r, c): (q_ref[0, 0, r, c].astype(F32) * ATT_SCALE).astype(BF16) for r, c in streams}

    def tile(j, state, first):
        start = pl.multiple_of(q0 - j * KT, QB)
        v = v_ref[0, 0, pl.ds(start, KT), :]
        valid = col >= KPAD - start
        if first:
            valid = valid & (row + KPAD - col >= 0)
        out = []
        for idx, (r, c) in enumerate(streams):
            k = k_ref[0, 0, c, pl.ds(start, KT), :]
            s = _dot_nt(qs[(r, c)], k)
            s = s + (b0_ref[r] if first else c31_ref[g * DIFF_REP + r])
            out.append(_softmax_step(s, valid, *state[idx], v))
        return tuple(out)

    init = tuple((jnp.full((QB, 1), NEG, F32), jnp.zeros((QB, 1), F32),
                  jnp.zeros((QB, 2 * HEAD_DIM), F32)) for _ in streams)
    state = tile(0, init, True)
    state = lax.fori_loop(1, qi // (KT // QB) + 1, lambda j, st: tile(j, st, False), state)
    lam = _diff_lambda(lam_ref, lam_init)
    outs = []
    for r in range(DIFF_REP):
        (_, l1, o1), (_, l2, o2) = state[2 * r], state[2 * r + 1]
        outs.append(_diff_finish(o1, l1, o2, l2, lam, sg_ref[...], lam_init))
    o_ref[0] = jnp.concatenate(outs, axis=-1).astype(o_ref.dtype)


def diff_prompt(q, k, v, bias0, c31, lam_rows, sub_g, lam_init):
    b, gk, _, _, t, _ = q.shape
    tp = k.shape[3]
    return pl.pallas_call(
        functools.partial(_diff_prompt_kernel, lam_init=lam_init), grid=(b, gk, t // QB),
        in_specs=[pl.BlockSpec(memory_space=pltpu.SMEM),
                  pl.BlockSpec((1, 1, DIFF_REP, 2, QB, HEAD_DIM), lambda b_, g_, i: (b_, g_, 0, 0, i, 0)),
                  pl.BlockSpec((1, 1, 2, tp, HEAD_DIM), lambda b_, g_, i: (b_, g_, 0, 0, 0)),
                  pl.BlockSpec((1, 1, tp, 2 * HEAD_DIM), lambda b_, g_, i: (b_, g_, 0, 0)),
                  pl.BlockSpec((DIFF_REP, QB, KT), lambda b_, g_, i: (g_, 0, 0)),
                  pl.BlockSpec((4, HEAD_DIM), lambda b_, g_, i: (0, 0)),
                  pl.BlockSpec((1, 2 * HEAD_DIM), lambda b_, g_, i: (0, 0))],
        out_specs=pl.BlockSpec((1, QB, DIFF_REP * 2 * HEAD_DIM), lambda b_, g_, i: (b_, i, g_)),
        out_shape=jax.ShapeDtypeStruct((b, t, DIFF_HEADS * 2 * HEAD_DIM), BF16),
        compiler_params=_params(("arbitrary",) * 3), name="diff_prompt")(
            c31, q, k, v, bias0, lam_rows, sub_g)


CMP_BAND_BACK = 4
CMP_BAND = 16


def _cmp_band_table(rel_bias):
    m = np.arange(CMP_BAND) - CMP_BAND_BACK
    dist = np.arange(QB)[:, None] - (m[None, :] * CMP_BLOCK + CMP_BLOCK - 1)
    delta = _bias_table(rel_bias, dist) - rel_bias.astype(F32)[N_BUCKETS - 1][:, None, None]
    hi = delta.astype(BF16)
    lo = (delta - hi.astype(F32)).astype(BF16)
    return jnp.stack([hi, lo], axis=1)


def _select_blocks(score_ref, n_rows, n_keep):
    sc = score_ref[...]
    blk = _iota(sc.shape, 0)

    def body(i, rank):
        r = score_ref[pl.ds(i, 1), :]
        better = (r > sc) | ((r == sc) & (i < blk))
        return rank + jnp.where(better, 1.0, 0.0)

    rank = lax.fori_loop(0, n_rows, body, jnp.zeros(sc.shape, F32), unroll=8)
    return rank < n_keep


def _nsa_prompt_kernel(c31_ref, q_ref, kc_ref, vc_ref, ks_ref, vs_ref, kw_ref, vw_ref, bc_ref, b0_ref,
                       gate_ref, o_ref, impt_ref, score_ref, *, n_sel_blocks, n_keep):
    g = pl.program_id(1)
    qi = pl.program_id(2)
    q0 = qi * QB
    n_tiles = qi // (KT // QB) + 1
    nc = kc_ref.shape[2]
    ns = nc // 2
    heads = range(NSA_REP)
    qs = [(q_ref[0, 0, r].astype(F32) * ATT_SCALE).astype(BF16) for r in heads]

    kc = kc_ref[0, 0]
    vc = vc_ref[0, 0]
    qrow = _iota((QB, nc), 0) + q0
    valid_c = qrow - (_iota((QB, nc), 1) * CMP_BLOCK + (CMP_BLOCK - 1)) >= 0
    first_blk = qi * (QB // CMP_BLOCK) - CMP_BAND_BACK
    place = jnp.where(_iota((CMP_BAND, nc), 1) == _iota((CMP_BAND, nc), 0) + first_blk, 1.0, 0.0).astype(BF16)
    o_c = []
    imp = jnp.zeros((QB, nc), F32)
    for r in heads:
        bias = c31_ref[g * NSA_REP + r] + _dot(bc_ref[r, 0], place) + _dot(bc_ref[r, 1], place)
        p = _softmax_full(_dot_nt(qs[r], kc) + bias, valid_c)
        o_c.append(_dot(p.astype(BF16), vc))
        imp = imp + p
    impt_ref[...] = imp.T
    imp_s = impt_ref[pl.ds(0, ns, stride=2), :] + impt_ref[pl.ds(1, ns, stride=2), :]
    blk = _iota((ns, QB), 0)
    qpos = _iota((ns, QB), 1) + q0
    forced = (blk == qpos // SEL_BLOCK) | (blk == 0)
    future = blk * SEL_BLOCK > qpos
    score = jnp.where(forced, FORCE, jnp.where(future, -FORCE, imp_s))
    score_ref[...] = jnp.where(blk < n_sel_blocks, score, -3e38)
    sel_t = jnp.where(_select_blocks(score_ref, ns, n_keep), 1.0, 0.0).astype(BF16)

    row = _iota((QB, KT), 0)
    col = _iota((QB, KT), 1)
    causal0 = row + KPAD - col >= 0
    blk_e = _iota((ns, KT), 0)
    col_e = _iota((ns, KT), 1) // SEL_BLOCK

    def branch(k_ref, v_ref, hi, valid_fn):
        def tile(j, state, first):
            start = pl.multiple_of(q0 - j * KT, QB)
            k = k_ref[0, 0, pl.ds(start, KT), :]
            v = v_ref[0, 0, pl.ds(start, KT), :]
            valid = valid_fn(j, start, first)
            out = []
            for r in heads:
                s = _dot_nt(qs[r], k)
                s = s + (b0_ref[r] if first else c31_ref[g * NSA_REP + r])
                out.append(_softmax_step(s, valid, *state[r], v))
            return tuple(out)

        init = tuple((jnp.full((QB, 1), NEG, F32), jnp.zeros((QB, 1), F32),
                      jnp.zeros((QB, HEAD_DIM), F32)) for _ in heads)
        state = tile(0, init, True)
        state = lax.fori_loop(1, hi, lambda j, st: tile(j, st, False), state)
        return [acc / jnp.maximum(l, 1e-30) for (_, l, acc) in state]

    def valid_sel(j, start, first):
        base = start // SEL_BLOCK - KPAD // SEL_BLOCK
        expand = jnp.where(blk_e == col_e + base, 1.0, 0.0).astype(BF16)
        chosen = _dot_tn(sel_t, expand) > 0.5
        return (chosen & causal0) if first else chosen

    def valid_win(j, start, first):
        ok = col >= KPAD - start
        if first:
            return ok & causal0
        return ok & (row + KPAD - col + j * KT < WINDOW)

    o_s = branch(ks_ref, vs_ref, n_tiles, valid_sel)
    o_w = branch(kw_ref, vw_ref, jnp.minimum(n_tiles, (WINDOW + QB - 1) // KT + 1), valid_win)

    gate = jax.nn.sigmoid(gate_ref[0, 0].astype(F32))
    outs = []
    for r in heads:
        g0, g1, g2 = (gate[:, 3 * r + i:3 * r + i + 1] for i in range(3))
        outs.append(g0 * o_c[r] + g1 * o_s[r] + g2 * o_w[r])
    o_ref[0] = jnp.concatenate(outs, axis=-1).astype(o_ref.dtype)


def nsa_prompt(q, kc, vc, ks, vs, kw, vw, bias_c, bias0, c31, gates, n_sel_blocks):
    b, g, _, t, _ = q.shape
    tp = ks.shape[2]
    nc = kc.shape[2]
    cmp_spec = pl.BlockSpec((1, 1, nc, HEAD_DIM), lambda b_, g_, i: (b_, g_, 0, 0))
    kv_spec = pl.BlockSpec((1, 1, tp, HEAD_DIM), lambda b_, g_, i: (b_, g_, 0, 0))
    kern = functools.partial(_nsa_prompt_kernel, n_sel_blocks=n_sel_blocks,
                             n_keep=min(N_SEL, n_sel_blocks))
    return pl.pallas_call(
        kern, grid=(b, g, t // QB),
        in_specs=[pl.BlockSpec(memory_space=pltpu.SMEM),
                  pl.BlockSpec((1, 1, NSA_REP, QB, HEAD_DIM), lambda b_, g_, i: (b_, g_, 0, i, 0)),
                  cmp_spec, cmp_spec, kv_spec, kv_spec, kv_spec, kv_spec,
                  pl.BlockSpec((NSA_REP, 2, QB, CMP_BAND), lambda b_, g_, i: (g_, 0, 0, 0)),
                  pl.BlockSpec((NSA_REP, QB, KT), lambda b_, g_, i: (g_, 0, 0)),
                  pl.BlockSpec((1, 1, QB, LANES), lambda b_, g_, i: (b_, g_, i, 0))],
        out_specs=pl.BlockSpec((1, QB, NSA_REP * HEAD_DIM), lambda b_, g_, i: (b_, i, g_)),
        out_shape=jax.ShapeDtypeStruct((b, t, NSA_HEADS * HEAD_DIM), BF16),
        scratch_shapes=[pltpu.VMEM((nc, QB), F32), pltpu.VMEM((nc // 2, QB), F32)],
        compiler_params=_params(("arbitrary",) * 3), name="nsa_prompt")(
            c31, q, kc, vc, ks, vs, kw, vw, bias_c, bias0, gates)


CMP_PP = 16
CMP_HALF = CMP_BLOCK // 2


def _compress_kernel(pt_ref, w_ref, pe_ref, *rest, pp):
    pages, o_ref, lo_ref, hi_ref = rest[:pp], rest[pp], rest[pp + 1], rest[pp + 2]
    width = 2 * HEAD_DIM
    acc = jnp.zeros((pp * SUBLANES, 2 * width), F32)
    for i in range(CMP_HALF):
        rows = [pages[p][pl.ds(i, SUBLANES, stride=CMP_HALF), :] + pe_ref[i] for p in range(pp)]
        acc = acc + _dot(jnp.concatenate(rows, axis=0).astype(BF16), w_ref[0, i])
    lo_ref[...] = acc[:, :width]
    hi_ref[...] = acc[:, width:]
    n = pp * SUBLANES // 2
    o_ref[0] = lo_ref[pl.ds(0, n, stride=2), :] + hi_ref[pl.ds(1, n, stride=2), :]


def compress_pages(page_table, src, layer, col_block, w_cat, pe_tiles):
    nseq, n_pages = page_table.shape
    pp = min(CMP_PP, n_pages)
    width = 2 * HEAD_DIM

    def page_spec(p):
        return pl.BlockSpec((None, None, PAGE, width),
                            lambda b, s, kv, pt: (pt[b, s * pp + p], layer, 0, col_block + kv))

    grid_spec = pltpu.PrefetchScalarGridSpec(
        num_scalar_prefetch=1, grid=(nseq, n_pages // pp, 2),
        in_specs=[pl.BlockSpec((1, CMP_HALF, width, 2 * width), lambda b, s, kv, pt: (kv, 0, 0, 0)),
                  pl.BlockSpec((CMP_HALF, SUBLANES, width), lambda b, s, kv, pt: (0, 0, 0))]
        + [page_spec(p) for p in range(pp)],
        out_specs=pl.BlockSpec((1, pp * 4, width), lambda b, s, kv, pt: (b, s, kv)),
        scratch_shapes=[pltpu.VMEM((pp * SUBLANES, width), F32), pltpu.VMEM((pp * SUBLANES, width), F32)])
    return pl.pallas_call(
        functools.partial(_compress_kernel, pp=pp), grid_spec=grid_spec,
        out_shape=jax.ShapeDtypeStruct((nseq, n_pages * 4, 2 * width), F32),
        compiler_params=_params(("arbitrary",) * 3), name="compress")(
            page_table, w_cat, pe_tiles, *([src] * pp))


def _compress_weights(pe, wk, wv):
    def cat(w):
        w = w.reshape(CMP_BLOCK, HEAD_DIM, HEAD_DIM)
        z = jnp.zeros_like(w)
        full = jnp.concatenate([jnp.concatenate([w, z], -1), jnp.concatenate([z, w], -1)], axis=1)
        return jnp.concatenate([full[:CMP_HALF], full[CMP_HALF:]], axis=-1)

    w_cat = jnp.stack([cat(wk), cat(wv)]).astype(BF16)
    pe2 = jnp.tile(pe.astype(F32), (1, 2))
    pe_tiles = jnp.stack([pe2[:CMP_HALF], pe2[CMP_HALF:]], axis=1)
    pe_tiles = jnp.tile(pe_tiles, (1, SUBLANES // 2, 1))
    return w_cat, pe_tiles


SB_PP = 8
DIFF_PP = 8
SEL_PP = 16


def _page_specs(pp, n_pages, width, col_block, layer, reverse=True):
    def spec(p):
        def index(b, s, pt):
            logical = n_pages - 1 - (s * pp + p) if reverse else s * pp + p
            return (pt[b, logical], layer, 0, col_block)
        return pl.BlockSpec((None, None, PAGE, width), index)
    return [spec(p) for p in range(pp)]


def _token_of_row(shape):
    return _iota(shape, 0) % DEC_S


def _sb_sample_kernel(pt_ref, q_ref, new_ref, u_ref, *rest, pp):
    pages, o_ref, carry_ref, acc_ref = rest[:pp], rest[pp], rest[pp + 1], rest[pp + 2]
    s = pl.program_id(1)
    q = q_ref[0]
    u = u_ref[...]
    rows = SB_HEADS * DEC_S

    @pl.when(s == 0)
    def _():
        k = new_ref[0, :, :SB_W]
        v = new_ref[0, :, SB_W:]
        valid = _iota((rows, PAGE), 1) < _token_of_row((rows, PAGE))
        carry, acc = _sb_step(_dot_nt(q, k), valid, jnp.zeros((rows, 1), F32),
                              jnp.zeros((rows, SB_W), F32), v, u)
        carry_ref[...] = carry
        acc_ref[...] = acc

    for p in range(pp):
        @pl.when(jnp.max(carry_ref[...]) > SB_DEAD)
        def _(p=p):
            page = pages[p][...]
            k = page[:, :SB_W].astype(BF16)
            v = page[:, SB_W:].astype(BF16)
            carry, acc = _sb_step(_dot_nt(q, k), None, carry_ref[...], acc_ref[...], v, u)
            carry_ref[...] = carry
            acc_ref[...] = acc

    @pl.when(s == pl.num_programs(1) - 1)
    def _():
        a = acc_ref[...]
        o_ref[0] = jnp.concatenate(
            [a[h * DEC_S:(h + 1) * DEC_S, h * HEAD_DIM:(h + 1) * HEAD_DIM] for h in range(SB_HEADS)],
            axis=-1).astype(o_ref.dtype)


def sb_sample(page_table, qbd, new_kv, cache, layer):
    db, n_pages = page_table.shape
    pp = min(SB_PP, n_pages)
    rows = SB_HEADS * DEC_S
    grid_spec = pltpu.PrefetchScalarGridSpec(
        num_scalar_prefetch=1, grid=(db, n_pages // pp),
        in_specs=[pl.BlockSpec((1, rows, SB_W), lambda b, s, pt: (b, 0, 0)),
                  pl.BlockSpec((1, PAGE, 2 * SB_W), lambda b, s, pt: (b, 0, 0)),
                  pl.BlockSpec((PAGE, PAGE), lambda b, s, pt: (0, 0))]
        + _page_specs(pp, n_pages, 2 * SB_W, 0, layer),
        out_specs=pl.BlockSpec((1, DEC_S, SB_W), lambda b, s, pt: (b, 0, 0)),
        scratch_shapes=[pltpu.VMEM((rows, 1), F32), pltpu.VMEM((rows, SB_W), F32)])
    return pl.pallas_call(
        functools.partial(_sb_sample_kernel, pp=pp), grid_spec=grid_spec,
        out_shape=jax.ShapeDtypeStruct((db, DEC_S, SB_W), BF16),
        compiler_params=_params(("arbitrary",) * 2), name="sb_sample")(
            page_table, qbd, new_kv, _later_matrix(PAGE), *([cache] * pp))


def _diff_sample_kernel(pt_ref, q_ref, new_ref, cvec_ref, blast_ref, bnew_ref, lam_ref, sg_ref, *rest,
                        pp, lam_init):
    pages, o_ref, m_ref, l_ref, acc_ref = rest[:pp], rest[pp], rest[pp + 1], rest[pp + 2], rest[pp + 3]
    s = pl.program_id(1)
    q = q_ref[0]
    rows = DIFF_HEADS * 2 * DEC_S

    @pl.when(s == 0)
    def _():
        k = new_ref[0, :, :DIFF_KV_W]
        v = new_ref[0, :, DIFF_KV_W:]
        valid = _iota((rows, PAGE), 1) <= _token_of_row((rows, PAGE))
        m, l, acc = _softmax_step(_dot_nt(q, k) + bnew_ref[...], valid, jnp.full((rows, 1), NEG, F32),
                                  jnp.zeros((rows, 1), F32), jnp.zeros((rows, DIFF_KV_W), F32), v)
        m_ref[...] = m
        l_ref[...] = l
        acc_ref[...] = acc

    m, l, acc = m_ref[...], l_ref[...], acc_ref[...]
    for p in range(pp):
        page = pages[p][...]
        k = page[:, :DIFF_KV_W].astype(BF16)
        v = page[:, DIFF_KV_W:].astype(BF16)
        bias = cvec_ref[...]
        if p == 0:
            bias = jnp.where(s == 0, blast_ref[...], bias)
        m, l, acc = _softmax_step(_dot_nt(q, k) + bias, None, m, l, acc, v)
    m_ref[...] = m
    l_ref[...] = l
    acc_ref[...] = acc

    @pl.when(s == pl.num_programs(1) - 1)
    def _():
        a = acc_ref[...]
        ls = l_ref[...]
        lam = _diff_lambda(lam_ref, lam_init)
        width = 2 * HEAD_DIM
        outs = []
        for g in range(DIFF_KV_HEADS):
            for r in range(DIFF_REP):
                r0 = (g * DIFF_REP + r) * 2 * DEC_S
                o1 = a[r0:r0 + DEC_S, g * width:(g + 1) * width]
                o2 = a[r0 + DEC_S:r0 + 2 * DEC_S, g * width:(g + 1) * width]
                outs.append(_diff_finish(o1, ls[r0:r0 + DEC_S], o2, ls[r0 + DEC_S:r0 + 2 * DEC_S],
                                         lam, sg_ref[...], lam_init))
        o_ref[0] = jnp.concatenate(outs, axis=-1).astype(o_ref.dtype)


def diff_sample(page_table, qbd, new_kv, cvec, blast, bnew, lam_rows, sub_g, cache, layer, lam_init):
    db, n_pages = page_table.shape
    pp = min(DIFF_PP, n_pages)
    rows = DIFF_HEADS * 2 * DEC_S
    const = lambda shape: pl.BlockSpec(shape, lambda b, s, pt: (0,) * len(shape))
    grid_spec = pltpu.PrefetchScalarGridSpec(
        num_scalar_prefetch=1, grid=(db, n_pages // pp),
        in_specs=[pl.BlockSpec((1, rows, DIFF_KV_W), lambda b, s, pt: (b, 0, 0)),
                  pl.BlockSpec((1, PAGE, 2 * DIFF_KV_W), lambda b, s, pt: (b, 0, 0)),
                  const((rows, 1)), const((rows, PAGE)), const((rows, PAGE)),
                  const((4, HEAD_DIM)), const((1, 2 * HEAD_DIM))]
        + _page_specs(pp, n_pages, 2 * DIFF_KV_W, 0, layer),
        out_specs=pl.BlockSpec((1, DEC_S, DIFF_HEADS * 2 * HEAD_DIM), lambda b, s, pt: (b, 0, 0)),
        scratch_shapes=[pltpu.VMEM((rows, 1), F32), pltpu.VMEM((rows, 1), F32),
                        pltpu.VMEM((rows, DIFF_KV_W), F32)])
    return pl.pallas_call(
        functools.partial(_diff_sample_kernel, pp=pp, lam_init=lam_init), grid_spec=grid_spec,
        out_shape=jax.ShapeDtypeStruct((db, DEC_S, DIFF_HEADS * 2 * HEAD_DIM), BF16),
        compiler_params=_params(("arbitrary",) * 2), name="diff_sample")(
            page_table, qbd, new_kv, cvec, blast, bnew, lam_rows, sub_g, *([cache] * pp))


NSA_ROWS = NSA_HEADS * DEC_S
NSA_GT = NSA_GROUPS * DEC_S


def _nsa_diag(a):
    pieces = []
    for r in range(NSA_REP):
        for g in range(NSA_GROUPS):
            r0 = r * NSA_GT + g * DEC_S
            pieces.append(a[r0:r0 + DEC_S, g * HEAD_DIM:(g + 1) * HEAD_DIM])
    return jnp.concatenate(pieces, axis=0)


def _nsa_sample_cw_kernel(q_ref, kcvc_ref, bc_ref, win_ref, bw_ref, oc_ref, ow_ref, sel_ref,
                          impt_ref, score_ref, *, win_len, n_keep):
    q = q_ref[0]
    nc = kcvc_ref.shape[1]
    ns = nc // 2
    kc = kcvc_ref[0, :, :NSA_KV_W].astype(BF16)
    vc = kcvc_ref[0, :, NSA_KV_W:].astype(BF16)
    p = _softmax_full(_dot_nt(q, kc) + bc_ref[...], None)
    oc_ref[0] = _nsa_diag(_dot(p.astype(BF16), vc))
    imp = p[0:NSA_GT]
    for r in range(1, NSA_REP):
        imp = imp + p[r * NSA_GT:(r + 1) * NSA_GT]
    imp = jnp.concatenate([imp, jnp.zeros((LANES - NSA_GT, nc), F32)], axis=0)
    impt_ref[...] = imp.T
    imp_s = impt_ref[pl.ds(0, ns, stride=2), :] + impt_ref[pl.ds(1, ns, stride=2), :]
    blk = _iota((ns, LANES), 0)
    score_ref[...] = jnp.where(blk == 0, FORCE, imp_s)
    sel_t = jnp.where(_select_blocks(score_ref, ns, n_keep), 1.0, 0.0)
    sel_ref[0] = sel_t.T[:NSA_GT].astype(sel_ref.dtype)

    wl = win_ref.shape[1]
    kw = win_ref[0, :, :NSA_KV_W]
    vw = win_ref[0, :, NSA_KV_W:]
    tok = _token_of_row((NSA_ROWS, wl))
    col = _iota((NSA_ROWS, wl), 1)
    valid = (col > tok + (win_len - WINDOW)) & (col <= tok + win_len) & (col < win_len + DEC_S)
    pw = _softmax_full(_dot_nt(q, kw) + bw_ref[...], valid)
    ow_ref[0] = _nsa_diag(_dot(pw.astype(BF16), vw))


def nsa_sample_cw(qbd, kcvc, bias_c, win_all, bias_w, win_len, n_keep):
    db = qbd.shape[0]
    nc = kcvc.shape[1]
    wl = win_all.shape[1]
    const = lambda shape: pl.BlockSpec(shape, lambda b: (0,) * len(shape))
    return pl.pallas_call(
        functools.partial(_nsa_sample_cw_kernel, win_len=win_len, n_keep=n_keep), grid=(db,),
        in_specs=[pl.BlockSpec((1, NSA_ROWS, NSA_KV_W), lambda b: (b, 0, 0)),
                  pl.BlockSpec((1, nc, 2 * NSA_KV_W), lambda b: (b, 0, 0)),
                  const((NSA_ROWS, nc)),
                  pl.BlockSpec((1, wl, 2 * NSA_KV_W), lambda b: (b, 0, 0)),
                  const((NSA_ROWS, wl))],
        out_specs=[pl.BlockSpec((1, NSA_ROWS, HEAD_DIM), lambda b: (b, 0, 0)),
                   pl.BlockSpec((1, NSA_ROWS, HEAD_DIM), lambda b: (b, 0, 0)),
                   pl.BlockSpec((1, NSA_GT, nc // 2), lambda b: (b, 0, 0))],
        out_shape=[jax.ShapeDtypeStruct((db, NSA_ROWS, HEAD_DIM), F32),
                   jax.ShapeDtypeStruct((db, NSA_ROWS, HEAD_DIM), F32),
                   jax.ShapeDtypeStruct((db, NSA_GT, nc // 2), BF16)],
        scratch_shapes=[pltpu.VMEM((nc, LANES), F32), pltpu.VMEM((nc // 2, LANES), F32)],
        compiler_params=_params(("arbitrary",)), name="nsa_sample_cw")(qbd, kcvc, bias_c, win_all, bias_w)


def _nsa_sample_sel_kernel(pt_ref, q_ref, new_ref, sel_ref, cvec_ref, blast_ref, bnew_ref, oc_ref, ow_ref,
                           gate_ref, *rest, pp, n_pages):
    pages, o_ref, m_ref, l_ref, acc_ref = rest[:pp], rest[pp], rest[pp + 1], rest[pp + 2], rest[pp + 3]
    s = pl.program_id(1)
    q = q_ref[0]
    sel = sel_ref[0]
    ns = sel.shape[1]

    @pl.when(s == 0)
    def _():
        k = new_ref[0, :, :NSA_KV_W]
        v = new_ref[0, :, NSA_KV_W:]
        valid = _iota((NSA_ROWS, PAGE), 1) <= _token_of_row((NSA_ROWS, PAGE))
        m, l, acc = _softmax_step(_dot_nt(q, k) + bnew_ref[...], valid, jnp.full((NSA_ROWS, 1), NEG, F32),
                                  jnp.zeros((NSA_ROWS, 1), F32), jnp.zeros((NSA_ROWS, NSA_KV_W), F32), v)
        m_ref[...] = m
        l_ref[...] = l
        acc_ref[...] = acc

    m, l, acc = m_ref[...], l_ref[...], acc_ref[...]
    blk_e = _iota((ns, PAGE), 0)
    col_e = _iota((ns, PAGE), 1) // SEL_BLOCK
    for p in range(pp):
        logical = n_pages - 1 - (s * pp + p)
        page = pages[p][...]
        k = page[:, :NSA_KV_W].astype(BF16)
        v = page[:, NSA_KV_W:].astype(BF16)
        expand = jnp.where(blk_e == col_e + logical * (PAGE // SEL_BLOCK), 1.0, 0.0).astype(BF16)
        chosen = _dot(sel, expand) > 0.5
        valid = jnp.concatenate([chosen] * NSA_REP, axis=0)
        bias = cvec_ref[...]
        if p == 0:
            bias = jnp.where(s == 0, blast_ref[...], bias)
        m, l, acc = _softmax_step(_dot_nt(q, k) + bias, valid, m, l, acc, v)
    m_ref[...] = m
    l_ref[...] = l
    acc_ref[...] = acc

    @pl.when(s == pl.num_programs(1) - 1)
    def _():
        o_s = _nsa_diag(acc_ref[...] / jnp.maximum(l_ref[...], 1e-30))
        gate = jax.nn.sigmoid(gate_ref[0].astype(F32))
        o = gate[:, 0:1] * oc_ref[0] + gate[:, 1:2] * o_s + gate[:, 2:3] * ow_ref[0]
        pieces = []
        for g in range(NSA_GROUPS):
            for r in range(NSA_REP):
                r0 = r * NSA_GT + g * DEC_S
                pieces.append(o[r0:r0 + DEC_S])
        o_ref[0] = jnp.concatenate(pieces, axis=-1).astype(o_ref.dtype)


def nsa_sample_sel(page_table, qbd, new_kv, sel, cvec, blast, bnew, o_c, o_w, gates, cache, layer):
    db, n_pages = page_table.shape
    pp = min(SEL_PP, n_pages)
    ns = sel.shape[2]
    const = lambda shape: pl.BlockSpec(shape, lambda b, s, pt: (0,) * len(shape))
    per_seq = lambda shape: pl.BlockSpec((1,) + shape, lambda b, s, pt: (b,) + (0,) * len(shape))
    grid_spec = pltpu.PrefetchScalarGridSpec(
        num_scalar_prefetch=1, grid=(db, n_pages // pp),
        in_specs=[per_seq((NSA_ROWS, NSA_KV_W)), per_seq((PAGE, 2 * NSA_KV_W)), per_seq((NSA_GT, ns)),
                  const((NSA_ROWS, 1)), const((NSA_ROWS, PAGE)), const((NSA_ROWS, PAGE)),
                  per_seq((NSA_ROWS, HEAD_DIM)), per_seq((NSA_ROWS, HEAD_DIM)), per_seq((NSA_ROWS, LANES))]
        + _page_specs(pp, n_pages, 2 * NSA_KV_W, 1, layer),
        out_specs=pl.BlockSpec((1, DEC_S, NSA_HEADS * HEAD_DIM), lambda b, s, pt: (b, 0, 0)),
        scratch_shapes=[pltpu.VMEM((NSA_ROWS, 1), F32), pltpu.VMEM((NSA_ROWS, 1), F32),
                        pltpu.VMEM((NSA_ROWS, NSA_KV_W), F32)])
    return pl.pallas_call(
        functools.partial(_nsa_sample_sel_kernel, pp=pp, n_pages=n_pages), grid_spec=grid_spec,
        out_shape=jax.ShapeDtypeStruct((db, DEC_S, NSA_HEADS * HEAD_DIM), BF16),
        compiler_params=_params(("arbitrary",) * 2), name="nsa_sample_sel")(
            page_table, qbd, new_kv, sel, cvec, blast, bnew, o_c, o_w, gates, *([cache] * pp))


def _layer_norm(y, g, b):
    mu = jnp.mean(y, axis=-1, keepdims=True)
    d = y - mu
    var = jnp.mean(d * d, axis=-1, keepdims=True)
    return d * lax.rsqrt(var + LN_EPS) * g + b


def _split_bf16(x):
    hi = x.astype(BF16)
    return hi, (x - hi.astype(F32)).astype(BF16)


def _mix_router_kernel(a1_ref, a2_ref, w1_ref, w2_ref, x_ref, g_ref, b_ref, wr_ref, br_ref, u_ref,
                       xn_ref, xb_ref, rt_ref, gt_ref, cnt_ref, *, alpha):
    mix = _dot(a1_ref[...], w1_ref[...]) + _dot(a2_ref[...], w2_ref[...])
    xn = _layer_norm(alpha * x_ref[...] + mix, g_ref[...], b_ref[...])
    xn_ref[...] = xn
    xb_ref[...] = xn.astype(BF16)
    xh, xl = _split_bf16(xn)
    wh, wl = _split_bf16(wr_ref[...])
    logits = _dot_nt(wh, xh) + _dot_nt(wh, xl) + _dot_nt(wl, xh) + br_ref[...]
    e_iota = _iota(logits.shape, 0)
    work = logits
    vals, sels = [], []
    for _ in range(TOP_K):
        top = jnp.max(work, axis=0, keepdims=True)
        idx = jnp.min(jnp.where(work == top, e_iota, N_EXPERTS), axis=0, keepdims=True)
        sel = e_iota == idx
        vals.append(top)
        sels.append(sel)
        work = jnp.where(sel, -jnp.inf, work)
    ex = [jnp.exp(v - vals[0]) for v in vals]
    den = ex[0] + ex[1] + ex[2] + ex[3]
    gate = jnp.zeros(logits.shape, F32)
    chosen = sels[0]
    for k in range(TOP_K):
        gate = jnp.where(sels[k], ex[k] / den, gate)
        chosen = chosen | sels[k]
    cf = jnp.where(chosen, 1.0, 0.0)
    rank = _dot(cf.astype(BF16), u_ref[...])
    rt_ref[0] = jnp.where(chosen, rank, -1.0)
    gt_ref[0] = gate
    cnt = jnp.sum(cf, axis=1, keepdims=True)
    cnt_ref[0] = jnp.broadcast_to(cnt, (N_EXPERTS, LANES)).astype(jnp.int32)


def mix_router(a1, a2, w1, w2, x, ln_g, ln_b, w_router_t, b_router, tt, alpha):
    n = x.shape[0]
    half = a1.shape[1]
    j = np.arange(tt)
    before = jnp.asarray((j[:, None] < j[None, :]).astype(np.float32), dtype=BF16)
    tile = lambda w: pl.BlockSpec((tt, w), lambda i: (i, 0))
    const = lambda shape: pl.BlockSpec(shape, lambda i: (0,) * len(shape))
    route = pl.BlockSpec((1, N_EXPERTS, tt), lambda i: (i, 0, 0))
    return pl.pallas_call(
        functools.partial(_mix_router_kernel, alpha=alpha), grid=(n // tt,),
        in_specs=[tile(half), tile(half), const((half, D_MODEL)), const((half, D_MODEL)), tile(D_MODEL),
                  const((1, D_MODEL)), const((1, D_MODEL)), const((N_EXPERTS, D_MODEL)),
                  const((N_EXPERTS, 1)), const((tt, tt))],
        out_specs=[tile(D_MODEL), tile(D_MODEL), route, route,
                   pl.BlockSpec((1, N_EXPERTS, LANES), lambda i: (i, 0, 0))],
        out_shape=[jax.ShapeDtypeStruct((n, D_MODEL), F32), jax.ShapeDtypeStruct((n, D_MODEL), BF16),
                   jax.ShapeDtypeStruct((n // tt, N_EXPERTS, tt), F32),
                   jax.ShapeDtypeStruct((n // tt, N_EXPERTS, tt), F32),
                   jax.ShapeDtypeStruct((n // tt, N_EXPERTS, LANES), jnp.int32)],
        compiler_params=_params(("arbitrary",)), name="mix_router")(
            a1, a2, w1, w2, x, ln_g, ln_b, w_router_t, b_router, before)


def _pair_permutation():
    c = np.arange(2 * LANES)
    dest = np.where(c % 2 == 0, c // 2, LANES + c // 2)
    return jnp.asarray((dest[:, None] == c[None, :]).astype(np.float32), dtype=BF16)


def _split_pairs_kernel(w_ref, s_ref, o_ref):
    s = s_ref[...]
    for blk in range(w_ref.shape[2] // (2 * LANES)):
        cols = slice(blk * 2 * LANES, (blk + 1) * 2 * LANES)
        o_ref[0, :, cols] = _dot(w_ref[0, :, cols].astype(BF16), s).astype(BF16)


def split_pairs(w_up):
    e, d, f2 = w_up.shape
    return pl.pallas_call(
        _split_pairs_kernel, grid=(e,),
        in_specs=[pl.BlockSpec((1, d, f2), lambda i: (i, 0, 0)),
                  pl.BlockSpec((2 * LANES, 2 * LANES), lambda i: (0, 0))],
        out_specs=pl.BlockSpec((1, d, f2), lambda i: (i, 0, 0)),
        out_shape=jax.ShapeDtypeStruct((e, d, f2), BF16),
        compiler_params=_params(("arbitrary",)), name="split_pairs")(w_up, _pair_permutation())


def _moe_kernel(cnt_ref, xb_ref, xn_ref, rt_ref, gt_ref, wu_ref, bu_ref, wd_ref, bd_ref,
                lg_ref, lb_ref, y_ref, yb_ref, acc_ref, *, alpha):
    i = pl.program_id(0)
    e = pl.program_id(1)
    tt = xb_ref.shape[0]

    @pl.when(e == 0)
    def _():
        acc_ref[...] = jnp.zeros_like(acc_ref)

    n_chunks = (cnt_ref[i, e] + MOE_CH - 1) // MOE_CH
    slot = rt_ref[0, pl.ds(e, 1), :]
    gate = gt_ref[0, pl.ds(e, 1), :]

    def chunk(c, _):
        want = (_iota((MOE_CH, tt), 0) + c * MOE_CH).astype(F32)
        hit = slot == want
        xe = _dot(jnp.where(hit, 1.0, 0.0).astype(BF16), xb_ref[...]).astype(BF16)
        h = _dot(xe, wu_ref[0]) + bu_ref[0]
        acts = []
        for blk in range(h.shape[1] // (2 * LANES)):
            h_glu = jnp.minimum(h[:, blk * 2 * LANES:blk * 2 * LANES + LANES], SWIGLU_LIMIT)
            h_lin = jnp.clip(h[:, blk * 2 * LANES + LANES:(blk + 1) * 2 * LANES], -SWIGLU_LIMIT, SWIGLU_LIMIT)
            acts.append((h_glu * jax.nn.sigmoid(SWIGLU_ALPHA * h_glu) * (h_lin + 1.0)).astype(BF16))
        y = _dot(jnp.concatenate(acts, axis=-1), wd_ref[0]) + bd_ref[0]
        back = jnp.where(hit, gate, 0.0).astype(BF16)
        acc_ref[...] += _dot_tn(back, y.astype(BF16))
        return 0

    lax.fori_loop(0, n_chunks, chunk, 0)

    @pl.when(e == pl.num_programs(1) - 1)
    def _():
        y = _layer_norm(alpha * xn_ref[...] + acc_ref[...], lg_ref[...], lb_ref[...])
        y_ref[...] = y
        yb_ref[...] = y.astype(BF16)


def moe_ln(cnt, xb, xn, rt, gt, w_up, b_up, w_down, b_down, ln_g, ln_b, tt, alpha):
    n = xn.shape[0]
    ff = w_down.shape[1]
    tile = pl.BlockSpec((tt, D_MODEL), lambda i, e, c: (i, 0))
    route = pl.BlockSpec((1, N_EXPERTS, tt), lambda i, e, c: (i, 0, 0))
    per_e = lambda shape: pl.BlockSpec((1,) + shape, lambda i, e, c: (e, 0, 0))
    const = pl.BlockSpec((1, D_MODEL), lambda i, e, c: (0, 0))
    grid_spec = pltpu.PrefetchScalarGridSpec(
        num_scalar_prefetch=1, grid=(n // tt, N_EXPERTS),
        in_specs=[tile, tile, route, route, per_e((D_MODEL, 2 * ff)), per_e((1, 2 * ff)),
                  per_e((ff, D_MODEL)), per_e((1, D_MODEL)), const, const],
        out_specs=[tile, tile],
        scratch_shapes=[pltpu.VMEM((tt, D_MODEL), F32)])
    return pl.pallas_call(
        functools.partial(_moe_kernel, alpha=alpha), grid_spec=grid_spec,
        out_shape=[jax.ShapeDtypeStruct((n, D_MODEL), F32), jax.ShapeDtypeStruct((n, D_MODEL), BF16)],
        compiler_params=_params(("arbitrary",) * 2), name="moe_ln")(
            cnt, xb, xn, rt, gt, w_up, b_up, w_down, b_down, ln_g, ln_b)


MOE_TILE = 1024


def _split_pairs_bias(b_up):
    e, f2 = b_up.shape
    b = b_up.astype(F32).reshape(e, f2 // (2 * LANES), LANES, 2).transpose(0, 1, 3, 2)
    return b.reshape(e, 1, f2)


def _front_pad(a, axis):
    pad = [(0, 0)] * a.ndim
    pad[axis] = (KPAD, 0)
    return jnp.pad(a, pad)


def _pad_rows(a, rows):
    return jnp.pad(a, ((0, 0), (0, rows - a.shape[1]), (0, 0)))


def _channel_mixer(layer, a1, a2, w_out, x, p, tt):
    alpha = p["alpha"]
    half = a1.shape[1]
    w_out = w_out.astype(BF16)
    xn, xb, rt, gt, cnt = mix_router(
        a1, a2, w_out[:half], w_out[half:], x, p["ln_mix_g"][layer][None], p["ln_mix_b"][layer][None],
        p["w_router"][layer].T.astype(F32), p["b_router"][layer][:, None].astype(F32), tt, alpha)
    return moe_ln(cnt[:, :, 0], xb, xn, rt, gt, *p["experts"][layer],
                  p["ln_ffn_g"][layer][None], p["ln_ffn_b"][layer][None], tt, alpha)


def _even_prompt(h, b, t, cmp_w, bias_c, bias0, c31):
    h3 = h.reshape(b, t, -1)
    heads = lambda a, nh: a.reshape(b, t, nh, HEAD_DIM).transpose(0, 2, 1, 3).astype(BF16)
    o = 3 * SB_W
    o_sb = sb_prompt(heads(h3[..., :SB_W], SB_HEADS),
                     _front_pad(heads(h3[..., SB_W:2 * SB_W], SB_HEADS), 2),
                     _front_pad(heads(h3[..., 2 * SB_W:o], SB_HEADS), 2))
    o_sb = o_sb.transpose(0, 2, 1, 3).reshape(b * t, SB_W)
    nq = h3[..., o:o + NSA_Q_W].reshape(b, t, NSA_GROUPS, NSA_REP, HEAD_DIM).transpose(0, 2, 3, 1, 4)
    o += NSA_Q_W
    nkv = h3[..., o:o + 6 * NSA_KV_W].reshape(b, t, 6, NSA_GROUPS, HEAD_DIM)
    gates = h3[..., o + 6 * NSA_KV_W:o + 6 * NSA_KV_W + NSA_GATE_W]
    gates = gates.reshape(b, t, NSA_GROUPS, NSA_REP * 3).transpose(0, 2, 1, 3)
    gates = jnp.pad(gates, ((0, 0), (0, 0), (0, 0), (0, LANES - NSA_REP * 3)))
    n_pages = t // PAGE
    table = jnp.arange(b * n_pages, dtype=jnp.int32).reshape(b, n_pages)
    kcvc = compress_pages(table, h.reshape(b * n_pages, 1, PAGE, h.shape[-1]), 0, o // (2 * HEAD_DIM), *cmp_w)
    nc = t // CMP_BLOCK
    ncp = -(-nc // LANES) * LANES
    kcvc = kcvc.reshape(b, nc, 2, NSA_GROUPS, HEAD_DIM).transpose(2, 0, 3, 1, 4).astype(BF16)
    kcvc = jnp.pad(kcvc, ((0, 0), (0, 0), (0, 0), (0, ncp - nc), (0, 0)))
    grp = lambda i: _front_pad(nkv[:, :, i].transpose(0, 2, 1, 3).astype(BF16), 2)
    o_nsa = nsa_prompt(nq.astype(BF16), kcvc[0], kcvc[1], grp(2), grp(3), grp(4), grp(5),
                       bias_c, bias0, c31, gates, -(-t // SEL_BLOCK))
    sb_rows = h3[..., SB_W:3 * SB_W].reshape(b, t, 2, SB_HEADS, HEAD_DIM)
    keep = min(WINDOW, t)
    return (o_sb, o_nsa.reshape(b * t, NSA_Q_W), sb_rows, nkv[:, :, :4], nkv[:, t - keep:, 4:])


def _even_sample(h, db, page_table, cache_sb, cache_nsa, win_state, layer, cmp_w, tabs):
    h3 = h.reshape(db, DEC_S, -1)
    past = page_table.shape[1] * PAGE
    o = 3 * SB_W
    q = h3[..., :SB_W].reshape(db, DEC_S, SB_HEADS, HEAD_DIM).transpose(0, 2, 1, 3) * ATT_SCALE
    eye = jnp.eye(SB_HEADS, dtype=F32)
    qbd = (q[:, :, :, None, :] * eye[None, :, None, :, None]).reshape(db, SB_HEADS * DEC_S, SB_W)
    o_sb = sb_sample(page_table, qbd.astype(BF16), _pad_rows(h3[..., SB_W:o], PAGE).astype(BF16),
                     cache_sb, layer)
    q = h3[..., o:o + NSA_Q_W].reshape(db, DEC_S, NSA_GROUPS, NSA_REP, HEAD_DIM).transpose(0, 3, 2, 1, 4)
    eye = jnp.eye(NSA_GROUPS, dtype=F32)
    qbd = (q[:, :, :, :, None, :] * ATT_SCALE * eye[None, None, :, None, :, None])
    qbd = qbd.reshape(db, NSA_ROWS, NSA_KV_W).astype(BF16)
    o += NSA_Q_W
    nkv = h3[..., o:o + 6 * NSA_KV_W]
    gates = h3[..., o + 6 * NSA_KV_W:o + 6 * NSA_KV_W + NSA_GATE_W]
    gates = gates.reshape(db, DEC_S, NSA_GROUPS, NSA_REP, 3).transpose(0, 3, 2, 1, 4).reshape(db, NSA_ROWS, 3)
    gates = jnp.pad(gates, ((0, 0), (0, 0), (0, LANES - 3)))
    kcvc = compress_pages(page_table, cache_nsa, layer, 0, *cmp_w)
    win_len = win_state.shape[2]
    new_win = nkv[..., 4 * NSA_KV_W:]
    win_all = jnp.concatenate([win_state[:, layer].reshape(db, win_len, 2 * NSA_KV_W), new_win], axis=1)
    wl = tabs["bias_w"].shape[1]
    n_blocks = -(-(past + DEC_S) // SEL_BLOCK)
    o_c, o_w, sel = nsa_sample_cw(qbd, kcvc, tabs["bias_c"], _pad_rows(win_all, wl).astype(BF16),
                                  tabs["bias_w"], win_len, min(N_SEL, n_blocks) - 1)
    o_nsa = nsa_sample_sel(page_table, qbd, _pad_rows(nkv[..., 2 * NSA_KV_W:4 * NSA_KV_W], PAGE).astype(BF16),
                           sel, tabs["cvec"], tabs["blast"], tabs["bnew"], o_c, o_w, gates, cache_nsa, layer)
    sb_rows = h3[..., SB_W:3 * SB_W].reshape(db, DEC_S, 2, SB_HEADS, HEAD_DIM)
    nsa_rows = nkv[..., :4 * NSA_KV_W].reshape(db, DEC_S, 4, NSA_GROUPS, HEAD_DIM)
    win_rows = win_all[:, max(0, win_len + DEC_S - WINDOW):].reshape(db, -1, 2, NSA_GROUPS, HEAD_DIM)
    return (o_sb.reshape(db * DEC_S, SB_W), o_nsa.reshape(db * DEC_S, NSA_Q_W), sb_rows, nsa_rows, win_rows)


def _odd_prompt(h, b, t, bias0, c31, lam_rows, sub_g, lam_init):
    h3 = h.reshape(b, t, -1)
    q = h3[..., :DIFF_Q_W].reshape(b, t, DIFF_KV_HEADS, DIFF_REP, 2, HEAD_DIM).transpose(0, 2, 3, 4, 1, 5)
    kv = h3[..., DIFF_Q_W:].reshape(b, t, 2, DIFF_KV_HEADS, 2 * HEAD_DIM)
    k = kv[:, :, 0].reshape(b, t, DIFF_KV_HEADS, 2, HEAD_DIM).transpose(0, 2, 3, 1, 4)
    v = kv[:, :, 1].transpose(0, 2, 1, 3)
    o = diff_prompt(q.astype(BF16), _front_pad(k.astype(BF16), 3), _front_pad(v.astype(BF16), 2),
                    bias0, c31, lam_rows, sub_g, lam_init)
    return o.reshape(b * t, DIFF_Q_W), kv


def _odd_sample(h, db, page_table, cache, layer, tabs, lam_rows, sub_g, lam_init):
    h3 = h.reshape(db, DEC_S, -1)
    q = h3[..., :DIFF_Q_W].reshape(db, DEC_S, DIFF_KV_HEADS, DIFF_REP, 2, HEAD_DIM).transpose(0, 2, 3, 4, 1, 5)
    eye_g = jnp.eye(DIFF_KV_HEADS, dtype=F32)
    eye_c = jnp.eye(2, dtype=F32)
    qbd = (q[..., None, None, :] * ATT_SCALE * eye_g[None, :, None, None, None, :, None, None]
           * eye_c[None, None, None, :, None, None, :, None])
    qbd = qbd.reshape(db, DIFF_HEADS * 2 * DEC_S, DIFF_KV_W).astype(BF16)
    o = diff_sample(page_table, qbd, _pad_rows(h3[..., DIFF_Q_W:], PAGE).astype(BF16), tabs["cvec"],
                    tabs["blast"], tabs["bnew"], lam_rows, sub_g, cache, layer, lam_init)
    kv = h3[..., DIFF_Q_W:].reshape(db, DEC_S, 2, DIFF_KV_HEADS, 2 * HEAD_DIM)
    return o.reshape(db * DEC_S, DIFF_Q_W), kv


def _sample_tables(rel_bias, past, head_of_row, tok_of_row, with_cmp):
    tok = np.asarray(tok_of_row)
    key = np.arange(PAGE)
    tabs = {
        "cvec": rel_bias.astype(F32)[N_BUCKETS - 1][np.asarray(head_of_row)][:, None],
        "blast": _bias_table(rel_bias, tok[:, None] + PAGE - key[None, :], head_of_row),
        "bnew": _bias_table(rel_bias, tok[:, None] - key[None, :], head_of_row),
    }
    if with_cmp:
        nc = past // CMP_BLOCK
        c_end = np.arange(nc) * CMP_BLOCK + CMP_BLOCK - 1
        tabs["bias_c"] = _bias_table(rel_bias, past + tok[:, None] - c_end[None, :], head_of_row)
        win_len = min(WINDOW, past)
        wl = -(-(win_len + DEC_S) // LANES) * LANES
        tabs["bias_w"] = _bias_table(rel_bias, tok[:, None] + win_len - np.arange(wl)[None, :], head_of_row)
    return tabs


def kernel(x_prompt, x_sample, cache_sb_kv, cache_nsa_kv, cache_diff_kv, state_nsa_win, page_table, rel_bias,
           even_w_in, even_cmp_pe, even_cmp_wk, even_cmp_wv, even_w_out, odd_w_in, odd_lambda, odd_subln_g,
           odd_w_out, ln_mix_g, ln_mix_b, ln_ffn_g, ln_ffn_b, moe_w_router, moe_b_router, moe_w_up, moe_b_up,
           moe_w_down, moe_b_down):
    b, t, d = x_prompt.shape
    db, s, _ = x_sample.shape
    depth = ln_mix_g.shape[0]
    n_pool = cache_sb_kv.shape[0]
    past = page_table.shape[1] * PAGE
    assert s == DEC_S and d == D_MODEL and t % QB == 0 and (b * t) % MOE_TILE == 0
    assert past % (SEL_BLOCK * LANES) == 0 and state_nsa_win.shape[2] == WINDOW

    params = {
        "alpha": (2.0 * depth) ** 0.25,
        "ln_mix_g": ln_mix_g.astype(F32), "ln_mix_b": ln_mix_b.astype(F32),
        "ln_ffn_g": ln_ffn_g.astype(F32), "ln_ffn_b": ln_ffn_b.astype(F32),
        "w_router": moe_w_router, "b_router": moe_b_router,
        "experts": [(split_pairs(moe_w_up[l]), _split_pairs_bias(moe_b_up[l]),
                     moe_w_down[l].astype(BF16), moe_b_down[l][:, None, :].astype(F32)) for l in range(depth)],
    }
    c31 = rel_bias.astype(F32)[N_BUCKETS - 1]
    bias0 = _bias_table(rel_bias, np.arange(QB)[:, None] + KPAD - np.arange(KT)[None, :])
    bias_c = _cmp_band_table(rel_bias)
    nsa_rows = [(r, g, tk) for r in range(NSA_REP) for g in range(NSA_GROUPS) for tk in range(DEC_S)]
    nsa_tabs = _sample_tables(rel_bias, past, [g * NSA_REP + r for r, g, _ in nsa_rows],
                              [tk for _, _, tk in nsa_rows], True)
    diff_rows = [(g, r, tk) for g in range(DIFF_KV_HEADS) for r in range(DIFF_REP) for _ in range(2)
                 for tk in range(DEC_S)]
    diff_tabs = _sample_tables(rel_bias, past, [g * DIFF_REP + r for g, r, _ in diff_rows],
                               [tk for _, _, tk in diff_rows], False)

    cache_sb = cache_sb_kv.reshape(n_pool, -1, PAGE, 2 * SB_W)
    cache_nsa = cache_nsa_kv.reshape(n_pool, -1, PAGE, 4 * NSA_KV_W)
    cache_diff = cache_diff_kv.reshape(n_pool, -1, PAGE, 2 * DIFF_KV_W)

    xp = x_prompt.reshape(b * t, d).astype(F32)
    xs = x_sample.reshape(db * s, d).astype(F32)
    xpb, xsb = xp.astype(BF16), xs.astype(BF16)
    outs = {k: [] for k in ("sb_p", "sb_s", "nsa_p", "nsa_s", "win_p", "win_s", "diff_p", "diff_s")}
    for l in range(depth):
        j = l // 2
        if l % 2 == 0:
            w_in = jnp.pad(even_w_in[j], ((0, 0), (0, EVEN_IN_PAD - EVEN_IN))).astype(BF16)
            cmp_w = _compress_weights(even_cmp_pe[j], even_cmp_wk[j], even_cmp_wv[j])
            a1, a2, r_sb, r_nsa, r_win = _even_prompt(matmul(xpb, w_in, 512), b, t, cmp_w, bias_c, bias0, c31)
            s1, s2, s_sb, s_nsa, s_win = _even_sample(matmul(xsb, w_in, 512), db, page_table, cache_sb,
                                                      cache_nsa, state_nsa_win, j, cmp_w, nsa_tabs)
            w_out = even_w_out[j]
            for key, val in (("sb_p", r_sb), ("sb_s", s_sb), ("nsa_p", r_nsa), ("nsa_s", s_nsa),
                             ("win_p", r_win), ("win_s", s_win)):
                outs[key].append(val)
        else:
            lam_init = 0.8 - 0.6 * math.exp(-0.3 * l)
            w_in = odd_w_in[j].astype(BF16)
            lam_rows = odd_lambda[j].astype(F32)
            sub_g = odd_subln_g[j][None].astype(F32)
            ap, r_diff = _odd_prompt(matmul(xpb, w_in, 512), b, t, bias0, c31, lam_rows, sub_g, lam_init)
            as_, s_diff = _odd_sample(matmul(xsb, w_in, 512), db, page_table, cache_diff, j, diff_tabs,
                                      lam_rows, sub_g, lam_init)
            half = DIFF_Q_W // 2
            a1, a2, s1, s2 = ap[:, :half], ap[:, half:], as_[:, :half], as_[:, half:]
            w_out = odd_w_out[j]
            outs["diff_p"].append(r_diff)
            outs["diff_s"].append(s_diff)
        xp, xpb = _channel_mixer(l, a1, a2, w_out, xp, params, MOE_TILE)
        xs, xsb = _channel_mixer(l, s1, s2, w_out, xs, params, db * s)
    stack = lambda key: jnp.stack(outs[key], axis=1)
    return (xp.reshape(b, t, d), xs.reshape(db, s, d), stack("sb_p"), stack("sb_s"), stack("nsa_p"),
            stack("nsa_s"), stack("diff_p"), stack("diff_s"), stack("win_p"), stack("win_s"))
```

```python
import functools
import math

import numpy as np
import jax
import jax.numpy as jnp
from jax import lax
from jax.experimental import pallas as pl
from jax.experimental.pallas import tpu as pltpu

F32 = jnp.float32
BF16 = jnp.bfloat16

D_MODEL = 1024
HEAD_DIM = 64
SB_HEADS = 8
NSA_HEADS = 8
NSA_GROUPS = 2
NSA_REP = NSA_HEADS // NSA_GROUPS
CMP_BLOCK = 32
SEL_BLOCK = 64
N_SEL = 16
WINDOW = 512
DIFF_HEADS = 8
DIFF_KV_HEADS = 4
DIFF_REP = DIFF_HEADS // DIFF_KV_HEADS
N_BUCKETS = 32
MAX_DISTANCE = 128
N_EXPERTS = 32
TOP_K = 4
SWIGLU_ALPHA = 1.702
SWIGLU_LIMIT = 7.0
LN_EPS = 1e-5
PAGE = 128
NEG = -1e30
FORCE = 1e9
ATT_SCALE = HEAD_DIM ** -0.5

SB_W = SB_HEADS * HEAD_DIM
NSA_Q_W = NSA_HEADS * HEAD_DIM
NSA_KV_W = NSA_GROUPS * HEAD_DIM
NSA_GATE_W = NSA_HEADS * 3
EVEN_IN = 3 * SB_W + NSA_Q_W + 6 * NSA_KV_W + NSA_GATE_W
EVEN_IN_PAD = -(-EVEN_IN // 128) * 128
DIFF_Q_W = DIFF_HEADS * 2 * HEAD_DIM
DIFF_KV_W = DIFF_KV_HEADS * 2 * HEAD_DIM
ODD_IN = DIFF_Q_W + 2 * DIFF_KV_W

LANES = 128
SUBLANES = 8
VMEM_LIMIT = 52 * 1024 * 1024

QB = 128
KT = 512
KPAD = KT - QB
MOE_CH = 160
DEC_S = 8


def _dot(a, b):
    return jnp.dot(a, b, preferred_element_type=F32)


def _dot_nt(a, b):
    return lax.dot_general(a, b, (((1,), (1,)), ((), ())), preferred_element_type=F32)


def _dot_tn(a, b):
    return lax.dot_general(a, b, (((0,), (0,)), ((), ())), preferred_element_type=F32)


def _iota(shape, dim):
    return lax.broadcasted_iota(jnp.int32, shape, dim)


def _params(sem, vmem=VMEM_LIMIT):
    return pltpu.CompilerParams(dimension_semantics=sem, vmem_limit_bytes=vmem)


def _bucket_np(dist):
    n = np.maximum(dist, 0)
    exact = N_BUCKETS // 2
    nf = np.maximum(n, 1).astype(np.float32)
    large = exact + (np.log(nf / np.float32(exact)) / np.float32(math.log(MAX_DISTANCE / exact))
                     * np.float32(N_BUCKETS - exact)).astype(np.int32)
    return np.where(n < exact, n, np.minimum(large, N_BUCKETS - 1)).astype(np.int32)


def _bias_table(rel_bias, dist, head_of_row=None):
    bucket = _bucket_np(dist)
    rb = rel_bias.astype(F32)
    if head_of_row is None:
        return jnp.transpose(rb[bucket], (2, 0, 1))
    return rb[bucket, np.asarray(head_of_row)[:, None]]


def _mm_kernel(x_ref, w_ref, o_ref):
    o_ref[...] = _dot(x_ref[...], w_ref[...])


def matmul(x, w, tm):
    m, k = x.shape
    n = w.shape[1]
    tm = min(tm, m)
    return pl.pallas_call(
        _mm_kernel, grid=(m // tm,),
        in_specs=[pl.BlockSpec((tm, k), lambda i: (i, 0)), pl.BlockSpec((k, n), lambda i: (0, 0))],
        out_specs=pl.BlockSpec((tm, n), lambda i: (i, 0)),
        out_shape=jax.ShapeDtypeStruct((m, n), F32),
        compiler_params=_params(("arbitrary",)), name="in_proj")(x, w)


def _softmax_step(s, valid, m, l, acc, v):
    if valid is not None:
        s = jnp.where(valid, s, NEG)
    m_new = jnp.maximum(m, jnp.max(s, axis=-1, keepdims=True))
    p = jnp.exp(s - m_new)
    if valid is not None:
        p = jnp.where(valid, p, 0.0)
    alpha = jnp.exp(m - m_new)
    l = alpha * l + jnp.sum(p, axis=-1, keepdims=True)
    acc = alpha * acc + _dot(p.astype(BF16), v)
    return m_new, l, acc


def _softmax_full(s, valid):
    if valid is not None:
        s = jnp.where(valid, s, NEG)
    m = jnp.max(s, axis=-1, keepdims=True)
    e = jnp.exp(s - m)
    if valid is not None:
        e = jnp.where(valid, e, 0.0)
    return e / jnp.maximum(jnp.sum(e, axis=-1, keepdims=True), 1e-30)


def _sb_step(z, valid, carry, acc, v, u):
    sp = jnp.maximum(z, 0.0) + jnp.log(1.0 + jnp.exp(-jnp.abs(z)))
    lk = -sp if valid is None else jnp.where(valid, -sp, 0.0)
    hi = lk.astype(BF16)
    lo = (lk - hi.astype(F32)).astype(BF16)
    later = _dot(hi, u) + _dot(lo, u) + carry
    w = jnp.exp(z - sp + later)
    if valid is not None:
        w = jnp.where(valid, w, 0.0)
    acc = acc + _dot(w.astype(BF16), v)
    carry = carry + jnp.sum(lk, axis=-1, keepdims=True)
    return carry, acc


def _softmax_step_t(s, valid, m, l, acc, vt):
    if valid is not None:
        s = jnp.where(valid, s, NEG)
    m_new = jnp.maximum(m, jnp.max(s, axis=0, keepdims=True))
    p = jnp.exp(s - m_new)
    if valid is not None:
        p = jnp.where(valid, p, 0.0)
    alpha = jnp.exp(m - m_new)
    l = alpha * l + jnp.sum(p, axis=0, keepdims=True)
    acc = alpha * acc + _dot(vt, p.astype(BF16))
    return m_new, l, acc


def _softmax_full_t(s, valid):
    s = jnp.where(valid, s, NEG)
    m = jnp.max(s, axis=0, keepdims=True)
    e = jnp.where(valid, jnp.exp(s - m), 0.0)
    return e / jnp.maximum(jnp.sum(e, axis=0, keepdims=True), 1e-30)


def _sb_step_t(z, valid, carry, acc, vt, lm):
    nq = z.shape[1]
    sp = jnp.maximum(z, 0.0) + jnp.log(1.0 + jnp.exp(-jnp.abs(z)))
    lk = jnp.where(valid, -sp, 0.0)
    hi = lk.astype(BF16)
    lo = (lk - hi.astype(F32)).astype(BF16)
    both = _dot(lm, jnp.concatenate([hi, lo], axis=1))
    later = both[:, :nq] + both[:, nq:] + carry
    w = jnp.where(valid, jnp.exp(z - sp + later), 0.0)
    acc = acc + _dot(vt, w.astype(BF16))
    carry = carry + jnp.sum(lk, axis=0, keepdims=True)
    return carry, acc


SB_DEAD = -104.0


def _later_matrix(n):
    j = np.arange(n)
    return jnp.asarray((j[:, None] > j[None, :]).astype(np.float32), dtype=BF16)


def _sb_prompt_kernel(q_ref, k_ref, vt_ref, lm_ref, o_ref):
    qi = pl.program_id(2)
    q0 = qi * QB
    q = (q_ref[0, 0].astype(F32) * ATT_SCALE).astype(BF16)
    lm = lm_ref[...]
    key = _iota((KT, QB), 0)
    qry = _iota((KT, QB), 1)

    def tile(j, carry, acc, first):
        start = pl.multiple_of(q0 - j * KT, QB)
        k = k_ref[0, 0, pl.ds(start, KT), :]
        vt = vt_ref[0, 0, :, pl.ds(start, KT)]
        z = _dot_nt(k, q)
        valid = key >= KPAD - start
        if first:
            valid = valid & (qry + KPAD - key > 0)
        return _sb_step_t(z, valid, carry, acc, vt, lm)

    carry, acc = tile(0, jnp.zeros((1, QB), F32), jnp.zeros((HEAD_DIM, QB), F32), True)
    n_tiles = qi // (KT // QB) + 1

    def more(state):
        j, carry, _ = state
        return (j < n_tiles) & (jnp.max(carry) > SB_DEAD)

    def step(state):
        j, carry, acc = state
        carry, acc = tile(j, carry, acc, False)
        return j + 1, carry, acc

    _, _, acc = lax.while_loop(more, step, (jnp.int32(1), carry, acc))
    o_ref[0, 0] = acc.T.astype(o_ref.dtype)


def sb_prompt(q, k, vt):
    b, h, t, _ = q.shape
    tp = k.shape[2]
    return pl.pallas_call(
        _sb_prompt_kernel, grid=(b, h, t // QB),
        in_specs=[pl.BlockSpec((1, 1, QB, HEAD_DIM), lambda b_, h_, i: (b_, h_, i, 0)),
                  pl.BlockSpec((1, 1, tp, HEAD_DIM), lambda b_, h_, i: (b_, h_, 0, 0)),
                  pl.BlockSpec((1, 1, HEAD_DIM, tp), lambda b_, h_, i: (b_, h_, 0, 0)),
                  pl.BlockSpec((KT, KT), lambda b_, h_, i: (0, 0))],
        out_specs=pl.BlockSpec((1, 1, QB, HEAD_DIM), lambda b_, h_, i: (b_, h_, i, 0)),
        out_shape=jax.ShapeDtypeStruct((b, h, t, HEAD_DIM), BF16),
        compiler_params=_params(("arbitrary",) * 3), name="sb_prompt")(q, k, vt, _later_matrix(KT).T)


def _diff_lambda(lam_ref, lam_init):
    lv = lam_ref[...].astype(F32)
    a = jnp.sum(lv[0:1] * lv[1:2], axis=-1, keepdims=True)
    b = jnp.sum(lv[2:3] * lv[3:4], axis=-1, keepdims=True)
    return jnp.exp(a) - jnp.exp(b) + lam_init


def _diff_finish(o1, l1, o2, l2, lam, sub_g, lam_init):
    a = o1 / jnp.maximum(l1, 1e-30) - lam * (o2 / jnp.maximum(l2, 1e-30))
    a = a * lax.rsqrt(jnp.mean(jnp.square(a), axis=-1, keepdims=True) + LN_EPS)
    return a * sub_g * (1.0 - lam_init)


def _diff_prompt_kernel(q_ref, k_ref, vt_ref, b0_ref, c31_ref, lam_ref, sg_ref, o_ref, *, lam_init):
    qi = pl.program_id(2)
    q0 = qi * QB
    w = DIFF_REP * QB
    key = _iota((KT, w), 0)
    qry = _iota((KT, w), 1) % QB
    qcat = [jnp.concatenate([(q_ref[0, 0, r, c].astype(F32) * ATT_SCALE).astype(BF16)
                             for r in range(DIFF_REP)], axis=0) for c in range(2)]

    def tile(j, state, first, padded):
        start = pl.multiple_of(q0 - j * KT, QB)
        vt = vt_ref[0, 0, :, pl.ds(start, KT)]
        valid = (key >= KPAD - start) if padded else None
        if first:
            valid = valid & (qry + KPAD - key >= 0)
        out = []
        for c in range(2):
            k = k_ref[0, 0, c, pl.ds(start, KT), :]
            s = _dot_nt(k, qcat[c]) + (b0_ref[0] if first else c31_ref[0])
            out.append(_softmax_step_t(s, valid, *state[c], vt))
        return tuple(out)

    init = tuple((jnp.full((1, w), NEG, F32), jnp.zeros((1, w), F32),
                  jnp.zeros((2 * HEAD_DIM, w), F32)) for _ in range(2))
    state = tile(0, init, True, True)
    last = qi // (KT // QB)
    state = lax.fori_loop(1, last, lambda j, st: tile(j, st, False, False), state)
    state = lax.cond(last >= 1, lambda st: tile(last, st, False, True), lambda st: st, state)
    lam = _diff_lambda(lam_ref, lam_init)
    (_, l1, o1), (_, l2, o2) = state
    a = o1 / jnp.maximum(l1, 1e-30) - lam * (o2 / jnp.maximum(l2, 1e-30))
    a = a * lax.rsqrt(jnp.mean(jnp.square(a), axis=0, keepdims=True) + LN_EPS)
    a = a * sg_ref[...] * (1.0 - lam_init)
    o_ref[0] = jnp.concatenate([a[:, r * QB:(r + 1) * QB].T for r in range(DIFF_REP)],
                               axis=-1).astype(o_ref.dtype)


def _heads_on_lanes(table, group):
    h, k, q = table.shape
    return table.reshape(h // group, group, k, q).transpose(0, 2, 1, 3).reshape(h // group, k, group * q)


def diff_prompt(q, k, vt, bias0_t, c31, lam_rows, sub_g_col, lam_init):
    b, gk, _, _, t, _ = q.shape
    tp = k.shape[3]
    w = DIFF_REP * QB
    b0 = _heads_on_lanes(bias0_t, DIFF_REP)
    c31 = _heads_on_lanes(jnp.broadcast_to(c31[:, None, None], (DIFF_HEADS, 1, QB)), DIFF_REP)
    return pl.pallas_call(
        functools.partial(_diff_prompt_kernel, lam_init=lam_init), grid=(b, gk, t // QB),
        in_specs=[pl.BlockSpec((1, 1, DIFF_REP, 2, QB, HEAD_DIM), lambda b_, g_, i: (b_, g_, 0, 0, i, 0)),
                  pl.BlockSpec((1, 1, 2, tp, HEAD_DIM), lambda b_, g_, i: (b_, g_, 0, 0, 0)),
                  pl.BlockSpec((1, 1, 2 * HEAD_DIM, tp), lambda b_, g_, i: (b_, g_, 0, 0)),
                  pl.BlockSpec((1, KT, w), lambda b_, g_, i: (g_, 0, 0)),
                  pl.BlockSpec((1, 1, w), lambda b_, g_, i: (g_, 0, 0)),
                  pl.BlockSpec((4, HEAD_DIM), lambda b_, g_, i: (0, 0)),
                  pl.BlockSpec((2 * HEAD_DIM, 1), lambda b_, g_, i: (0, 0))],
        out_specs=pl.BlockSpec((1, QB, DIFF_REP * 2 * HEAD_DIM), lambda b_, g_, i: (b_, i, g_)),
        out_shape=jax.ShapeDtypeStruct((b, t, DIFF_HEADS * 2 * HEAD_DIM), BF16),
        compiler_params=_params(("arbitrary",) * 3), name="diff_prompt")(
            q, k, vt, b0, c31, lam_rows, sub_g_col)


CMP_BAND_BACK = 4
CMP_BAND = 16


def _cmp_band_table(rel_bias):
    m = np.arange(CMP_BAND) - CMP_BAND_BACK
    dist = np.arange(QB)[None, :] - (m[:, None] * CMP_BLOCK + CMP_BLOCK - 1)
    delta = _bias_table(rel_bias, dist) - rel_bias.astype(F32)[N_BUCKETS - 1][:, None, None]
    hi = delta.astype(BF16)
    lo = (delta - hi.astype(F32)).astype(BF16)
    return jnp.stack([hi, lo], axis=1)


def _select_blocks(score_ref, n_rows, n_keep):
    sc = score_ref[...]
    blk = _iota(sc.shape, 0)

    def body(i, rank):
        r = score_ref[pl.ds(i, 1), :]
        better = (r > sc) | ((r == sc) & (i < blk))
        return rank + jnp.where(better, 1.0, 0.0)

    rank = lax.fori_loop(0, n_rows, body, jnp.zeros(sc.shape, F32), unroll=8)
    return rank < n_keep


def _nsa_prompt_kernel(q_ref, kc_ref, vct_ref, ks_ref, vst_ref, kw_ref, vwt_ref, bc_ref, b0_ref, c31_ref,
                       gate_ref, o_ref, impt_ref, score_ref, *, n_sel_blocks, n_keep):
    qi = pl.program_id(2)
    q0 = qi * QB
    n_tiles = qi // (KT // QB) + 1
    nc = kc_ref.shape[2]
    ns = nc // 2
    heads = range(NSA_REP)
    w = NSA_REP * QB
    qcat = jnp.concatenate([(q_ref[0, 0, r].astype(F32) * ATT_SCALE).astype(BF16) for r in heads], axis=0)
    on_lanes = lambda rows: jnp.concatenate(rows, axis=1)

    valid_c = _iota((nc, w), 1) % QB + q0 - (_iota((nc, w), 0) * CMP_BLOCK + (CMP_BLOCK - 1)) >= 0
    first_blk = qi * (QB // CMP_BLOCK) - CMP_BAND_BACK
    place = jnp.where(_iota((nc, CMP_BAND), 0) == _iota((nc, CMP_BAND), 1) + first_blk, 1.0, 0.0).astype(BF16)
    bias = c31_ref[0] + _dot(place, bc_ref[0, 0]) + _dot(place, bc_ref[0, 1])
    p = _softmax_full_t(_dot_nt(kc_ref[0, 0], qcat) + bias, valid_c)
    o_c = _dot(vct_ref[0, 0], p.astype(BF16))
    imp = p[:, :QB]
    for r in range(1, NSA_REP):
        imp = imp + p[:, r * QB:(r + 1) * QB]
    impt_ref[...] = imp
    imp_s = impt_ref[pl.ds(0, ns, stride=2), :] + impt_ref[pl.ds(1, ns, stride=2), :]
    blk = _iota((ns, QB), 0)
    qpos = _iota((ns, QB), 1) + q0
    forced = (blk == qpos // SEL_BLOCK) | (blk == 0)
    future = blk * SEL_BLOCK > qpos
    score = jnp.where(forced, FORCE, jnp.where(future, -FORCE, imp_s))
    score_ref[...] = jnp.where(blk < n_sel_blocks, score, -3e38)
    sel_t = jnp.where(_select_blocks(score_ref, ns, n_keep), 1.0, 0.0).astype(BF16)

    key = _iota((KT, w), 0)
    qry = _iota((KT, w), 1) % QB
    causal0 = qry + KPAD - key >= 0
    key_blk = _iota((KT, ns), 0) // SEL_BLOCK
    blk_e = _iota((KT, ns), 1)

    def branch(k_ref, vt_ref, hi, valid_fn):
        def tile(j, state, first):
            start = pl.multiple_of(q0 - j * KT, QB)
            k = k_ref[0, 0, pl.ds(start, KT), :]
            vt = vt_ref[0, 0, :, pl.ds(start, KT)]
            s = _dot_nt(k, qcat) + (b0_ref[0] if first else c31_ref[0])
            return _softmax_step_t(s, valid_fn(j, start, first), *state, vt)

        init = (jnp.full((1, w), NEG, F32), jnp.zeros((1, w), F32), jnp.zeros((HEAD_DIM, w), F32))
        state = tile(0, init, True)
        _, l, acc = lax.fori_loop(1, hi, lambda j, st: tile(j, st, False), state)
        return acc / jnp.maximum(l, 1e-30)

    def valid_sel(j, start, first):
        base = start // SEL_BLOCK - KPAD // SEL_BLOCK
        expand = jnp.where(blk_e == key_blk + base, 1.0, 0.0).astype(BF16)
        chosen = on_lanes([_dot(expand, sel_t)] * NSA_REP) > 0.5
        return (chosen & causal0) if first else chosen

    def valid_win(j, start, first):
        ok = key >= KPAD - start
        if first:
            return ok & causal0
        return ok & (qry + KPAD - key + j * KT < WINDOW)

    o_s = branch(ks_ref, vst_ref, n_tiles, valid_sel)
    o_w = branch(kw_ref, vwt_ref, jnp.minimum(n_tiles, (WINDOW + QB - 1) // KT + 1), valid_win)

    gate = jax.nn.sigmoid(gate_ref[0, 0].astype(F32))
    g0, g1, g2 = (on_lanes([gate[3 * r + i:3 * r + i + 1, :] for r in heads]) for i in range(3))
    o = g0 * o_c + g1 * o_s + g2 * o_w
    o_ref[0] = jnp.concatenate([o[:, r * QB:(r + 1) * QB] for r in heads], axis=0).T.astype(o_ref.dtype)


def nsa_prompt(q, kc, vct, ks, vst, kw, vwt, bias_c, bias0_t, c31, gates_t, n_sel_blocks):
    b, g, _, t, _ = q.shape
    tp = ks.shape[2]
    nc = kc.shape[2]
    k_spec = pl.BlockSpec((1, 1, tp, HEAD_DIM), lambda b_, g_, i: (b_, g_, 0, 0))
    vt_spec = pl.BlockSpec((1, 1, HEAD_DIM, tp), lambda b_, g_, i: (b_, g_, 0, 0))
    kern = functools.partial(_nsa_prompt_kernel, n_sel_blocks=n_sel_blocks,
                             n_keep=min(N_SEL, n_sel_blocks))
    w = NSA_REP * QB
    b0 = _heads_on_lanes(bias0_t, NSA_REP)
    c31 = _heads_on_lanes(jnp.broadcast_to(c31[:, None, None], (NSA_HEADS, 1, QB)), NSA_REP)
    band = _heads_on_lanes(bias_c.reshape(NSA_HEADS, 2 * CMP_BAND, QB), NSA_REP).reshape(g, 2, CMP_BAND, w)
    return pl.pallas_call(
        kern, grid=(b, g, t // QB),
        in_specs=[pl.BlockSpec((1, 1, NSA_REP, QB, HEAD_DIM), lambda b_, g_, i: (b_, g_, 0, i, 0)),
                  pl.BlockSpec((1, 1, nc, HEAD_DIM), lambda b_, g_, i: (b_, g_, 0, 0)),
                  pl.BlockSpec((1, 1, HEAD_DIM, nc), lambda b_, g_, i: (b_, g_, 0, 0)),
                  k_spec, vt_spec, k_spec, vt_spec,
                  pl.BlockSpec((1, 2, CMP_BAND, w), lambda b_, g_, i: (g_, 0, 0, 0)),
                  pl.BlockSpec((1, KT, w), lambda b_, g_, i: (g_, 0, 0)),
                  pl.BlockSpec((1, 1, w), lambda b_, g_, i: (g_, 0, 0)),
                  pl.BlockSpec((1, 1, 4 * NSA_REP, QB), lambda b_, g_, i: (b_, g_, 0, i))],
        out_specs=pl.BlockSpec((1, QB, NSA_REP * HEAD_DIM), lambda b_, g_, i: (b_, i, g_)),
        out_shape=jax.ShapeDtypeStruct((b, t, NSA_HEADS * HEAD_DIM), BF16),
        scratch_shapes=[pltpu.VMEM((nc, QB), F32), pltpu.VMEM((nc // 2, QB), F32)],
        compiler_params=_params(("arbitrary",) * 3), name="nsa_prompt")(
            q, kc, vct, ks, vst, kw, vwt, band, b0, c31, gates_t)


CMP_PP = 16
CMP_HALF = CMP_BLOCK // 2


def _compress_kernel(pt_ref, w_ref, pe_ref, *rest, pp):
    pages, o_ref, lo_ref, hi_ref = rest[:pp], rest[pp], rest[pp + 1], rest[pp + 2]
    width = 2 * HEAD_DIM
    acc = jnp.zeros((pp * SUBLANES, 2 * width), F32)
    for i in range(CMP_HALF):
        rows = [pages[p][pl.ds(i, SUBLANES, stride=CMP_HALF), :] + pe_ref[i] for p in range(pp)]
        acc = acc + _dot(jnp.concatenate(rows, axis=0).astype(BF16), w_ref[0, i])
    lo_ref[...] = acc[:, :width]
    hi_ref[...] = acc[:, width:]
    n = pp * SUBLANES // 2
    o_ref[0] = lo_ref[pl.ds(0, n, stride=2), :] + hi_ref[pl.ds(1, n, stride=2), :]


def compress_pages(page_table, src, layer, col_block, w_cat, pe_tiles):
    nseq, n_pages = page_table.shape
    pp = min(CMP_PP, n_pages)
    width = 2 * HEAD_DIM

    def page_spec(p):
        return pl.BlockSpec((None, None, PAGE, width),
                            lambda b, s, kv, pt: (pt[b, s * pp + p], layer, 0, col_block + kv))

    grid_spec = pltpu.PrefetchScalarGridSpec(
        num_scalar_prefetch=1, grid=(nseq, n_pages // pp, 2),
        in_specs=[pl.BlockSpec((1, CMP_HALF, width, 2 * width), lambda b, s, kv, pt: (kv, 0, 0, 0)),
                  pl.BlockSpec((CMP_HALF, SUBLANES, width), lambda b, s, kv, pt: (0, 0, 0))]
        + [page_spec(p) for p in range(pp)],
        out_specs=pl.BlockSpec((1, pp * 4, width), lambda b, s, kv, pt: (b, s, kv)),
        scratch_shapes=[pltpu.VMEM((pp * SUBLANES, width), F32), pltpu.VMEM((pp * SUBLANES, width), F32)])
    return pl.pallas_call(
        functools.partial(_compress_kernel, pp=pp), grid_spec=grid_spec,
        out_shape=jax.ShapeDtypeStruct((nseq, n_pages * 4, 2 * width), F32),
        compiler_params=_params(("arbitrary",) * 3), name="compress")(
            page_table, w_cat, pe_tiles, *([src] * pp))


def _compress_weights(pe, wk, wv):
    def cat(w):
        w = w.reshape(CMP_BLOCK, HEAD_DIM, HEAD_DIM)
        z = jnp.zeros_like(w)
        full = jnp.concatenate([jnp.concatenate([w, z], -1), jnp.concatenate([z, w], -1)], axis=1)
        return jnp.concatenate([full[:CMP_HALF], full[CMP_HALF:]], axis=-1)

    w_cat = jnp.stack([cat(wk), cat(wv)]).astype(BF16)
    pe2 = jnp.tile(pe.astype(F32), (1, 2))
    pe_tiles = jnp.stack([pe2[:CMP_HALF], pe2[CMP_HALF:]], axis=1)
    pe_tiles = jnp.tile(pe_tiles, (1, SUBLANES // 2, 1))
    return w_cat, pe_tiles


SB_PP = 8
DIFF_PP = 8
SEL_PP = 16


def _page_specs(pp, n_pages, width, col_block, layer, reverse=True):
    def spec(p):
        def index(b, s, pt):
            logical = n_pages - 1 - (s * pp + p) if reverse else s * pp + p
            return (pt[b, logical], layer, 0, col_block)
        return pl.BlockSpec((None, None, PAGE, width), index)
    return [spec(p) for p in range(pp)]


def _token_of_row(shape):
    return _iota(shape, 0) % DEC_S


def _sb_sample_kernel(pt_ref, q_ref, new_ref, u_ref, *rest, pp):
    pages, o_ref, carry_ref, acc_ref = rest[:pp], rest[pp], rest[pp + 1], rest[pp + 2]
    s = pl.program_id(1)
    q = q_ref[0]
    u = u_ref[...]
    rows = SB_HEADS * DEC_S

    @pl.when(s == 0)
    def _():
        k = new_ref[0, :, :SB_W]
        v = new_ref[0, :, SB_W:]
        valid = _iota((rows, PAGE), 1) < _token_of_row((rows, PAGE))
        carry, acc = _sb_step(_dot_nt(q, k), valid, jnp.zeros((rows, 1), F32),
                              jnp.zeros((rows, SB_W), F32), v, u)
        carry_ref[...] = carry
        acc_ref[...] = acc

    for p in range(pp):
        @pl.when(jnp.max(carry_ref[...]) > SB_DEAD)
        def _(p=p):
            page = pages[p][...]
            k = page[:, :SB_W].astype(BF16)
            v = page[:, SB_W:].astype(BF16)
            carry, acc = _sb_step(_dot_nt(q, k), None, carry_ref[...], acc_ref[...], v, u)
            carry_ref[...] = carry
            acc_ref[...] = acc

    @pl.when(s == pl.num_programs(1) - 1)
    def _():
        a = acc_ref[...]
        o_ref[0] = jnp.concatenate(
            [a[h * DEC_S:(h + 1) * DEC_S, h * HEAD_DIM:(h + 1) * HEAD_DIM] for h in range(SB_HEADS)],
            axis=-1).astype(o_ref.dtype)


def sb_sample(page_table, qbd, new_kv, cache, layer):
    db, n_pages = page_table.shape
    pp = min(SB_PP, n_pages)
    rows = SB_HEADS * DEC_S
    grid_spec = pltpu.PrefetchScalarGridSpec(
        num_scalar_prefetch=1, grid=(db, n_pages // pp),
        in_specs=[pl.BlockSpec((1, rows, SB_W), lambda b, s, pt: (b, 0, 0)),
                  pl.BlockSpec((1, PAGE, 2 * SB_W), lambda b, s, pt: (b, 0, 0)),
                  pl.BlockSpec((PAGE, PAGE), lambda b, s, pt: (0, 0))]
        + _page_specs(pp, n_pages, 2 * SB_W, 0, layer),
        out_specs=pl.BlockSpec((1, DEC_S, SB_W), lambda b, s, pt: (b, 0, 0)),
        scratch_shapes=[pltpu.VMEM((rows, 1), F32), pltpu.VMEM((rows, SB_W), F32)])
    return pl.pallas_call(
        functools.partial(_sb_sample_kernel, pp=pp), grid_spec=grid_spec,
        out_shape=jax.ShapeDtypeStruct((db, DEC_S, SB_W), BF16),
        compiler_params=_params(("arbitrary",) * 2), name="sb_sample")(
            page_table, qbd, new_kv, _later_matrix(PAGE), *([cache] * pp))


def _diff_sample_kernel(pt_ref, q_ref, new_ref, cvec_ref, blast_ref, bnew_ref, lam_ref, sg_ref, *rest,
                        pp, lam_init):
    pages, o_ref, m_ref, l_ref, acc_ref = rest[:pp], rest[pp], rest[pp + 1], rest[pp + 2], rest[pp + 3]
    s = pl.program_id(1)
    q = q_ref[0]
    rows = DIFF_HEADS * 2 * DEC_S

    @pl.when(s == 0)
    def _():
        k = new_ref[0, :, :DIFF_KV_W]
        v = new_ref[0, :, DIFF_KV_W:]
        valid = _iota((rows, PAGE), 1) <= _token_of_row((rows, PAGE))
        m, l, acc = _softmax_step(_dot_nt(q, k) + bnew_ref[...], valid, jnp.full((rows, 1), NEG, F32),
                                  jnp.zeros((rows, 1), F32), jnp.zeros((rows, DIFF_KV_W), F32), v)
        m_ref[...] = m
        l_ref[...] = l
        acc_ref[...] = acc

    m, l, acc = m_ref[...], l_ref[...], acc_ref[...]
    for p in range(pp):
        page = pages[p][...]
        k = page[:, :DIFF_KV_W].astype(BF16)
        v = page[:, DIFF_KV_W:].astype(BF16)
        bias = cvec_ref[...]
        if p == 0:
            bias = jnp.where(s == 0, blast_ref[...], bias)
        m, l, acc = _softmax_step(_dot_nt(q, k) + bias, None, m, l, acc, v)
    m_ref[...] = m
    l_ref[...] = l
    acc_ref[...] = acc

    @pl.when(s == pl.num_programs(1) - 1)
    def _():
        a = acc_ref[...]
        ls = l_ref[...]
        lam = _diff_lambda(lam_ref, lam_init)
        width = 2 * HEAD_DIM
        outs = []
        for g in range(DIFF_KV_HEADS):
            for r in range(DIFF_REP):
                r0 = (g * DIFF_REP + r) * 2 * DEC_S
                o1 = a[r0:r0 + DEC_S, g * width:(g + 1) * width]
                o2 = a[r0 + DEC_S:r0 + 2 * DEC_S, g * width:(g + 1) * width]
                outs.append(_diff_finish(o1, ls[r0:r0 + DEC_S], o2, ls[r0 + DEC_S:r0 + 2 * DEC_S],
                                         lam, sg_ref[...], lam_init))
        o_ref[0] = jnp.concatenate(outs, axis=-1).astype(o_ref.dtype)


def diff_sample(page_table, qbd, new_kv, cvec, blast, bnew, lam_rows, sub_g, cache, layer, lam_init):
    db, n_pages = page_table.shape
    pp = min(DIFF_PP, n_pages)
    rows = DIFF_HEADS * 2 * DEC_S
    const = lambda shape: pl.BlockSpec(shape, lambda b, s, pt: (0,) * len(shape))
    grid_spec = pltpu.PrefetchScalarGridSpec(
        num_scalar_prefetch=1, grid=(db, n_pages // pp),
        in_specs=[pl.BlockSpec((1, rows, DIFF_KV_W), lambda b, s, pt: (b, 0, 0)),
                  pl.BlockSpec((1, PAGE, 2 * DIFF_KV_W), lambda b, s, pt: (b, 0, 0)),
                  const((rows, 1)), const((rows, PAGE)), const((rows, PAGE)),
                  const((4, HEAD_DIM)), const((1, 2 * HEAD_DIM))]
        + _page_specs(pp, n_pages, 2 * DIFF_KV_W, 0, layer),
        out_specs=pl.BlockSpec((1, DEC_S, DIFF_HEADS * 2 * HEAD_DIM), lambda b, s, pt: (b, 0, 0)),
        scratch_shapes=[pltpu.VMEM((rows, 1), F32), pltpu.VMEM((rows, 1), F32),
                        pltpu.VMEM((rows, DIFF_KV_W), F32)])
    return pl.pallas_call(
        functools.partial(_diff_sample_kernel, pp=pp, lam_init=lam_init), grid_spec=grid_spec,
        out_shape=jax.ShapeDtypeStruct((db, DEC_S, DIFF_HEADS * 2 * HEAD_DIM), BF16),
        compiler_params=_params(("arbitrary",) * 2), name="diff_sample")(
            page_table, qbd, new_kv, cvec, blast, bnew, lam_rows, sub_g, *([cache] * pp))


NSA_ROWS = NSA_HEADS * DEC_S
NSA_GT = NSA_GROUPS * DEC_S


def _nsa_diag(a):
    pieces = []
    for r in range(NSA_REP):
        for g in range(NSA_GROUPS):
            r0 = r * NSA_GT + g * DEC_S
            pieces.append(a[r0:r0 + DEC_S, g * HEAD_DIM:(g + 1) * HEAD_DIM])
    return jnp.concatenate(pieces, axis=0)


def _nsa_sample_cw_kernel(q_ref, kcvc_ref, bc_ref, win_ref, bw_ref, oc_ref, ow_ref, sel_ref,
                          impt_ref, score_ref, *, win_len, n_keep):
    q = q_ref[0]
    nc = kcvc_ref.shape[1]
    ns = nc // 2
    kc = kcvc_ref[0, :, :NSA_KV_W].astype(BF16)
    vc = kcvc_ref[0, :, NSA_KV_W:].astype(BF16)
    p = _softmax_full(_dot_nt(q, kc) + bc_ref[...], None)
    oc_ref[0] = _nsa_diag(_dot(p.astype(BF16), vc))
    imp = p[0:NSA_GT]
    for r in range(1, NSA_REP):
        imp = imp + p[r * NSA_GT:(r + 1) * NSA_GT]
    imp = jnp.concatenate([imp, jnp.zeros((LANES - NSA_GT, nc), F32)], axis=0)
    impt_ref[...] = imp.T
    imp_s = impt_ref[pl.ds(0, ns, stride=2), :] + impt_ref[pl.ds(1, ns, stride=2), :]
    blk = _iota((ns, LANES), 0)
    score_ref[...] = jnp.where(blk == 0, FORCE, imp_s)
    sel_t = jnp.where(_select_blocks(score_ref, ns, n_keep), 1.0, 0.0)
    sel_ref[0] = sel_t.T[:NSA_GT].astype(sel_ref.dtype)

    wl = win_ref.shape[1]
    kw = win_ref[0, :, :NSA_KV_W]
    vw = win_ref[0, :, NSA_KV_W:]
    tok = _token_of_row((NSA_ROWS, wl))
    col = _iota((NSA_ROWS, wl), 1)
    valid = (col > tok + (win_len - WINDOW)) & (col <= tok + win_len) & (col < win_len + DEC_S)
    pw = _softmax_full(_dot_nt(q, kw) + bw_ref[...], valid)
    ow_ref[0] = _nsa_diag(_dot(pw.astype(BF16), vw))


def nsa_sample_cw(qbd, kcvc, bias_c, win_all, bias_w, win_len, n_keep):
    db = qbd.shape[0]
    nc = kcvc.shape[1]
    wl = win_all.shape[1]
    const = lambda shape: pl.BlockSpec(shape, lambda b: (0,) * len(shape))
    return pl.pallas_call(
        functools.partial(_nsa_sample_cw_kernel, win_len=win_len, n_keep=n_keep), grid=(db,),
        in_specs=[pl.BlockSpec((1, NSA_ROWS, NSA_KV_W), lambda b: (b, 0, 0)),
                  pl.BlockSpec((1, nc, 2 * NSA_KV_W), lambda b: (b, 0, 0)),
                  const((NSA_ROWS, nc)),
                  pl.BlockSpec((1, wl, 2 * NSA_KV_W), lambda b: (b, 0, 0)),
                  const((NSA_ROWS, wl))],
        out_specs=[pl.BlockSpec((1, NSA_ROWS, HEAD_DIM), lambda b: (b, 0, 0)),
                   pl.BlockSpec((1, NSA_ROWS, HEAD_DIM), lambda b: (b, 0, 0)),
                   pl.BlockSpec((1, NSA_GT, nc // 2), lambda b: (b, 0, 0))],
        out_shape=[jax.ShapeDtypeStruct((db, NSA_ROWS, HEAD_DIM), F32),
                   jax.ShapeDtypeStruct((db, NSA_ROWS, HEAD_DIM), F32),
                   jax.ShapeDtypeStruct((db, NSA_GT, nc // 2), BF16)],
        scratch_shapes=[pltpu.VMEM((nc, LANES), F32), pltpu.VMEM((nc // 2, LANES), F32)],
        compiler_params=_params(("arbitrary",)), name="nsa_sample_cw")(qbd, kcvc, bias_c, win_all, bias_w)


def _nsa_sample_sel_kernel(pt_ref, q_ref, new_ref, sel_ref, cvec_ref, blast_ref, bnew_ref, oc_ref, ow_ref,
                           gate_ref, *rest, pp, n_pages):
    pages, o_ref, m_ref, l_ref, acc_ref = rest[:pp], rest[pp], rest[pp + 1], rest[pp + 2], rest[pp + 3]
    s = pl.program_id(1)
    q = q_ref[0]
    sel = sel_ref[0]
    ns = sel.shape[1]

    @pl.when(s == 0)
    def _():
        k = new_ref[0, :, :NSA_KV_W]
        v = new_ref[0, :, NSA_KV_W:]
        valid = _iota((NSA_ROWS, PAGE), 1) <= _token_of_row((NSA_ROWS, PAGE))
        m, l, acc = _softmax_step(_dot_nt(q, k) + bnew_ref[...], valid, jnp.full((NSA_ROWS, 1), NEG, F32),
                                  jnp.zeros((NSA_ROWS, 1), F32), jnp.zeros((NSA_ROWS, NSA_KV_W), F32), v)
        m_ref[...] = m
        l_ref[...] = l
        acc_ref[...] = acc

    m, l, acc = m_ref[...], l_ref[...], acc_ref[...]
    blk_e = _iota((ns, PAGE), 0)
    col_e = _iota((ns, PAGE), 1) // SEL_BLOCK
    for p in range(pp):
        logical = n_pages - 1 - (s * pp + p)
        page = pages[p][...]
        k = page[:, :NSA_KV_W].astype(BF16)
        v = page[:, NSA_KV_W:].astype(BF16)
        expand = jnp.where(blk_e == col_e + logical * (PAGE // SEL_BLOCK), 1.0, 0.0).astype(BF16)
        chosen = _dot(sel, expand) > 0.5
        valid = jnp.concatenate([chosen] * NSA_REP, axis=0)
        bias = cvec_ref[...]
        if p == 0:
            bias = jnp.where(s == 0, blast_ref[...], bias)
        m, l, acc = _softmax_step(_dot_nt(q, k) + bias, valid, m, l, acc, v)
    m_ref[...] = m
    l_ref[...] = l
    acc_ref[...] = acc

    @pl.when(s == pl.num_programs(1) - 1)
    def _():
        o_s = _nsa_diag(acc_ref[...] / jnp.maximum(l_ref[...], 1e-30))
        gate = jax.nn.sigmoid(gate_ref[0].astype(F32))
        o = gate[:, 0:1] * oc_ref[0] + gate[:, 1:2] * o_s + gate[:, 2:3] * ow_ref[0]
        pieces = []
        for g in range(NSA_GROUPS):
            for r in range(NSA_REP):
                r0 = r * NSA_GT + g * DEC_S
                pieces.append(o[r0:r0 + DEC_S])
        o_ref[0] = jnp.concatenate(pieces, axis=-1).astype(o_ref.dtype)


def nsa_sample_sel(page_table, qbd, new_kv, sel, cvec, blast, bnew, o_c, o_w, gates, cache, layer):
    db, n_pages = page_table.shape
    pp = min(SEL_PP, n_pages)
    ns = sel.shape[2]
    const = lambda shape: pl.BlockSpec(shape, lambda b, s, pt: (0,) * len(shape))
    per_seq = lambda shape: pl.BlockSpec((1,) + shape, lambda b, s, pt: (b,) + (0,) * len(shape))
    grid_spec = pltpu.PrefetchScalarGridSpec(
        num_scalar_prefetch=1, grid=(db, n_pages // pp),
        in_specs=[per_seq((NSA_ROWS, NSA_KV_W)), per_seq((PAGE, 2 * NSA_KV_W)), per_seq((NSA_GT, ns)),
                  const((NSA_ROWS, 1)), const((NSA_ROWS, PAGE)), const((NSA_ROWS, PAGE)),
                  per_seq((NSA_ROWS, HEAD_DIM)), per_seq((NSA_ROWS, HEAD_DIM)), per_seq((NSA_ROWS, LANES))]
        + _page_specs(pp, n_pages, 2 * NSA_KV_W, 1, layer),
        out_specs=pl.BlockSpec((1, DEC_S, NSA_HEADS * HEAD_DIM), lambda b, s, pt: (b, 0, 0)),
        scratch_shapes=[pltpu.VMEM((NSA_ROWS, 1), F32), pltpu.VMEM((NSA_ROWS, 1), F32),
                        pltpu.VMEM((NSA_ROWS, NSA_KV_W), F32)])
    return pl.pallas_call(
        functools.partial(_nsa_sample_sel_kernel, pp=pp, n_pages=n_pages), grid_spec=grid_spec,
        out_shape=jax.ShapeDtypeStruct((db, DEC_S, NSA_HEADS * HEAD_DIM), BF16),
        compiler_params=_params(("arbitrary",) * 2), name="nsa_sample_sel")(
            page_table, qbd, new_kv, sel, cvec, blast, bnew, o_c, o_w, gates, *([cache] * pp))


def _layer_norm(y, g, b):
    mu = jnp.mean(y, axis=-1, keepdims=True)
    d = y - mu
    var = jnp.mean(d * d, axis=-1, keepdims=True)
    return d * lax.rsqrt(var + LN_EPS) * g + b


def _split_bf16(x):
    hi = x.astype(BF16)
    return hi, (x - hi.astype(F32)).astype(BF16)


def _mix_router_kernel(a1_ref, a2_ref, w1_ref, w2_ref, x_ref, g_ref, b_ref, wr_ref, br_ref, u_ref,
                       xn_ref, xb_ref, rt_ref, gt_ref, cnt_ref, *, alpha):
    mix = _dot(a1_ref[...], w1_ref[...]) + _dot(a2_ref[...], w2_ref[...])
    xn = _layer_norm(alpha * x_ref[...] + mix, g_ref[...], b_ref[...])
    xn_ref[...] = xn
    xb_ref[...] = xn.astype(BF16)
    xh, xl = _split_bf16(xn)
    wh, wl = _split_bf16(wr_ref[...])
    logits = _dot_nt(wh, xh) + _dot_nt(wh, xl) + _dot_nt(wl, xh) + br_ref[...]
    e_iota = _iota(logits.shape, 0)
    work = logits
    vals, sels = [], []
    for _ in range(TOP_K):
        top = jnp.max(work, axis=0, keepdims=True)
        idx = jnp.min(jnp.where(work == top, e_iota, N_EXPERTS), axis=0, keepdims=True)
        sel = e_iota == idx
        vals.append(top)
        sels.append(sel)
        work = jnp.where(sel, -jnp.inf, work)
    ex = [jnp.exp(v - vals[0]) for v in vals]
    den = ex[0] + ex[1] + ex[2] + ex[3]
    gate = jnp.zeros(logits.shape, F32)
    chosen = sels[0]
    for k in range(TOP_K):
        gate = jnp.where(sels[k], ex[k] / den, gate)
        chosen = chosen | sels[k]
    cf = jnp.where(chosen, 1.0, 0.0)
    rank = _dot(cf.astype(BF16), u_ref[...])
    rt_ref[0] = jnp.where(chosen, rank, -1.0)
    gt_ref[0] = gate
    cnt = jnp.sum(cf, axis=1, keepdims=True)
    cnt_ref[0] = jnp.broadcast_to(cnt, (N_EXPERTS, LANES)).astype(jnp.int32)


def mix_router(a1, a2, w1, w2, x, ln_g, ln_b, w_router_t, b_router, tt, alpha):
    n = x.shape[0]
    half = a1.shape[1]
    j = np.arange(tt)
    before = jnp.asarray((j[:, None] < j[None, :]).astype(np.float32), dtype=BF16)
    tile = lambda w: pl.BlockSpec((tt, w), lambda i: (i, 0))
    const = lambda shape: pl.BlockSpec(shape, lambda i: (0,) * len(shape))
    route = pl.BlockSpec((1, N_EXPERTS, tt), lambda i: (i, 0, 0))
    return pl.pallas_call(
        functools.partial(_mix_router_kernel, alpha=alpha), grid=(n // tt,),
        in_specs=[tile(half), tile(half), const((half, D_MODEL)), const((half, D_MODEL)), tile(D_MODEL),
                  const((1, D_MODEL)), const((1, D_MODEL)), const((N_EXPERTS, D_MODEL)),
                  const((N_EXPERTS, 1)), const((tt, tt))],
        out_specs=[tile(D_MODEL), tile(D_MODEL), route, route,
                   pl.BlockSpec((1, N_EXPERTS, LANES), lambda i: (i, 0, 0))],
        out_shape=[jax.ShapeDtypeStruct((n, D_MODEL), F32), jax.ShapeDtypeStruct((n, D_MODEL), BF16),
                   jax.ShapeDtypeStruct((n // tt, N_EXPERTS, tt), F32),
                   jax.ShapeDtypeStruct((n // tt, N_EXPERTS, tt), F32),
                   jax.ShapeDtypeStruct((n // tt, N_EXPERTS, LANES), jnp.int32)],
        compiler_params=_params(("arbitrary",)), name="mix_router")(
            a1, a2, w1, w2, x, ln_g, ln_b, w_router_t, b_router, before)


def _pair_permutation():
    c = np.arange(2 * LANES)
    dest = np.where(c % 2 == 0, c // 2, LANES + c // 2)
    return jnp.asarray((dest[:, None] == c[None, :]).astype(np.float32), dtype=BF16)


def _split_pairs_kernel(w_ref, s_ref, o_ref):
    s = s_ref[...]
    for blk in range(w_ref.shape[2] // (2 * LANES)):
        cols = slice(blk * 2 * LANES, (blk + 1) * 2 * LANES)
        o_ref[0, :, cols] = _dot(w_ref[0, :, cols].astype(BF16), s).astype(BF16)


def split_pairs(w_up):
    e, d, f2 = w_up.shape
    return pl.pallas_call(
        _split_pairs_kernel, grid=(e,),
        in_specs=[pl.BlockSpec((1, d, f2), lambda i: (i, 0, 0)),
                  pl.BlockSpec((2 * LANES, 2 * LANES), lambda i: (0, 0))],
        out_specs=pl.BlockSpec((1, d, f2), lambda i: (i, 0, 0)),
        out_shape=jax.ShapeDtypeStruct((e, d, f2), BF16),
        compiler_params=_params(("arbitrary",)), name="split_pairs")(w_up, _pair_permutation())


def _moe_kernel(cnt_ref, xb_ref, xn_ref, rt_ref, gt_ref, wu_ref, bu_ref, wd_ref, bd_ref,
                lg_ref, lb_ref, y_ref, yb_ref, acc_ref, *, alpha):
    i = pl.program_id(0)
    e = pl.program_id(1)
    tt = xb_ref.shape[0]

    @pl.when(e == 0)
    def _():
        acc_ref[...] = jnp.zeros_like(acc_ref)

    n_chunks = (cnt_ref[i, e] + MOE_CH - 1) // MOE_CH
    slot = rt_ref[0, pl.ds(e, 1), :]
    gate = gt_ref[0, pl.ds(e, 1), :]

    def chunk(c, _):
        want = (_iota((MOE_CH, tt), 0) + c * MOE_CH).astype(F32)
        hit = slot == want
        xe = _dot(jnp.where(hit, 1.0, 0.0).astype(BF16), xb_ref[...]).astype(BF16)
        h = _dot(xe, wu_ref[0]) + bu_ref[0]
        acts = []
        for blk in range(h.shape[1] // (2 * LANES)):
            h_glu = jnp.minimum(h[:, blk * 2 * LANES:blk * 2 * LANES + LANES], SWIGLU_LIMIT)
            h_lin = jnp.clip(h[:, blk * 2 * LANES + LANES:(blk + 1) * 2 * LANES], -SWIGLU_LIMIT, SWIGLU_LIMIT)
            acts.append((h_glu * jax.nn.sigmoid(SWIGLU_ALPHA * h_glu) * (h_lin + 1.0)).astype(BF16))
        y = _dot(jnp.concatenate(acts, axis=-1), wd_ref[0]) + bd_ref[0]
        back = jnp.where(hit, gate, 0.0).astype(BF16)
        acc_ref[...] += _dot_tn(back, y.astype(BF16))
        return 0

    lax.fori_loop(0, n_chunks, chunk, 0)

    @pl.when(e == pl.num_programs(1) - 1)
    def _():
        y = _layer_norm(alpha * xn_ref[...] + acc_ref[...], lg_ref[...], lb_ref[...])
        y_ref[...] = y
        yb_ref[...] = y.astype(BF16)


def moe_ln(cnt, xb, xn, rt, gt, w_up, b_up, w_down, b_down, ln_g, ln_b, tt, alpha):
    n = xn.shape[0]
    ff = w_down.shape[1]
    tile = pl.BlockSpec((tt, D_MODEL), lambda i, e, c: (i, 0))
    route = pl.BlockSpec((1, N_EXPERTS, tt), lambda i, e, c: (i, 0, 0))
    per_e = lambda shape: pl.BlockSpec((1,) + shape, lambda i, e, c: (e, 0, 0))
    const = pl.BlockSpec((1, D_MODEL), lambda i, e, c: (0, 0))
    grid_spec = pltpu.PrefetchScalarGridSpec(
        num_scalar_prefetch=1, grid=(n // tt, N_EXPERTS),
        in_specs=[tile, tile, route, route, per_e((D_MODEL, 2 * ff)), per_e((1, 2 * ff)),
                  per_e((ff, D_MODEL)), per_e((1, D_MODEL)), const, const],
        out_specs=[tile, tile],
        scratch_shapes=[pltpu.VMEM((tt, D_MODEL), F32)])
    return pl.pallas_call(
        functools.partial(_moe_kernel, alpha=alpha), grid_spec=grid_spec,
        out_shape=[jax.ShapeDtypeStruct((n, D_MODEL), F32), jax.ShapeDtypeStruct((n, D_MODEL), BF16)],
        compiler_params=_params(("arbitrary",) * 2), name="moe_ln")(
            cnt, xb, xn, rt, gt, w_up, b_up, w_down, b_down, ln_g, ln_b)


MOE_TILE = 1024


def _split_pairs_bias(b_up):
    e, f2 = b_up.shape
    b = b_up.astype(F32).reshape(e, f2 // (2 * LANES), LANES, 2).transpose(0, 1, 3, 2)
    return b.reshape(e, 1, f2)


def _front_pad(a, axis):
    pad = [(0, 0)] * a.ndim
    pad[axis] = (KPAD, 0)
    return jnp.pad(a, pad)


def _pad_rows(a, rows):
    return jnp.pad(a, ((0, 0), (0, rows - a.shape[1]), (0, 0)))


def _channel_mixer(layer, a1, a2, w_out, x, p, tt):
    alpha = p["alpha"]
    half = a1.shape[1]
    w_out = w_out.astype(BF16)
    xn, xb, rt, gt, cnt = mix_router(
        a1, a2, w_out[:half], w_out[half:], x, p["ln_mix_g"][layer][None], p["ln_mix_b"][layer][None],
        p["w_router"][layer].T.astype(F32), p["b_router"][layer][:, None].astype(F32), tt, alpha)
    return moe_ln(cnt[:, :, 0], xb, xn, rt, gt, *p["experts"][layer],
                  p["ln_ffn_g"][layer][None], p["ln_ffn_b"][layer][None], tt, alpha)


def _even_prompt(h, b, t, cmp_w, bias_c, bias0, c31):
    h3 = h.reshape(b, t, -1)
    heads = lambda a, nh: a.reshape(b, t, nh, HEAD_DIM).transpose(0, 2, 1, 3).astype(BF16)
    heads_t = lambda a, nh: a.reshape(b, t, nh, HEAD_DIM).transpose(0, 2, 3, 1).astype(BF16)
    o = 3 * SB_W
    o_sb = sb_prompt(heads(h3[..., :SB_W], SB_HEADS),
                     _front_pad(heads(h3[..., SB_W:2 * SB_W], SB_HEADS), 2),
                     _front_pad(heads_t(h3[..., 2 * SB_W:o], SB_HEADS), 3))
    o_sb = o_sb.transpose(0, 2, 1, 3).reshape(b * t, SB_W)
    nq = h3[..., o:o + NSA_Q_W].reshape(b, t, NSA_GROUPS, NSA_REP, HEAD_DIM).transpose(0, 2, 3, 1, 4)
    o += NSA_Q_W
    nkv = h3[..., o:o + 6 * NSA_KV_W].reshape(b, t, 6, NSA_GROUPS, HEAD_DIM)
    gates = h3[..., o + 6 * NSA_KV_W:o + 6 * NSA_KV_W + NSA_GATE_W]
    gates = gates.reshape(b, t, NSA_GROUPS, NSA_REP * 3).transpose(0, 2, 3, 1)
    gates = jnp.pad(gates, ((0, 0), (0, 0), (0, NSA_REP), (0, 0)))
    n_pages = t // PAGE
    table = jnp.arange(b * n_pages, dtype=jnp.int32).reshape(b, n_pages)
    kcvc = compress_pages(table, h.reshape(b * n_pages, 1, PAGE, h.shape[-1]), 0, o // (2 * HEAD_DIM), *cmp_w)
    nc = t // CMP_BLOCK
    ncp = -(-nc // LANES) * LANES
    kcvc = kcvc.reshape(b, nc, 2, NSA_GROUPS, HEAD_DIM).astype(BF16)
    kc = jnp.pad(kcvc[:, :, 0].transpose(0, 2, 1, 3), ((0, 0), (0, 0), (0, ncp - nc), (0, 0)))
    vct = jnp.pad(kcvc[:, :, 1].transpose(0, 2, 3, 1), ((0, 0), (0, 0), (0, 0), (0, ncp - nc)))
    grp = lambda i: _front_pad(nkv[:, :, i].transpose(0, 2, 1, 3).astype(BF16), 2)
    grp_t = lambda i: _front_pad(nkv[:, :, i].transpose(0, 2, 3, 1).astype(BF16), 3)
    o_nsa = nsa_prompt(nq.astype(BF16), kc, vct, grp(2), grp_t(3), grp(4), grp_t(5),
                       bias_c, bias0, c31, gates, -(-t // SEL_BLOCK))
    sb_rows = h3[..., SB_W:3 * SB_W].reshape(b, t, 2, SB_HEADS, HEAD_DIM)
    keep = min(WINDOW, t)
    return (o_sb, o_nsa.reshape(b * t, NSA_Q_W), sb_rows, nkv[:, :, :4], nkv[:, t - keep:, 4:])


def _even_sample(h, db, page_table, cache_sb, cache_nsa, win_state, layer, cmp_w, tabs):
    h3 = h.reshape(db, DEC_S, -1)
    past = page_table.shape[1] * PAGE
    o = 3 * SB_W
    q = h3[..., :SB_W].reshape(db, DEC_S, SB_HEADS, HEAD_DIM).transpose(0, 2, 1, 3) * ATT_SCALE
    eye = jnp.eye(SB_HEADS, dtype=F32)
    qbd = (q[:, :, :, None, :] * eye[None, :, None, :, None]).reshape(db, SB_HEADS * DEC_S, SB_W)
    o_sb = sb_sample(page_table, qbd.astype(BF16), _pad_rows(h3[..., SB_W:o], PAGE).astype(BF16),
                     cache_sb, layer)
    q = h3[..., o:o + NSA_Q_W].reshape(db, DEC_S, NSA_GROUPS, NSA_REP, HEAD_DIM).transpose(0, 3, 2, 1, 4)
    eye = jnp.eye(NSA_GROUPS, dtype=F32)
    qbd = (q[:, :, :, :, None, :] * ATT_SCALE * eye[None, None, :, None, :, None])
    qbd = qbd.reshape(db, NSA_ROWS, NSA_KV_W).astype(BF16)
    o += NSA_Q_W
    nkv = h3[..., o:o + 6 * NSA_KV_W]
    gates = h3[..., o + 6 * NSA_KV_W:o + 6 * NSA_KV_W + NSA_GATE_W]
    gates = gates.reshape(db, DEC_S, NSA_GROUPS, NSA_REP, 3).transpose(0, 3, 2, 1, 4).reshape(db, NSA_ROWS, 3)
    gates = jnp.pad(gates, ((0, 0), (0, 0), (0, LANES - 3)))
    kcvc = compress_pages(page_table, cache_nsa, layer, 0, *cmp_w)
    win_len = win_state.shape[2]
    new_win = nkv[..., 4 * NSA_KV_W:]
    win_all = jnp.concatenate([win_state[:, layer].reshape(db, win_len, 2 * NSA_KV_W), new_win], axis=1)
    wl = tabs["bias_w"].shape[1]
    n_blocks = -(-(past + DEC_S) // SEL_BLOCK)
    o_c, o_w, sel = nsa_sample_cw(qbd, kcvc, tabs["bias_c"], _pad_rows(win_all, wl).astype(BF16),
                                  tabs["bias_w"], win_len, min(N_SEL, n_blocks) - 1)
    o_nsa = nsa_sample_sel(page_table, qbd, _pad_rows(nkv[..., 2 * NSA_KV_W:4 * NSA_KV_W], PAGE).astype(BF16),
                           sel, tabs["cvec"], tabs["blast"], tabs["bnew"], o_c, o_w, gates, cache_nsa, layer)
    sb_rows = h3[..., SB_W:3 * SB_W].reshape(db, DEC_S, 2, SB_HEADS, HEAD_DIM)
    nsa_rows = nkv[..., :4 * NSA_KV_W].reshape(db, DEC_S, 4, NSA_GROUPS, HEAD_DIM)
    win_rows = win_all[:, max(0, win_len + DEC_S - WINDOW):].reshape(db, -1, 2, NSA_GROUPS, HEAD_DIM)
    return (o_sb.reshape(db * DEC_S, SB_W), o_nsa.reshape(db * DEC_S, NSA_Q_W), sb_rows, nsa_rows, win_rows)


def _odd_prompt(h, b, t, bias0, c31, lam_rows, sub_g, lam_init):
    h3 = h.reshape(b, t, -1)
    q = h3[..., :DIFF_Q_W].reshape(b, t, DIFF_KV_HEADS, DIFF_REP, 2, HEAD_DIM).transpose(0, 2, 3, 4, 1, 5)
    kv = h3[..., DIFF_Q_W:].reshape(b, t, 2, DIFF_KV_HEADS, 2 * HEAD_DIM)
    k = kv[:, :, 0].reshape(b, t, DIFF_KV_HEADS, 2, HEAD_DIM).transpose(0, 2, 3, 1, 4)
    vt = kv[:, :, 1].transpose(0, 2, 3, 1)
    o = diff_prompt(q.astype(BF16), _front_pad(k.astype(BF16), 3), _front_pad(vt.astype(BF16), 3),
                    bias0, c31, lam_rows, sub_g.T, lam_init)
    return o.reshape(b * t, DIFF_Q_W), kv


def _odd_sample(h, db, page_table, cache, layer, tabs, lam_rows, sub_g, lam_init):
    h3 = h.reshape(db, DEC_S, -1)
    q = h3[..., :DIFF_Q_W].reshape(db, DEC_S, DIFF_KV_HEADS, DIFF_REP, 2, HEAD_DIM).transpose(0, 2, 3, 4, 1, 5)
    eye_g = jnp.eye(DIFF_KV_HEADS, dtype=F32)
    eye_c = jnp.eye(2, dtype=F32)
    qbd = (q[..., None, None, :] * ATT_SCALE * eye_g[None, :, None, None, None, :, None, None]
           * eye_c[None, None, None, :, None, None, :, None])
    qbd = qbd.reshape(db, DIFF_HEADS * 2 * DEC_S, DIFF_KV_W).astype(BF16)
    o = diff_sample(page_table, qbd, _pad_rows(h3[..., DIFF_Q_W:], PAGE).astype(BF16), tabs["cvec"],
                    tabs["blast"], tabs["bnew"], lam_rows, sub_g, cache, layer, lam_init)
    kv = h3[..., DIFF_Q_W:].reshape(db, DEC_S, 2, DIFF_KV_HEADS, 2 * HEAD_DIM)
    return o.reshape(db * DEC_S, DIFF_Q_W), kv


def _sample_tables(rel_bias, past, head_of_row, tok_of_row, with_cmp):
    tok = np.asarray(tok_of_row)
    key = np.arange(PAGE)
    tabs = {
        "cvec": rel_bias.astype(F32)[N_BUCKETS - 1][np.asarray(head_of_row)][:, None],
        "blast": _bias_table(rel_bias, tok[:, None] + PAGE - key[None, :], head_of_row),
        "bnew": _bias_table(rel_bias, tok[:, None] - key[None, :], head_of_row),
    }
    if with_cmp:
        nc = past // CMP_BLOCK
        c_end = np.arange(nc) * CMP_BLOCK + CMP_BLOCK - 1
        tabs["bias_c"] = _bias_table(rel_bias, past + tok[:, None] - c_end[None, :], head_of_row)
        win_len = min(WINDOW, past)
        wl = -(-(win_len + DEC_S) // LANES) * LANES
        tabs["bias_w"] = _bias_table(rel_bias, tok[:, None] + win_len - np.arange(wl)[None, :], head_of_row)
    return tabs


def kernel(x_prompt, x_sample, cache_sb_kv, cache_nsa_kv, cache_diff_kv, state_nsa_win, page_table, rel_bias,
           even_w_in, even_cmp_pe, even_cmp_wk, even_cmp_wv, even_w_out, odd_w_in, odd_lambda, odd_subln_g,
           odd_w_out, ln_mix_g, ln_mix_b, ln_ffn_g, ln_ffn_b, moe_w_router, moe_b_router, moe_w_up, moe_b_up,
           moe_w_down, moe_b_down):
    b, t, d = x_prompt.shape
    db, s, _ = x_sample.shape
    depth = ln_mix_g.shape[0]
    n_pool = cache_sb_kv.shape[0]
    past = page_table.shape[1] * PAGE
    assert s == DEC_S and d == D_MODEL and t % QB == 0 and (b * t) % MOE_TILE == 0
    assert past % (SEL_BLOCK * LANES) == 0 and state_nsa_win.shape[2] == WINDOW

    params = {
        "alpha": (2.0 * depth) ** 0.25,
        "ln_mix_g": ln_mix_g.astype(F32), "ln_mix_b": ln_mix_b.astype(F32),
        "ln_ffn_g": ln_ffn_g.astype(F32), "ln_ffn_b": ln_ffn_b.astype(F32),
        "w_router": moe_w_router, "b_router": moe_b_router,
        "experts": [(split_pairs(moe_w_up[l]), _split_pairs_bias(moe_b_up[l]),
                     moe_w_down[l].astype(BF16), moe_b_down[l][:, None, :].astype(F32)) for l in range(depth)],
    }
    c31 = rel_bias.astype(F32)[N_BUCKETS - 1]
    bias0 = _bias_table(rel_bias, np.arange(QB)[None, :] + KPAD - np.arange(KT)[:, None])
    bias_c = _cmp_band_table(rel_bias)
    nsa_rows = [(r, g, tk) for r in range(NSA_REP) for g in range(NSA_GROUPS) for tk in range(DEC_S)]
    nsa_tabs = _sample_tables(rel_bias, past, [g * NSA_REP + r for r, g, _ in nsa_rows],
                              [tk for _, _, tk in nsa_rows], True)
    diff_rows = [(g, r, tk) for g in range(DIFF_KV_HEADS) for r in range(DIFF_REP) for _ in range(2)
                 for tk in range(DEC_S)]
    diff_tabs = _sample_tables(rel_bias, past, [g * DIFF_REP + r for g, r, _ in diff_rows],
                               [tk for _, _, tk in diff_rows], False)

    cache_sb = cache_sb_kv.reshape(n_pool, -1, PAGE, 2 * SB_W)
    cache_nsa = cache_nsa_kv.reshape(n_pool, -1, PAGE, 4 * NSA_KV_W)
    cache_diff = cache_diff_kv.reshape(n_pool, -1, PAGE, 2 * DIFF_KV_W)

    xp = x_prompt.reshape(b * t, d).astype(F32)
    xs = x_sample.reshape(db * s, d).astype(F32)
    xpb, xsb = xp.astype(BF16), xs.astype(BF16)
    outs = {k: [] for k in ("sb_p", "sb_s", "nsa_p", "nsa_s", "win_p", "win_s", "diff_p", "diff_s")}
    for l in range(depth):
        j = l // 2
        if l % 2 == 0:
            w_in = jnp.pad(even_w_in[j], ((0, 0), (0, EVEN_IN_PAD - EVEN_IN))).astype(BF16)
            cmp_w = _compress_weights(even_cmp_pe[j], even_cmp_wk[j], even_cmp_wv[j])
            a1, a2, r_sb, r_nsa, r_win = _even_prompt(matmul(xpb, w_in, 512), b, t, cmp_w, bias_c, bias0, c31)
            s1, s2, s_sb, s_nsa, s_win = _even_sample(matmul(xsb, w_in, 512), db, page_table, cache_sb,
                                                      cache_nsa, state_nsa_win, j, cmp_w, nsa_tabs)
            w_out = even_w_out[j]
            for key, val in (("sb_p", r_sb), ("sb_s", s_sb), ("nsa_p", r_nsa), ("nsa_s", s_nsa),
                             ("win_p", r_win), ("win_s", s_win)):
                outs[key].append(val)
        else:
            lam_init = 0.8 - 0.6 * math.exp(-0.3 * l)
            w_in = odd_w_in[j].astype(BF16)
            lam_rows = odd_lambda[j].astype(F32)
            sub_g = odd_subln_g[j][None].astype(F32)
            ap, r_diff = _odd_prompt(matmul(xpb, w_in, 512), b, t, bias0, c31, lam_rows, sub_g, lam_init)
            as_, s_diff = _odd_sample(matmul(xsb, w_in, 512), db, page_table, cache_diff, j, diff_tabs,
                                      lam_rows, sub_g, lam_init)
            half = DIFF_Q_W // 2
            a1, a2, s1, s2 = ap[:, :half], ap[:, half:], as_[:, :half], as_[:, half:]
            w_out = odd_w_out[j]
            outs["diff_p"].append(r_diff)
            outs["diff_s"].append(s_diff)
        xp, xpb = _channel_mixer(l, a1, a2, w_out, xp, params, MOE_TILE)
        xs, xsb = _channel_mixer(l, s1, s2, w_out, xs, params, db * s)
    stack = lambda key: jnp.stack(outs[key], axis=1)
    return (xp.reshape(b, t, d), xs.reshape(db, s, d), stack("sb_p"), stack("sb_s"), stack("nsa_p"),
            stack("nsa_s"), stack("diff_p"), stack("diff_s"), stack("win_p"), stack("win_s"))
```

```python
import functools
import math

import numpy as np
import jax
import jax.numpy as jnp
from jax import lax
from jax.experimental import pallas as pl
from jax.experimental.pallas import tpu as pltpu

F32 = jnp.float32
BF16 = jnp.bfloat16

D_MODEL = 1024
HEAD_DIM = 64
SB_HEADS = 8
NSA_HEADS = 8
NSA_GROUPS = 2
NSA_REP = NSA_HEADS // NSA_GROUPS
CMP_BLOCK = 32
SEL_BLOCK = 64
N_SEL = 16
WINDOW = 512
DIFF_HEADS = 8
DIFF_KV_HEADS = 4
DIFF_REP = DIFF_HEADS // DIFF_KV_HEADS
N_BUCKETS = 32
MAX_DISTANCE = 128
N_EXPERTS = 32
TOP_K = 4
SWIGLU_ALPHA = 1.702
SWIGLU_LIMIT = 7.0
LN_EPS = 1e-5
PAGE = 128
NEG = -1e30
FORCE = 1e9
ATT_SCALE = HEAD_DIM ** -0.5

SB_W = SB_HEADS * HEAD_DIM
NSA_Q_W = NSA_HEADS * HEAD_DIM
NSA_KV_W = NSA_GROUPS * HEAD_DIM
NSA_GATE_W = NSA_HEADS * 3
EVEN_IN = 3 * SB_W + NSA_Q_W + 6 * NSA_KV_W + NSA_GATE_W
EVEN_IN_PAD = -(-EVEN_IN // 128) * 128
DIFF_Q_W = DIFF_HEADS * 2 * HEAD_DIM
DIFF_KV_W = DIFF_KV_HEADS * 2 * HEAD_DIM
ODD_IN = DIFF_Q_W + 2 * DIFF_KV_W

LANES = 128
SUBLANES = 8
VMEM_LIMIT = 52 * 1024 * 1024

QB = 128
KT = 512
KPAD = KT - QB
MOE_CH = 160
DEC_S = 8


def _dot(a, b):
    return jnp.dot(a, b, preferred_element_type=F32)


def _dot_nt(a, b):
    return lax.dot_general(a, b, (((1,), (1,)), ((), ())), preferred_element_type=F32)


def _dot_tn(a, b):
    return lax.dot_general(a, b, (((0,), (0,)), ((), ())), preferred_element_type=F32)


def _iota(shape, dim):
    return lax.broadcasted_iota(jnp.int32, shape, dim)


def _params(sem, vmem=VMEM_LIMIT):
    return pltpu.CompilerParams(dimension_semantics=sem, vmem_limit_bytes=vmem)


def _bucket_np(dist):
    n = np.maximum(dist, 0)
    exact = N_BUCKETS // 2
    nf = np.maximum(n, 1).astype(np.float32)
    large = exact + (np.log(nf / np.float32(exact)) / np.float32(math.log(MAX_DISTANCE / exact))
                     * np.float32(N_BUCKETS - exact)).astype(np.int32)
    return np.where(n < exact, n, np.minimum(large, N_BUCKETS - 1)).astype(np.int32)


def _bias_table(rel_bias, dist, head_of_row=None):
    bucket = _bucket_np(dist)
    rb = rel_bias.astype(F32)
    if head_of_row is None:
        return jnp.transpose(rb[bucket], (2, 0, 1))
    return rb[bucket, np.asarray(head_of_row)[:, None]]


def _mm_kernel(x_ref, w_ref, o_ref):
    o_ref[...] = _dot(x_ref[...], w_ref[...])


def matmul(x, w, tm):
    m, k = x.shape
    n = w.shape[1]
    tm = min(tm, m)
    return pl.pallas_call(
        _mm_kernel, grid=(m // tm,),
        in_specs=[pl.BlockSpec((tm, k), lambda i: (i, 0)), pl.BlockSpec((k, n), lambda i: (0, 0))],
        out_specs=pl.BlockSpec((tm, n), lambda i: (i, 0)),
        out_shape=jax.ShapeDtypeStruct((m, n), F32),
        compiler_params=_params(("arbitrary",)), name="in_proj")(x, w)


def _softmax_step(s, valid, m, l, acc, v, v_on_lanes=False):
    if valid is not None:
        s = jnp.where(valid, s, NEG)
    m_new = jnp.maximum(m, jnp.max(s, axis=-1, keepdims=True))
    p = jnp.exp(s - m_new)
    if valid is not None:
        p = jnp.where(valid, p, 0.0)
    alpha = jnp.exp(m - m_new)
    l = alpha * l + jnp.sum(p, axis=-1, keepdims=True)
    pv = _dot_nt(p.astype(BF16), v) if v_on_lanes else _dot(p.astype(BF16), v)
    return m_new, l, alpha * acc + pv


def _softmax_full(s, valid):
    if valid is not None:
        s = jnp.where(valid, s, NEG)
    m = jnp.max(s, axis=-1, keepdims=True)
    e = jnp.exp(s - m)
    if valid is not None:
        e = jnp.where(valid, e, 0.0)
    return e / jnp.maximum(jnp.sum(e, axis=-1, keepdims=True), 1e-30)


def _sb_step(z, valid, carry, acc, v, u, v_on_lanes=False):
    sp = jnp.maximum(z, 0.0) + jnp.log(1.0 + jnp.exp(-jnp.abs(z)))
    lk = -sp if valid is None else jnp.where(valid, -sp, 0.0)
    hi = lk.astype(BF16)
    lo = (lk - hi.astype(F32)).astype(BF16)
    later = _dot(hi, u) + _dot(lo, u) + carry
    w = jnp.exp(z - sp + later)
    if valid is not None:
        w = jnp.where(valid, w, 0.0)
    acc = acc + (_dot_nt(w.astype(BF16), v) if v_on_lanes else _dot(w.astype(BF16), v))
    carry = carry + jnp.sum(lk, axis=-1, keepdims=True)
    return carry, acc


def _softmax_step_t(s, valid, m, l, acc, vt):
    if valid is not None:
        s = jnp.where(valid, s, NEG)
    m_new = jnp.maximum(m, jnp.max(s, axis=0, keepdims=True))
    p = jnp.exp(s - m_new)
    if valid is not None:
        p = jnp.where(valid, p, 0.0)
    alpha = jnp.exp(m - m_new)
    l = alpha * l + jnp.sum(p, axis=0, keepdims=True)
    acc = alpha * acc + _dot(vt, p.astype(BF16))
    return m_new, l, acc


def _softmax_full_t(s, valid):
    s = jnp.where(valid, s, NEG)
    m = jnp.max(s, axis=0, keepdims=True)
    e = jnp.where(valid, jnp.exp(s - m), 0.0)
    return e / jnp.maximum(jnp.sum(e, axis=0, keepdims=True), 1e-30)


def _sb_step_t(z, valid, carry, acc, vt, lm):
    nq = z.shape[1]
    sp = jnp.maximum(z, 0.0) + jnp.log(1.0 + jnp.exp(-jnp.abs(z)))
    lk = jnp.where(valid, -sp, 0.0)
    hi = lk.astype(BF16)
    lo = (lk - hi.astype(F32)).astype(BF16)
    both = _dot(lm, jnp.concatenate([hi, lo], axis=1))
    later = both[:, :nq] + both[:, nq:] + carry
    w = jnp.where(valid, jnp.exp(z - sp + later), 0.0)
    acc = acc + _dot(vt, w.astype(BF16))
    carry = carry + jnp.sum(lk, axis=0, keepdims=True)
    return carry, acc


SB_DEAD = -104.0


def _later_matrix(n):
    j = np.arange(n)
    return jnp.asarray((j[:, None] > j[None, :]).astype(np.float32), dtype=BF16)


def _sb_prompt_kernel(q_ref, k_ref, vt_ref, lm_ref, o_ref):
    qi = pl.program_id(2)
    q0 = qi * QB
    q = (q_ref[0, 0].astype(F32) * ATT_SCALE).astype(BF16)
    lm = lm_ref[...]
    key = _iota((KT, QB), 0)
    qry = _iota((KT, QB), 1)

    def tile(j, carry, acc, first):
        start = pl.multiple_of(q0 - j * KT, QB)
        k = k_ref[0, 0, pl.ds(start, KT), :]
        vt = vt_ref[0, 0, :, pl.ds(start, KT)]
        z = _dot_nt(k, q)
        valid = key >= KPAD - start
        if first:
            valid = valid & (qry + KPAD - key > 0)
        return _sb_step_t(z, valid, carry, acc, vt, lm)

    carry, acc = tile(0, jnp.zeros((1, QB), F32), jnp.zeros((HEAD_DIM, QB), F32), True)
    n_tiles = qi // (KT // QB) + 1

    def more(state):
        j, carry, _ = state
        return (j < n_tiles) & (jnp.max(carry) > SB_DEAD)

    def step(state):
        j, carry, acc = state
        carry, acc = tile(j, carry, acc, False)
        return j + 1, carry, acc

    _, _, acc = lax.while_loop(more, step, (jnp.int32(1), carry, acc))
    o_ref[0, 0] = acc.T.astype(o_ref.dtype)


def sb_prompt(q, k, vt):
    b, h, t, _ = q.shape
    tp = k.shape[2]
    return pl.pallas_call(
        _sb_prompt_kernel, grid=(b, h, t // QB),
        in_specs=[pl.BlockSpec((1, 1, QB, HEAD_DIM), lambda b_, h_, i: (b_, h_, i, 0)),
                  pl.BlockSpec((1, 1, tp, HEAD_DIM), lambda b_, h_, i: (b_, h_, 0, 0)),
                  pl.BlockSpec((1, 1, HEAD_DIM, tp), lambda b_, h_, i: (b_, h_, 0, 0)),
                  pl.BlockSpec((KT, KT), lambda b_, h_, i: (0, 0))],
        out_specs=pl.BlockSpec((1, 1, QB, HEAD_DIM), lambda b_, h_, i: (b_, h_, i, 0)),
        out_shape=jax.ShapeDtypeStruct((b, h, t, HEAD_DIM), BF16),
        compiler_params=_params(("arbitrary",) * 3), name="sb_prompt")(q, k, vt, _later_matrix(KT).T)


def _diff_lambda(lam_ref, lam_init):
    lv = lam_ref[...].astype(F32)
    a = jnp.sum(lv[0:1] * lv[1:2], axis=-1, keepdims=True)
    b = jnp.sum(lv[2:3] * lv[3:4], axis=-1, keepdims=True)
    return jnp.exp(a) - jnp.exp(b) + lam_init


def _diff_finish(o1, l1, o2, l2, lam, sub_g, lam_init):
    a = o1 / jnp.maximum(l1, 1e-30) - lam * (o2 / jnp.maximum(l2, 1e-30))
    a = a * lax.rsqrt(jnp.mean(jnp.square(a), axis=-1, keepdims=True) + LN_EPS)
    return a * sub_g * (1.0 - lam_init)


def _diff_prompt_kernel(q_ref, k_ref, vt_ref, b0_ref, c31_ref, lam_ref, sg_ref, o_ref, *, lam_init):
    qi = pl.program_id(2)
    q0 = qi * QB
    w = DIFF_REP * QB
    key = _iota((KT, w), 0)
    qry = _iota((KT, w), 1) % QB
    qcat = [jnp.concatenate([(q_ref[0, 0, r, c].astype(F32) * ATT_SCALE).astype(BF16)
                             for r in range(DIFF_REP)], axis=0) for c in range(2)]

    def tile(j, state, first, padded):
        start = pl.multiple_of(q0 - j * KT, QB)
        vt = vt_ref[0, 0, :, pl.ds(start, KT)]
        valid = (key >= KPAD - start) if padded else None
        if first:
            valid = valid & (qry + KPAD - key >= 0)
        out = []
        for c in range(2):
            k = k_ref[0, 0, c, pl.ds(start, KT), :]
            s = _dot_nt(k, qcat[c]) + (b0_ref[0] if first else c31_ref[0])
            out.append(_softmax_step_t(s, valid, *state[c], vt))
        return tuple(out)

    init = tuple((jnp.full((1, w), NEG, F32), jnp.zeros((1, w), F32),
                  jnp.zeros((2 * HEAD_DIM, w), F32)) for _ in range(2))
    state = tile(0, init, True, True)
    last = qi // (KT // QB)
    state = lax.fori_loop(1, last, lambda j, st: tile(j, st, False, False), state)
    state = lax.cond(last >= 1, lambda st: tile(last, st, False, True), lambda st: st, state)
    lam = _diff_lambda(lam_ref, lam_init)
    (_, l1, o1), (_, l2, o2) = state
    a = o1 / jnp.maximum(l1, 1e-30) - lam * (o2 / jnp.maximum(l2, 1e-30))
    a = a * lax.rsqrt(jnp.mean(jnp.square(a), axis=0, keepdims=True) + LN_EPS)
    a = a * sg_ref[...] * (1.0 - lam_init)
    o_ref[0] = jnp.concatenate([a[:, r * QB:(r + 1) * QB].T for r in range(DIFF_REP)],
                               axis=-1).astype(o_ref.dtype)


def _heads_on_lanes(table, group):
    h, k, q = table.shape
    return table.reshape(h // group, group, k, q).transpose(0, 2, 1, 3).reshape(h // group, k, group * q)


def diff_prompt(q, k, vt, bias0_t, c31, lam_rows, sub_g_col, lam_init):
    b, gk, _, _, t, _ = q.shape
    tp = k.shape[3]
    w = DIFF_REP * QB
    b0 = _heads_on_lanes(bias0_t, DIFF_REP)
    c31 = _heads_on_lanes(jnp.broadcast_to(c31[:, None, None], (DIFF_HEADS, 1, QB)), DIFF_REP)
    return pl.pallas_call(
        functools.partial(_diff_prompt_kernel, lam_init=lam_init), grid=(b, gk, t // QB),
        in_specs=[pl.BlockSpec((1, 1, DIFF_REP, 2, QB, HEAD_DIM), lambda b_, g_, i: (b_, g_, 0, 0, i, 0)),
                  pl.BlockSpec((1, 1, 2, tp, HEAD_DIM), lambda b_, g_, i: (b_, g_, 0, 0, 0)),
                  pl.BlockSpec((1, 1, 2 * HEAD_DIM, tp), lambda b_, g_, i: (b_, g_, 0, 0)),
                  pl.BlockSpec((1, KT, w), lambda b_, g_, i: (g_, 0, 0)),
                  pl.BlockSpec((1, 1, w), lambda b_, g_, i: (g_, 0, 0)),
                  pl.BlockSpec((4, HEAD_DIM), lambda b_, g_, i: (0, 0)),
                  pl.BlockSpec((2 * HEAD_DIM, 1), lambda b_, g_, i: (0, 0))],
        out_specs=pl.BlockSpec((1, QB, DIFF_REP * 2 * HEAD_DIM), lambda b_, g_, i: (b_, i, g_)),
        out_shape=jax.ShapeDtypeStruct((b, t, DIFF_HEADS * 2 * HEAD_DIM), BF16),
        compiler_params=_params(("arbitrary",) * 3), name="diff_prompt")(
            q, k, vt, b0, c31, lam_rows, sub_g_col)


CMP_BAND_BACK = 4
CMP_BAND = 16


def _cmp_band_table(rel_bias):
    m = np.arange(CMP_BAND) - CMP_BAND_BACK
    dist = np.arange(QB)[None, :] - (m[:, None] * CMP_BLOCK + CMP_BLOCK - 1)
    delta = _bias_table(rel_bias, dist) - rel_bias.astype(F32)[N_BUCKETS - 1][:, None, None]
    hi = delta.astype(BF16)
    lo = (delta - hi.astype(F32)).astype(BF16)
    return jnp.stack([hi, lo], axis=1)


def _select_blocks(score_ref, n_rows, n_keep):
    sc = score_ref[...]
    blk = _iota(sc.shape, 0)

    def body(i, rank):
        r = score_ref[pl.ds(i, 1), :]
        better = (r > sc) | ((r == sc) & (i < blk))
        return rank + jnp.where(better, 1.0, 0.0)

    rank = lax.fori_loop(0, n_rows, body, jnp.zeros(sc.shape, F32), unroll=8)
    return rank < n_keep


def _nsa_prompt_kernel(q_ref, kc_ref, vct_ref, ks_ref, vst_ref, kw_ref, vwt_ref, bc_ref, b0_ref, c31_ref,
                       gate_ref, o_ref, impt_ref, score_ref, *, n_sel_blocks, n_keep):
    qi = pl.program_id(2)
    q0 = qi * QB
    n_tiles = qi // (KT // QB) + 1
    nc = kc_ref.shape[2]
    ns = nc // 2
    heads = range(NSA_REP)
    w = NSA_REP * QB
    qcat = jnp.concatenate([(q_ref[0, 0, r].astype(F32) * ATT_SCALE).astype(BF16) for r in heads], axis=0)
    on_lanes = lambda rows: jnp.concatenate(rows, axis=1)

    valid_c = _iota((nc, w), 1) % QB + q0 - (_iota((nc, w), 0) * CMP_BLOCK + (CMP_BLOCK - 1)) >= 0
    first_blk = qi * (QB // CMP_BLOCK) - CMP_BAND_BACK
    place = jnp.where(_iota((nc, CMP_BAND), 0) == _iota((nc, CMP_BAND), 1) + first_blk, 1.0, 0.0).astype(BF16)
    bias = c31_ref[0] + _dot(place, bc_ref[0, 0]) + _dot(place, bc_ref[0, 1])
    p = _softmax_full_t(_dot_nt(kc_ref[0, 0], qcat) + bias, valid_c)
    o_c = _dot(vct_ref[0, 0], p.astype(BF16))
    imp = p[:, :QB]
    for r in range(1, NSA_REP):
        imp = imp + p[:, r * QB:(r + 1) * QB]
    impt_ref[...] = imp
    imp_s = impt_ref[pl.ds(0, ns, stride=2), :] + impt_ref[pl.ds(1, ns, stride=2), :]
    blk = _iota((ns, QB), 0)
    qpos = _iota((ns, QB), 1) + q0
    forced = (blk == qpos // SEL_BLOCK) | (blk == 0)
    future = blk * SEL_BLOCK > qpos
    score = jnp.where(forced, FORCE, jnp.where(future, -FORCE, imp_s))
    score_ref[...] = jnp.where(blk < n_sel_blocks, score, -3e38)
    sel_t = jnp.where(_select_blocks(score_ref, ns, n_keep), 1.0, 0.0).astype(BF16)

    key = _iota((KT, w), 0)
    qry = _iota((KT, w), 1) % QB
    causal0 = qry + KPAD - key >= 0
    key_blk = _iota((KT, ns), 0) // SEL_BLOCK
    blk_e = _iota((KT, ns), 1)

    def branch(k_ref, vt_ref, hi, valid_fn):
        def tile(j, state, first):
            start = pl.multiple_of(q0 - j * KT, QB)
            k = k_ref[0, 0, pl.ds(start, KT), :]
            vt = vt_ref[0, 0, :, pl.ds(start, KT)]
            s = _dot_nt(k, qcat) + (b0_ref[0] if first else c31_ref[0])
            return _softmax_step_t(s, valid_fn(j, start, first), *state, vt)

        init = (jnp.full((1, w), NEG, F32), jnp.zeros((1, w), F32), jnp.zeros((HEAD_DIM, w), F32))
        state = tile(0, init, True)
        _, l, acc = lax.fori_loop(1, hi, lambda j, st: tile(j, st, False), state)
        return acc / jnp.maximum(l, 1e-30)

    def valid_sel(j, start, first):
        base = start // SEL_BLOCK - KPAD // SEL_BLOCK
        expand = jnp.where(blk_e == key_blk + base, 1.0, 0.0).astype(BF16)
        chosen = on_lanes([_dot(expand, sel_t)] * NSA_REP) > 0.5
        return (chosen & causal0) if first else chosen

    def valid_win(j, start, first):
        ok = key >= KPAD - start
        if first:
            return ok & causal0
        return ok & (qry + KPAD - key + j * KT < WINDOW)

    o_s = branch(ks_ref, vst_ref, n_tiles, valid_sel)
    o_w = branch(kw_ref, vwt_ref, jnp.minimum(n_tiles, (WINDOW + QB - 1) // KT + 1), valid_win)

    gate = jax.nn.sigmoid(gate_ref[0, 0].astype(F32))
    g0, g1, g2 = (on_lanes([gate[3 * r + i:3 * r + i + 1, :] for r in heads]) for i in range(3))
    o = g0 * o_c + g1 * o_s + g2 * o_w
    o_ref[0] = jnp.concatenate([o[:, r * QB:(r + 1) * QB] for r in heads], axis=0).T.astype(o_ref.dtype)


def nsa_prompt(q, kc, vct, ks, vst, kw, vwt, bias_c, bias0_t, c31, gates_t, n_sel_blocks):
    b, g, _, t, _ = q.shape
    tp = ks.shape[2]
    nc = kc.shape[2]
    k_spec = pl.BlockSpec((1, 1, tp, HEAD_DIM), lambda b_, g_, i: (b_, g_, 0, 0))
    vt_spec = pl.BlockSpec((1, 1, HEAD_DIM, tp), lambda b_, g_, i: (b_, g_, 0, 0))
    kern = functools.partial(_nsa_prompt_kernel, n_sel_blocks=n_sel_blocks,
                             n_keep=min(N_SEL, n_sel_blocks))
    w = NSA_REP * QB
    b0 = _heads_on_lanes(bias0_t, NSA_REP)
    c31 = _heads_on_lanes(jnp.broadcast_to(c31[:, None, None], (NSA_HEADS, 1, QB)), NSA_REP)
    band = _heads_on_lanes(bias_c.reshape(NSA_HEADS, 2 * CMP_BAND, QB), NSA_REP).reshape(g, 2, CMP_BAND, w)
    return pl.pallas_call(
        kern, grid=(b, g, t // QB),
        in_specs=[pl.BlockSpec((1, 1, NSA_REP, QB, HEAD_DIM), lambda b_, g_, i: (b_, g_, 0, i, 0)),
                  pl.BlockSpec((1, 1, nc, HEAD_DIM), lambda b_, g_, i: (b_, g_, 0, 0)),
                  pl.BlockSpec((1, 1, HEAD_DIM, nc), lambda b_, g_, i: (b_, g_, 0, 0)),
                  k_spec, vt_spec, k_spec, vt_spec,
                  pl.BlockSpec((1, 2, CMP_BAND, w), lambda b_, g_, i: (g_, 0, 0, 0)),
                  pl.BlockSpec((1, KT, w), lambda b_, g_, i: (g_, 0, 0)),
                  pl.BlockSpec((1, 1, w), lambda b_, g_, i: (g_, 0, 0)),
                  pl.BlockSpec((1, 1, 4 * NSA_REP, QB), lambda b_, g_, i: (b_, g_, 0, i))],
        out_specs=pl.BlockSpec((1, QB, NSA_REP * HEAD_DIM), lambda b_, g_, i: (b_, i, g_)),
        out_shape=jax.ShapeDtypeStruct((b, t, NSA_HEADS * HEAD_DIM), BF16),
        scratch_shapes=[pltpu.VMEM((nc, QB), F32), pltpu.VMEM((nc // 2, QB), F32)],
        compiler_params=_params(("arbitrary",) * 3), name="nsa_prompt")(
            q, kc, vct, ks, vst, kw, vwt, band, b0, c31, gates_t)


CMP_PP = 16
CMP_HALF = CMP_BLOCK // 2


def _compress_kernel(pt_ref, w_ref, pe_ref, *rest, pp, positions_on_lanes):
    pages, o_ref, lo_ref, hi_ref = rest[:pp], rest[pp], rest[pp + 1], rest[pp + 2]
    if positions_on_lanes:
        stage_ref = rest[pp + 3]
        for p in range(pp):
            stage_ref[p] = pages[p][...].T
        pages = [stage_ref.at[p] for p in range(pp)]
    width = 2 * HEAD_DIM
    acc = jnp.zeros((pp * SUBLANES, 2 * width), F32)
    for i in range(CMP_HALF):
        rows = [pages[p][pl.ds(i, SUBLANES, stride=CMP_HALF), :] + pe_ref[i] for p in range(pp)]
        acc = acc + _dot(jnp.concatenate(rows, axis=0).astype(BF16), w_ref[0, i])
    lo_ref[...] = acc[:, :width]
    hi_ref[...] = acc[:, width:]
    n = pp * SUBLANES // 2
    o_ref[0] = lo_ref[pl.ds(0, n, stride=2), :] + hi_ref[pl.ds(1, n, stride=2), :]


def compress_pages(page_table, src, layer, block, w_cat, pe_tiles, positions_on_lanes=False):
    nseq, n_pages = page_table.shape
    pp = min(CMP_PP, n_pages)
    width = 2 * HEAD_DIM

    def page_spec(p):
        def index(b, s, kv, pt):
            where = (block + kv, 0) if positions_on_lanes else (0, block + kv)
            return (pt[b, s * pp + p], layer) + where
        return pl.BlockSpec((None, None, PAGE, width), index)

    scratch = [pltpu.VMEM((pp * SUBLANES, width), F32), pltpu.VMEM((pp * SUBLANES, width), F32)]
    if positions_on_lanes:
        scratch.append(pltpu.VMEM((pp, PAGE, width), F32))
    grid_spec = pltpu.PrefetchScalarGridSpec(
        num_scalar_prefetch=1, grid=(nseq, n_pages // pp, 2),
        in_specs=[pl.BlockSpec((1, CMP_HALF, width, 2 * width), lambda b, s, kv, pt: (kv, 0, 0, 0)),
                  pl.BlockSpec((CMP_HALF, SUBLANES, width), lambda b, s, kv, pt: (0, 0, 0))]
        + [page_spec(p) for p in range(pp)],
        out_specs=pl.BlockSpec((1, pp * 4, width), lambda b, s, kv, pt: (b, s, kv)),
        scratch_shapes=scratch)
    return pl.pallas_call(
        functools.partial(_compress_kernel, pp=pp, positions_on_lanes=positions_on_lanes), grid_spec=grid_spec,
        out_shape=jax.ShapeDtypeStruct((nseq, n_pages * 4, 2 * width), F32),
        compiler_params=_params(("arbitrary",) * 3), name="compress")(
            page_table, w_cat, pe_tiles, *([src] * pp))


def _compress_weights(pe, wk, wv):
    def cat(w):
        w = w.reshape(CMP_BLOCK, HEAD_DIM, HEAD_DIM)
        z = jnp.zeros_like(w)
        full = jnp.concatenate([jnp.concatenate([w, z], -1), jnp.concatenate([z, w], -1)], axis=1)
        return jnp.concatenate([full[:CMP_HALF], full[CMP_HALF:]], axis=-1)

    w_cat = jnp.stack([cat(wk), cat(wv)]).astype(BF16)
    pe2 = jnp.tile(pe.astype(F32), (1, 2))
    pe_tiles = jnp.stack([pe2[:CMP_HALF], pe2[CMP_HALF:]], axis=1)
    pe_tiles = jnp.tile(pe_tiles, (1, SUBLANES // 2, 1))
    return w_cat, pe_tiles


SB_PP = 8
DIFF_PP = 8
SEL_PP = 16


def _page_specs(pp, n_pages, rows, row_block, layer):
    def spec(p):
        def index(b, s, pt):
            return (pt[b, n_pages - 1 - (s * pp + p)], layer, row_block, 0)
        return pl.BlockSpec((None, None, rows, LANES), index)
    return [spec(p) for p in range(pp)]


def _positions_on_lanes(cache):
    pool, layers = cache.shape[:2]
    return jnp.transpose(cache, (0, 1, 3, 4, 5, 2)).reshape(pool, layers, -1, PAGE)


def _token_of_row(shape):
    return _iota(shape, 0) % DEC_S


def _sb_sample_kernel(pt_ref, q_ref, new_ref, u_ref, *rest, pp):
    pages, o_ref, carry_ref, acc_ref = rest[:pp], rest[pp], rest[pp + 1], rest[pp + 2]
    s = pl.program_id(1)
    q = q_ref[0]
    u = u_ref[...]
    rows = SB_HEADS * DEC_S

    @pl.when(s == 0)
    def _():
        k = new_ref[0, :, :SB_W]
        v = new_ref[0, :, SB_W:]
        valid = _iota((rows, PAGE), 1) < _token_of_row((rows, PAGE))
        carry, acc = _sb_step(_dot_nt(q, k), valid, jnp.zeros((rows, 1), F32),
                              jnp.zeros((rows, SB_W), F32), v, u)
        carry_ref[...] = carry
        acc_ref[...] = acc

    for p in range(pp):
        @pl.when(jnp.max(carry_ref[...]) > SB_DEAD)
        def _(p=p):
            kt = pages[p][:SB_W, :].astype(BF16)
            vt = pages[p][SB_W:, :].astype(BF16)
            carry, acc = _sb_step(_dot(q, kt), None, carry_ref[...], acc_ref[...], vt, u, v_on_lanes=True)
            carry_ref[...] = carry
            acc_ref[...] = acc

    @pl.when(s == pl.num_programs(1) - 1)
    def _():
        a = acc_ref[...]
        o_ref[0] = jnp.concatenate(
            [a[h * DEC_S:(h + 1) * DEC_S, h * HEAD_DIM:(h + 1) * HEAD_DIM] for h in range(SB_HEADS)],
            axis=-1).astype(o_ref.dtype)


def sb_sample(page_table, qbd, new_kv, cache, layer):
    db, n_pages = page_table.shape
    pp = min(SB_PP, n_pages)
    rows = SB_HEADS * DEC_S
    grid_spec = pltpu.PrefetchScalarGridSpec(
        num_scalar_prefetch=1, grid=(db, n_pages // pp),
        in_specs=[pl.BlockSpec((1, rows, SB_W), lambda b, s, pt: (b, 0, 0)),
                  pl.BlockSpec((1, PAGE, 2 * SB_W), lambda b, s, pt: (b, 0, 0)),
                  pl.BlockSpec((PAGE, PAGE), lambda b, s, pt: (0, 0))]
        + _page_specs(pp, n_pages, 2 * SB_W, 0, layer),
        out_specs=pl.BlockSpec((1, DEC_S, SB_W), lambda b, s, pt: (b, 0, 0)),
        scratch_shapes=[pltpu.VMEM((rows, 1), F32), pltpu.VMEM((rows, SB_W), F32)])
    return pl.pallas_call(
        functools.partial(_sb_sample_kernel, pp=pp), grid_spec=grid_spec,
        out_shape=jax.ShapeDtypeStruct((db, DEC_S, SB_W), BF16),
        compiler_params=_params(("arbitrary",) * 2), name="sb_sample")(
            page_table, qbd, new_kv, _later_matrix(PAGE), *([cache] * pp))


DIFF_GROUP_ROWS = DIFF_REP * 2 * DEC_S
DIFF_ROW_STRIDE = 2 * DIFF_KV_HEADS


def _diff_sample_tiles(q, loads, bias, valid, m, l, acc):
    heads = range(DIFF_KV_HEADS)
    rows = lambda g: slice(g * DIFF_GROUP_ROWS, (g + 1) * DIFF_GROUP_ROWS)
    s = jnp.concatenate(
        [jnp.concatenate([_dot_nt(q[rows(g)], load(g).astype(BF16)) for g in heads], axis=0)
         for load in loads], axis=1) + bias
    if valid is not None:
        s = jnp.where(valid, s, NEG)
    m_new = jnp.maximum(m, jnp.max(s, axis=-1, keepdims=True))
    p = jnp.exp(s - m_new)
    if valid is not None:
        p = jnp.where(valid, p, 0.0)
    alpha = jnp.exp(m - m_new)
    l = alpha * l + jnp.sum(p, axis=-1, keepdims=True)
    p = p.astype(BF16)
    pv = []
    for g in heads:
        terms = [_dot(p[rows(g), i * PAGE:(i + 1) * PAGE], load(DIFF_KV_HEADS + g).astype(BF16))
                 for i, load in enumerate(loads)]
        pv.append(functools.reduce(lambda a, b: a + b, terms))
    return m_new, l, alpha * acc + jnp.concatenate(pv, axis=0)


def _diff_sample_kernel(pt_ref, q_ref, new_ref, cvec_ref, blast_ref, bnew_ref, lam_ref, sg_ref, *rest,
                        pp, lam_init):
    pages, o_ref, m_ref, l_ref, acc_ref = rest[:pp], rest[pp], rest[pp + 1], rest[pp + 2], rest[pp + 3]
    s = pl.program_id(1)
    q = q_ref[0]
    rows = DIFF_HEADS * 2 * DEC_S
    strided = lambda ref: (lambda off: ref[pl.ds(off, PAGE, stride=DIFF_ROW_STRIDE), :])

    @pl.when(s == 0)
    def _():
        valid = _iota((rows, PAGE), 1) <= _token_of_row((rows, PAGE))
        m, l, acc = _diff_sample_tiles(q, [strided(new_ref)], bnew_ref[...], valid,
                                       jnp.full((rows, 1), NEG, F32), jnp.zeros((rows, 1), F32),
                                       jnp.zeros((rows, 2 * HEAD_DIM), F32))
        m_ref[...] = m
        l_ref[...] = l
        acc_ref[...] = acc

    bias = jnp.broadcast_to(cvec_ref[...], (rows, PAGE))
    bias = jnp.concatenate([jnp.where(s == 0, blast_ref[...], bias)] + [bias] * (pp - 1), axis=1)
    m, l, acc = _diff_sample_tiles(q, [strided(ref) for ref in pages], bias, None,
                                   m_ref[...], l_ref[...], acc_ref[...])
    m_ref[...] = m
    l_ref[...] = l
    acc_ref[...] = acc

    @pl.when(s == pl.num_programs(1) - 1)
    def _():
        a = acc_ref[...]
        ls = l_ref[...]
        lam = _diff_lambda(lam_ref, lam_init)
        outs = []
        for head in range(DIFF_HEADS):
            r0 = head * 2 * DEC_S
            outs.append(_diff_finish(a[r0:r0 + DEC_S], ls[r0:r0 + DEC_S], a[r0 + DEC_S:r0 + 2 * DEC_S],
                                     ls[r0 + DEC_S:r0 + 2 * DEC_S], lam, sg_ref[...], lam_init))
        o_ref[0] = jnp.concatenate(outs, axis=-1).astype(o_ref.dtype)


def diff_sample(page_table, qc, new_kv, cvec, blast, bnew, lam_rows, sub_g, cache, layer, lam_init):
    db, n_pages = page_table.shape
    pp = min(DIFF_PP, n_pages)
    rows = DIFF_HEADS * 2 * DEC_S
    page_rows = PAGE * DIFF_ROW_STRIDE
    const = lambda shape: pl.BlockSpec(shape, lambda b, s, pt: (0,) * len(shape))
    grid_spec = pltpu.PrefetchScalarGridSpec(
        num_scalar_prefetch=1, grid=(db, n_pages // pp),
        in_specs=[pl.BlockSpec((1, rows, 2 * HEAD_DIM), lambda b, s, pt: (b, 0, 0)),
                  pl.BlockSpec((None, page_rows, LANES), lambda b, s, pt: (b, 0, 0)),
                  const((rows, 1)), const((rows, PAGE)), const((rows, PAGE)),
                  const((4, HEAD_DIM)), const((1, 2 * HEAD_DIM))]
        + _page_specs(pp, n_pages, page_rows, 0, layer),
        out_specs=pl.BlockSpec((1, DEC_S, DIFF_HEADS * 2 * HEAD_DIM), lambda b, s, pt: (b, 0, 0)),
        scratch_shapes=[pltpu.VMEM((rows, 1), F32), pltpu.VMEM((rows, 1), F32),
                        pltpu.VMEM((rows, 2 * HEAD_DIM), F32)])
    return pl.pallas_call(
        functools.partial(_diff_sample_kernel, pp=pp, lam_init=lam_init), grid_spec=grid_spec,
        out_shape=jax.ShapeDtypeStruct((db, DEC_S, DIFF_HEADS * 2 * HEAD_DIM), BF16),
        compiler_params=_params(("arbitrary",) * 2), name="diff_sample")(
            page_table, qc, new_kv, cvec, blast, bnew, lam_rows, sub_g, *([cache] * pp))


NSA_ROWS = NSA_HEADS * DEC_S
NSA_GT = NSA_GROUPS * DEC_S


def _nsa_diag(a):
    pieces = []
    for r in range(NSA_REP):
        for g in range(NSA_GROUPS):
            r0 = r * NSA_GT + g * DEC_S
            pieces.append(a[r0:r0 + DEC_S, g * HEAD_DIM:(g + 1) * HEAD_DIM])
    return jnp.concatenate(pieces, axis=0)


def _nsa_sample_cw_kernel(q_ref, kcvc_ref, bc_ref, win_ref, bw_ref, oc_ref, ow_ref, sel_ref,
                          impt_ref, score_ref, *, win_len, n_keep):
    q = q_ref[0]
    nc = kcvc_ref.shape[1]
    ns = nc // 2
    kc = kcvc_ref[0, :, :NSA_KV_W].astype(BF16)
    vc = kcvc_ref[0, :, NSA_KV_W:].astype(BF16)
    p = _softmax_full(_dot_nt(q, kc) + bc_ref[...], None)
    oc_ref[0] = _nsa_diag(_dot(p.astype(BF16), vc))
    imp = p[0:NSA_GT]
    for r in range(1, NSA_REP):
        imp = imp + p[r * NSA_GT:(r + 1) * NSA_GT]
    imp = jnp.concatenate([imp, jnp.zeros((LANES - NSA_GT, nc), F32)], axis=0)
    impt_ref[...] = imp.T
    imp_s = impt_ref[pl.ds(0, ns, stride=2), :] + impt_ref[pl.ds(1, ns, stride=2), :]
    blk = _iota((ns, LANES), 0)
    score_ref[...] = jnp.where(blk == 0, FORCE, imp_s)
    sel_t = jnp.where(_select_blocks(score_ref, ns, n_keep), 1.0, 0.0)
    sel_ref[0] = sel_t.T[:NSA_GT].astype(sel_ref.dtype)

    wl = win_ref.shape[1]
    kw = win_ref[0, :, :NSA_KV_W]
    vw = win_ref[0, :, NSA_KV_W:]
    tok = _token_of_row((NSA_ROWS, wl))
    col = _iota((NSA_ROWS, wl), 1)
    valid = (col > tok + (win_len - WINDOW)) & (col <= tok + win_len) & (col < win_len + DEC_S)
    pw = _softmax_full(_dot_nt(q, kw) + bw_ref[...], valid)
    ow_ref[0] = _nsa_diag(_dot(pw.astype(BF16), vw))


def nsa_sample_cw(qbd, kcvc, bias_c, win_all, bias_w, win_len, n_keep):
    db = qbd.shape[0]
    nc = kcvc.shape[1]
    wl = win_all.shape[1]
    const = lambda shape: pl.BlockSpec(shape, lambda b: (0,) * len(shape))
    return pl.pallas_call(
        functools.partial(_nsa_sample_cw_kernel, win_len=win_len, n_keep=n_keep), grid=(db,),
        in_specs=[pl.BlockSpec((1, NSA_ROWS, NSA_KV_W), lambda b: (b, 0, 0)),
                  pl.BlockSpec((1, nc, 2 * NSA_KV_W), lambda b: (b, 0, 0)),
                  const((NSA_ROWS, nc)),
                  pl.BlockSpec((1, wl, 2 * NSA_KV_W), lambda b: (b, 0, 0)),
                  const((NSA_ROWS, wl))],
        out_specs=[pl.BlockSpec((1, NSA_ROWS, HEAD_DIM), lambda b: (b, 0, 0)),
                   pl.BlockSpec((1, NSA_ROWS, HEAD_DIM), lambda b: (b, 0, 0)),
                   pl.BlockSpec((1, NSA_GT, nc // 2), lambda b: (b, 0, 0))],
        out_shape=[jax.ShapeDtypeStruct((db, NSA_ROWS, HEAD_DIM), F32),
                   jax.ShapeDtypeStruct((db, NSA_ROWS, HEAD_DIM), F32),
                   jax.ShapeDtypeStruct((db, NSA_GT, nc // 2), BF16)],
        scratch_shapes=[pltpu.VMEM((nc, LANES), F32), pltpu.VMEM((nc // 2, LANES), F32)],
        compiler_params=_params(("arbitrary",)), name="nsa_sample_cw")(qbd, kcvc, bias_c, win_all, bias_w)


def _nsa_sample_sel_kernel(pt_ref, q_ref, new_ref, sel_ref, cvec_ref, blast_ref, bnew_ref, oc_ref, ow_ref,
                           gate_ref, *rest, pp, n_pages):
    pages, o_ref, m_ref, l_ref, acc_ref = rest[:pp], rest[pp], rest[pp + 1], rest[pp + 2], rest[pp + 3]
    s = pl.program_id(1)
    q = q_ref[0]
    sel = sel_ref[0]
    ns = sel.shape[1]

    @pl.when(s == 0)
    def _():
        k = new_ref[0, :, :NSA_KV_W]
        v = new_ref[0, :, NSA_KV_W:]
        valid = _iota((NSA_ROWS, PAGE), 1) <= _token_of_row((NSA_ROWS, PAGE))
        m, l, acc = _softmax_step(_dot_nt(q, k) + bnew_ref[...], valid, jnp.full((NSA_ROWS, 1), NEG, F32),
                                  jnp.zeros((NSA_ROWS, 1), F32), jnp.zeros((NSA_ROWS, NSA_KV_W), F32), v)
        m_ref[...] = m
        l_ref[...] = l
        acc_ref[...] = acc

    m, l, acc = m_ref[...], l_ref[...], acc_ref[...]
    blk_e = _iota((ns, PAGE), 0)
    col_e = _iota((ns, PAGE), 1) // SEL_BLOCK
    for p in range(pp):
        logical = n_pages - 1 - (s * pp + p)
        kt = pages[p][:NSA_KV_W, :].astype(BF16)
        vt = pages[p][NSA_KV_W:, :].astype(BF16)
        expand = jnp.where(blk_e == col_e + logical * (PAGE // SEL_BLOCK), 1.0, 0.0).astype(BF16)
        chosen = _dot(sel, expand) > 0.5
        valid = jnp.concatenate([chosen] * NSA_REP, axis=0)
        bias = cvec_ref[...]
        if p == 0:
            bias = jnp.where(s == 0, blast_ref[...], bias)
        m, l, acc = _softmax_step(_dot(q, kt) + bias, valid, m, l, acc, vt, v_on_lanes=True)
    m_ref[...] = m
    l_ref[...] = l
    acc_ref[...] = acc

    @pl.when(s == pl.num_programs(1) - 1)
    def _():
        o_s = _nsa_diag(acc_ref[...] / jnp.maximum(l_ref[...], 1e-30))
        gate = jax.nn.sigmoid(gate_ref[0].astype(F32))
        o = gate[:, 0:1] * oc_ref[0] + gate[:, 1:2] * o_s + gate[:, 2:3] * ow_ref[0]
        pieces = []
        for g in range(NSA_GROUPS):
            for r in range(NSA_REP):
                r0 = r * NSA_GT + g * DEC_S
                pieces.append(o[r0:r0 + DEC_S])
        o_ref[0] = jnp.concatenate(pieces, axis=-1).astype(o_ref.dtype)


def nsa_sample_sel(page_table, qbd, new_kv, sel, cvec, blast, bnew, o_c, o_w, gates, cache, layer):
    db, n_pages = page_table.shape
    pp = min(SEL_PP, n_pages)
    ns = sel.shape[2]
    const = lambda shape: pl.BlockSpec(shape, lambda b, s, pt: (0,) * len(shape))
    per_seq = lambda shape: pl.BlockSpec((1,) + shape, lambda b, s, pt: (b,) + (0,) * len(shape))
    grid_spec = pltpu.PrefetchScalarGridSpec(
        num_scalar_prefetch=1, grid=(db, n_pages // pp),
        in_specs=[per_seq((NSA_ROWS, NSA_KV_W)), per_seq((PAGE, 2 * NSA_KV_W)), per_seq((NSA_GT, ns)),
                  const((NSA_ROWS, 1)), const((NSA_ROWS, PAGE)), const((NSA_ROWS, PAGE)),
                  per_seq((NSA_ROWS, HEAD_DIM)), per_seq((NSA_ROWS, HEAD_DIM)), per_seq((NSA_ROWS, LANES))]
        + _page_specs(pp, n_pages, 2 * NSA_KV_W, 1, layer),
        out_specs=pl.BlockSpec((1, DEC_S, NSA_HEADS * HEAD_DIM), lambda b, s, pt: (b, 0, 0)),
        scratch_shapes=[pltpu.VMEM((NSA_ROWS, 1), F32), pltpu.VMEM((NSA_ROWS, 1), F32),
                        pltpu.VMEM((NSA_ROWS, NSA_KV_W), F32)])
    return pl.pallas_call(
        functools.partial(_nsa_sample_sel_kernel, pp=pp, n_pages=n_pages), grid_spec=grid_spec,
        out_shape=jax.ShapeDtypeStruct((db, DEC_S, NSA_HEADS * HEAD_DIM), BF16),
        compiler_params=_params(("arbitrary",) * 2), name="nsa_sample_sel")(
            page_table, qbd, new_kv, sel, cvec, blast, bnew, o_c, o_w, gates, *([cache] * pp))


def _layer_norm(y, g, b):
    mu = jnp.mean(y, axis=-1, keepdims=True)
    d = y - mu
    var = jnp.mean(d * d, axis=-1, keepdims=True)
    return d * lax.rsqrt(var + LN_EPS) * g + b


def _split_bf16(x):
    hi = x.astype(BF16)
    return hi, (x - hi.astype(F32)).astype(BF16)


def _mix_router_kernel(a1_ref, a2_ref, w1_ref, w2_ref, x_ref, g_ref, b_ref, wr_ref, br_ref, u_ref,
                       xn_ref, xb_ref, rt_ref, gt_ref, cnt_ref, *, alpha):
    mix = _dot(a1_ref[...], w1_ref[...]) + _dot(a2_ref[...], w2_ref[...])
    xn = _layer_norm(alpha * x_ref[...] + mix, g_ref[...], b_ref[...])
    xn_ref[...] = xn
    xb_ref[...] = xn.astype(BF16)
    xh, xl = _split_bf16(xn)
    wh, wl = _split_bf16(wr_ref[...])
    logits = _dot_nt(wh, xh) + _dot_nt(wh, xl) + _dot_nt(wl, xh) + br_ref[...]
    e_iota = _iota(logits.shape, 0)
    work = logits
    vals, sels = [], []
    for _ in range(TOP_K):
        top = jnp.max(work, axis=0, keepdims=True)
        idx = jnp.min(jnp.where(work == top, e_iota, N_EXPERTS), axis=0, keepdims=True)
        sel = e_iota == idx
        vals.append(top)
        sels.append(sel)
        work = jnp.where(sel, -jnp.inf, work)
    ex = [jnp.exp(v - vals[0]) for v in vals]
    den = ex[0] + ex[1] + ex[2] + ex[3]
    gate = jnp.zeros(logits.shape, F32)
    chosen = sels[0]
    for k in range(TOP_K):
        gate = jnp.where(sels[k], ex[k] / den, gate)
        chosen = chosen | sels[k]
    cf = jnp.where(chosen, 1.0, 0.0)
    rank = _dot(cf.astype(BF16), u_ref[...])
    rt_ref[0] = jnp.where(chosen, rank, -1.0)
    gt_ref[0] = gate
    cnt = jnp.sum(cf, axis=1, keepdims=True)
    cnt_ref[0] = jnp.broadcast_to(cnt, (N_EXPERTS, LANES)).astype(jnp.int32)


def mix_router(a1, a2, w1, w2, x, ln_g, ln_b, w_router_t, b_router, tt, alpha):
    n = x.shape[0]
    half = a1.shape[1]
    j = np.arange(tt)
    before = jnp.asarray((j[:, None] < j[None, :]).astype(np.float32), dtype=BF16)
    tile = lambda w: pl.BlockSpec((tt, w), lambda i: (i, 0))
    const = lambda shape: pl.BlockSpec(shape, lambda i: (0,) * len(shape))
    route = pl.BlockSpec((1, N_EXPERTS, tt), lambda i: (i, 0, 0))
    return pl.pallas_call(
        functools.partial(_mix_router_kernel, alpha=alpha), grid=(n // tt,),
        in_specs=[tile(half), tile(half), const((half, D_MODEL)), const((half, D_MODEL)), tile(D_MODEL),
                  const((1, D_MODEL)), const((1, D_MODEL)), const((N_EXPERTS, D_MODEL)),
                  const((N_EXPERTS, 1)), const((tt, tt))],
        out_specs=[tile(D_MODEL), tile(D_MODEL), route, route,
                   pl.BlockSpec((1, N_EXPERTS, LANES), lambda i: (i, 0, 0))],
        out_shape=[jax.ShapeDtypeStruct((n, D_MODEL), F32), jax.ShapeDtypeStruct((n, D_MODEL), BF16),
                   jax.ShapeDtypeStruct((n // tt, N_EXPERTS, tt), F32),
                   jax.ShapeDtypeStruct((n // tt, N_EXPERTS, tt), F32),
                   jax.ShapeDtypeStruct((n // tt, N_EXPERTS, LANES), jnp.int32)],
        compiler_params=_params(("arbitrary",)), name="mix_router")(
            a1, a2, w1, w2, x, ln_g, ln_b, w_router_t, b_router, before)


def _pair_permutation():
    c = np.arange(2 * LANES)
    dest = np.where(c % 2 == 0, c // 2, LANES + c // 2)
    return jnp.asarray((dest[:, None] == c[None, :]).astype(np.float32), dtype=BF16)


def _split_pairs_kernel(w_ref, s_ref, o_ref):
    s = s_ref[...]
    for blk in range(w_ref.shape[2] // (2 * LANES)):
        cols = slice(blk * 2 * LANES, (blk + 1) * 2 * LANES)
        o_ref[0, :, cols] = _dot(w_ref[0, :, cols].astype(BF16), s).astype(BF16)


def split_pairs(w_up):
    e, d, f2 = w_up.shape
    return pl.pallas_call(
        _split_pairs_kernel, grid=(e,),
        in_specs=[pl.BlockSpec((1, d, f2), lambda i: (i, 0, 0)),
                  pl.BlockSpec((2 * LANES, 2 * LANES), lambda i: (0, 0))],
        out_specs=pl.BlockSpec((1, d, f2), lambda i: (i, 0, 0)),
        out_shape=jax.ShapeDtypeStruct((e, d, f2), BF16),
        compiler_params=_params(("arbitrary",)), name="split_pairs")(w_up, _pair_permutation())


def _moe_kernel(cnt_ref, xb_ref, xn_ref, rt_ref, gt_ref, wu_ref, bu_ref, wd_ref, bd_ref,
                lg_ref, lb_ref, y_ref, yb_ref, acc_ref, *, alpha):
    i = pl.program_id(0)
    e = pl.program_id(1)
    tt = xb_ref.shape[0]

    @pl.when(e == 0)
    def _():
        acc_ref[...] = jnp.zeros_like(acc_ref)

    n_chunks = (cnt_ref[i, e] + MOE_CH - 1) // MOE_CH
    slot = rt_ref[0, pl.ds(e, 1), :]
    gate = gt_ref[0, pl.ds(e, 1), :]

    def chunk(c, _):
        want = (_iota((MOE_CH, tt), 0) + c * MOE_CH).astype(F32)
        hit = slot == want
        xe = _dot(jnp.where(hit, 1.0, 0.0).astype(BF16), xb_ref[...]).astype(BF16)
        h = _dot(xe, wu_ref[0]) + bu_ref[0]
        acts = []
        for blk in range(h.shape[1] // (2 * LANES)):
            h_glu = jnp.minimum(h[:, blk * 2 * LANES:blk * 2 * LANES + LANES], SWIGLU_LIMIT)
            h_lin = jnp.clip(h[:, blk * 2 * LANES + LANES:(blk + 1) * 2 * LANES], -SWIGLU_LIMIT, SWIGLU_LIMIT)
            acts.append((h_glu * jax.nn.sigmoid(SWIGLU_ALPHA * h_glu) * (h_lin + 1.0)).astype(BF16))
        y = _dot(jnp.concatenate(acts, axis=-1), wd_ref[0]) + bd_ref[0]
        back = jnp.where(hit, gate, 0.0).astype(BF16)
        acc_ref[...] += _dot_tn(back, y.astype(BF16))
        return 0

    lax.fori_loop(0, n_chunks, chunk, 0)

    @pl.when(e == pl.num_programs(1) - 1)
    def _():
        y = _layer_norm(alpha * xn_ref[...] + acc_ref[...], lg_ref[...], lb_ref[...])
        y_ref[...] = y
        yb_ref[...] = y.astype(BF16)


def moe_ln(cnt, xb, xn, rt, gt, w_up, b_up, w_down, b_down, ln_g, ln_b, tt, alpha):
    n = xn.shape[0]
    ff = w_down.shape[1]
    tile = pl.BlockSpec((tt, D_MODEL), lambda i, e, c: (i, 0))
    route = pl.BlockSpec((1, N_EXPERTS, tt), lambda i, e, c: (i, 0, 0))
    per_e = lambda shape: pl.BlockSpec((1,) + shape, lambda i, e, c: (e, 0, 0))
    const = pl.BlockSpec((1, D_MODEL), lambda i, e, c: (0, 0))
    grid_spec = pltpu.PrefetchScalarGridSpec(
        num_scalar_prefetch=1, grid=(n // tt, N_EXPERTS),
        in_specs=[tile, tile, route, route, per_e((D_MODEL, 2 * ff)), per_e((1, 2 * ff)),
                  per_e((ff, D_MODEL)), per_e((1, D_MODEL)), const, const],
        out_specs=[tile, tile],
        scratch_shapes=[pltpu.VMEM((tt, D_MODEL), F32)])
    return pl.pallas_call(
        functools.partial(_moe_kernel, alpha=alpha), grid_spec=grid_spec,
        out_shape=[jax.ShapeDtypeStruct((n, D_MODEL), F32), jax.ShapeDtypeStruct((n, D_MODEL), BF16)],
        compiler_params=_params(("arbitrary",) * 2), name="moe_ln")(
            cnt, xb, xn, rt, gt, w_up, b_up, w_down, b_down, ln_g, ln_b)


MOE_TILE = 1024


def _split_pairs_bias(b_up):
    e, f2 = b_up.shape
    b = b_up.astype(F32).reshape(e, f2 // (2 * LANES), LANES, 2).transpose(0, 1, 3, 2)
    return b.reshape(e, 1, f2)


def _front_pad(a, axis):
    pad = [(0, 0)] * a.ndim
    pad[axis] = (KPAD, 0)
    return jnp.pad(a, pad)


def _pad_rows(a, rows):
    return jnp.pad(a, ((0, 0), (0, rows - a.shape[1]), (0, 0)))


def _channel_mixer(layer, a1, a2, w_out, x, p, tt):
    alpha = p["alpha"]
    half = a1.shape[1]
    w_out = w_out.astype(BF16)
    xn, xb, rt, gt, cnt = mix_router(
        a1, a2, w_out[:half], w_out[half:], x, p["ln_mix_g"][layer][None], p["ln_mix_b"][layer][None],
        p["w_router"][layer].T.astype(F32), p["b_router"][layer][:, None].astype(F32), tt, alpha)
    return moe_ln(cnt[:, :, 0], xb, xn, rt, gt, *p["experts"][layer],
                  p["ln_ffn_g"][layer][None], p["ln_ffn_b"][layer][None], tt, alpha)


def _even_prompt(h, b, t, cmp_w, bias_c, bias0, c31):
    h3 = h.reshape(b, t, -1)
    heads = lambda a, nh: a.reshape(b, t, nh, HEAD_DIM).transpose(0, 2, 1, 3).astype(BF16)
    heads_t = lambda a, nh: a.reshape(b, t, nh, HEAD_DIM).transpose(0, 2, 3, 1).astype(BF16)
    o = 3 * SB_W
    o_sb = sb_prompt(heads(h3[..., :SB_W], SB_HEADS),
                     _front_pad(heads(h3[..., SB_W:2 * SB_W], SB_HEADS), 2),
                     _front_pad(heads_t(h3[..., 2 * SB_W:o], SB_HEADS), 3))
    o_sb = o_sb.transpose(0, 2, 1, 3).reshape(b * t, SB_W)
    nq = h3[..., o:o + NSA_Q_W].reshape(b, t, NSA_GROUPS, NSA_REP, HEAD_DIM).transpose(0, 2, 3, 1, 4)
    o += NSA_Q_W
    nkv = h3[..., o:o + 6 * NSA_KV_W].reshape(b, t, 6, NSA_GROUPS, HEAD_DIM)
    gates = h3[..., o + 6 * NSA_KV_W:o + 6 * NSA_KV_W + NSA_GATE_W]
    gates = gates.reshape(b, t, NSA_GROUPS, NSA_REP * 3).transpose(0, 2, 3, 1)
    gates = jnp.pad(gates, ((0, 0), (0, 0), (0, NSA_REP), (0, 0)))
    n_pages = t // PAGE
    table = jnp.arange(b * n_pages, dtype=jnp.int32).reshape(b, n_pages)
    kcvc = compress_pages(table, h.reshape(b * n_pages, 1, PAGE, h.shape[-1]), 0, o // (2 * HEAD_DIM), *cmp_w)
    nc = t // CMP_BLOCK
    ncp = -(-nc // LANES) * LANES
    kcvc = kcvc.reshape(b, nc, 2, NSA_GROUPS, HEAD_DIM).astype(BF16)
    kc = jnp.pad(kcvc[:, :, 0].transpose(0, 2, 1, 3), ((0, 0), (0, 0), (0, ncp - nc), (0, 0)))
    vct = jnp.pad(kcvc[:, :, 1].transpose(0, 2, 3, 1), ((0, 0), (0, 0), (0, 0), (0, ncp - nc)))
    grp = lambda i: _front_pad(nkv[:, :, i].transpose(0, 2, 1, 3).astype(BF16), 2)
    grp_t = lambda i: _front_pad(nkv[:, :, i].transpose(0, 2, 3, 1).astype(BF16), 3)
    o_nsa = nsa_prompt(nq.astype(BF16), kc, vct, grp(2), grp_t(3), grp(4), grp_t(5),
                       bias_c, bias0, c31, gates, -(-t // SEL_BLOCK))
    sb_rows = h3[..., SB_W:3 * SB_W].reshape(b, t, 2, SB_HEADS, HEAD_DIM)
    keep = min(WINDOW, t)
    return (o_sb, o_nsa.reshape(b * t, NSA_Q_W), sb_rows, nkv[:, :, :4], nkv[:, t - keep:, 4:])


def _even_sample(h, db, page_table, cache_sb, cache_nsa, win_state, layer, cmp_w, tabs):
    h3 = h.reshape(db, DEC_S, -1)
    past = page_table.shape[1] * PAGE
    o = 3 * SB_W
    q = h3[..., :SB_W].reshape(db, DEC_S, SB_HEADS, HEAD_DIM).transpose(0, 2, 1, 3) * ATT_SCALE
    eye = jnp.eye(SB_HEADS, dtype=F32)
    qbd = (q[:, :, :, None, :] * eye[None, :, None, :, None]).reshape(db, SB_HEADS * DEC_S, SB_W)
    o_sb = sb_sample(page_table, qbd.astype(BF16), _pad_rows(h3[..., SB_W:o], PAGE).astype(BF16),
                     cache_sb, layer)
    q = h3[..., o:o + NSA_Q_W].reshape(db, DEC_S, NSA_GROUPS, NSA_REP, HEAD_DIM).transpose(0, 3, 2, 1, 4)
    eye = jnp.eye(NSA_GROUPS, dtype=F32)
    qbd = (q[:, :, :, :, None, :] * ATT_SCALE * eye[None, None, :, None, :, None])
    qbd = qbd.reshape(db, NSA_ROWS, NSA_KV_W).astype(BF16)
    o += NSA_Q_W
    nkv = h3[..., o:o + 6 * NSA_KV_W]
    gates = h3[..., o + 6 * NSA_KV_W:o + 6 * NSA_KV_W + NSA_GATE_W]
    gates = gates.reshape(db, DEC_S, NSA_GROUPS, NSA_REP, 3).transpose(0, 3, 2, 1, 4).reshape(db, NSA_ROWS, 3)
    gates = jnp.pad(gates, ((0, 0), (0, 0), (0, LANES - 3)))
    kcvc = compress_pages(page_table, cache_nsa, layer, 0, *cmp_w, positions_on_lanes=True)
    win_len = win_state.shape[2]
    new_win = nkv[..., 4 * NSA_KV_W:]
    win_all = jnp.concatenate([win_state[:, layer].reshape(db, win_len, 2 * NSA_KV_W), new_win], axis=1)
    wl = tabs["bias_w"].shape[1]
    n_blocks = -(-(past + DEC_S) // SEL_BLOCK)
    o_c, o_w, sel = nsa_sample_cw(qbd, kcvc, tabs["bias_c"], _pad_rows(win_all, wl).astype(BF16),
                                  tabs["bias_w"], win_len, min(N_SEL, n_blocks) - 1)
    o_nsa = nsa_sample_sel(page_table, qbd, _pad_rows(nkv[..., 2 * NSA_KV_W:4 * NSA_KV_W], PAGE).astype(BF16),
                           sel, tabs["cvec"], tabs["blast"], tabs["bnew"], o_c, o_w, gates, cache_nsa, layer)
    sb_rows = h3[..., SB_W:3 * SB_W].reshape(db, DEC_S, 2, SB_HEADS, HEAD_DIM)
    nsa_rows = nkv[..., :4 * NSA_KV_W].reshape(db, DEC_S, 4, NSA_GROUPS, HEAD_DIM)
    win_rows = win_all[:, max(0, win_len + DEC_S - WINDOW):].reshape(db, -1, 2, NSA_GROUPS, HEAD_DIM)
    return (o_sb.reshape(db * DEC_S, SB_W), o_nsa.reshape(db * DEC_S, NSA_Q_W), sb_rows, nsa_rows, win_rows)


def _odd_prompt(h, b, t, bias0, c31, lam_rows, sub_g, lam_init):
    h3 = h.reshape(b, t, -1)
    q = h3[..., :DIFF_Q_W].reshape(b, t, DIFF_KV_HEADS, DIFF_REP, 2, HEAD_DIM).transpose(0, 2, 3, 4, 1, 5)
    kv = h3[..., DIFF_Q_W:].reshape(b, t, 2, DIFF_KV_HEADS, 2 * HEAD_DIM)
    k = kv[:, :, 0].reshape(b, t, DIFF_KV_HEADS, 2, HEAD_DIM).transpose(0, 2, 3, 1, 4)
    vt = kv[:, :, 1].transpose(0, 2, 3, 1)
    o = diff_prompt(q.astype(BF16), _front_pad(k.astype(BF16), 3), _front_pad(vt.astype(BF16), 3),
                    bias0, c31, lam_rows, sub_g.T, lam_init)
    return o.reshape(b * t, DIFF_Q_W), kv


def _odd_sample(h, db, page_table, cache, layer, tabs, lam_rows, sub_g, lam_init):
    h3 = h.reshape(db, DEC_S, -1)
    q = h3[..., :DIFF_Q_W].reshape(db, DEC_S, DIFF_KV_HEADS, DIFF_REP, 2, HEAD_DIM).transpose(0, 2, 3, 4, 1, 5)
    eye_c = jnp.eye(2, dtype=F32)
    qc = q[..., None, :] * ATT_SCALE * eye_c[None, None, None, :, None, :, None]
    qc = qc.reshape(db, DIFF_HEADS * 2 * DEC_S, 2 * HEAD_DIM).astype(BF16)
    kv = h3[..., DIFF_Q_W:].reshape(db, DEC_S, 2, DIFF_KV_HEADS, 2 * HEAD_DIM)
    new_kv = _pad_rows(kv.reshape(db, DEC_S * DIFF_ROW_STRIDE, 2 * HEAD_DIM), PAGE * DIFF_ROW_STRIDE)
    o = diff_sample(page_table, qc, new_kv, tabs["cvec"], tabs["blast"], tabs["bnew"], lam_rows, sub_g,
                    cache, layer, lam_init)
    return o.reshape(db * DEC_S, DIFF_Q_W), kv


def _sample_tables(rel_bias, past, head_of_row, tok_of_row, with_cmp):
    tok = np.asarray(tok_of_row)
    key = np.arange(PAGE)
    tabs = {
        "cvec": rel_bias.astype(F32)[N_BUCKETS - 1][np.asarray(head_of_row)][:, None],
        "blast": _bias_table(rel_bias, tok[:, None] + PAGE - key[None, :], head_of_row),
        "bnew": _bias_table(rel_bias, tok[:, None] - key[None, :], head_of_row),
    }
    if with_cmp:
        nc = past // CMP_BLOCK
        c_end = np.arange(nc) * CMP_BLOCK + CMP_BLOCK - 1
        tabs["bias_c"] = _bias_table(rel_bias, past + tok[:, None] - c_end[None, :], head_of_row)
        win_len = min(WINDOW, past)
        wl = -(-(win_len + DEC_S) // LANES) * LANES
        tabs["bias_w"] = _bias_table(rel_bias, tok[:, None] + win_len - np.arange(wl)[None, :], head_of_row)
    return tabs


def kernel(x_prompt, x_sample, cache_sb_kv, cache_nsa_kv, cache_diff_kv, state_nsa_win, page_table, rel_bias,
           even_w_in, even_cmp_pe, even_cmp_wk, even_cmp_wv, even_w_out, odd_w_in, odd_lambda, odd_subln_g,
           odd_w_out, ln_mix_g, ln_mix_b, ln_ffn_g, ln_ffn_b, moe_w_router, moe_b_router, moe_w_up, moe_b_up,
           moe_w_down, moe_b_down):
    b, t, d = x_prompt.shape
    db, s, _ = x_sample.shape
    depth = ln_mix_g.shape[0]
    n_pool = cache_sb_kv.shape[0]
    past = page_table.shape[1] * PAGE
    assert s == DEC_S and d == D_MODEL and t % QB == 0 and (b * t) % MOE_TILE == 0
    assert past % (SEL_BLOCK * LANES) == 0 and state_nsa_win.shape[2] == WINDOW

    params = {
        "alpha": (2.0 * depth) ** 0.25,
        "ln_mix_g": ln_mix_g.astype(F32), "ln_mix_b": ln_mix_b.astype(F32),
        "ln_ffn_g": ln_ffn_g.astype(F32), "ln_ffn_b": ln_ffn_b.astype(F32),
        "w_router": moe_w_router, "b_router": moe_b_router,
        "experts": [(split_pairs(moe_w_up[l]), _split_pairs_bias(moe_b_up[l]),
                     moe_w_down[l].astype(BF16), moe_b_down[l][:, None, :].astype(F32)) for l in range(depth)],
    }
    c31 = rel_bias.astype(F32)[N_BUCKETS - 1]
    bias0 = _bias_table(rel_bias, np.arange(QB)[None, :] + KPAD - np.arange(KT)[:, None])
    bias_c = _cmp_band_table(rel_bias)
    nsa_rows = [(r, g, tk) for r in range(NSA_REP) for g in range(NSA_GROUPS) for tk in range(DEC_S)]
    nsa_tabs = _sample_tables(rel_bias, past, [g * NSA_REP + r for r, g, _ in nsa_rows],
                              [tk for _, _, tk in nsa_rows], True)
    diff_rows = [(g, r, tk) for g in range(DIFF_KV_HEADS) for r in range(DIFF_REP) for _ in range(2)
                 for tk in range(DEC_S)]
    diff_tabs = _sample_tables(rel_bias, past, [g * DIFF_REP + r for g, r, _ in diff_rows],
                               [tk for _, _, tk in diff_rows], False)

    cache_sb = _positions_on_lanes(cache_sb_kv)
    cache_nsa = _positions_on_lanes(cache_nsa_kv)
    cache_diff = cache_diff_kv.reshape(n_pool, -1, PAGE * DIFF_ROW_STRIDE, 2 * HEAD_DIM)

    xp = x_prompt.reshape(b * t, d).astype(F32)
    xs = x_sample.reshape(db * s, d).astype(F32)
    xpb, xsb = xp.astype(BF16), xs.astype(BF16)
    outs = {k: [] for k in ("sb_p", "sb_s", "nsa_p", "nsa_s", "win_p", "win_s", "diff_p", "diff_s")}
    for l in range(depth):
        j = l // 2
        if l % 2 == 0:
            w_in = jnp.pad(even_w_in[j], ((0, 0), (0, EVEN_IN_PAD - EVEN_IN))).astype(BF16)
            cmp_w = _compress_weights(even_cmp_pe[j], even_cmp_wk[j], even_cmp_wv[j])
            a1, a2, r_sb, r_nsa, r_win = _even_prompt(matmul(xpb, w_in, 512), b, t, cmp_w, bias_c, bias0, c31)
            s1, s2, s_sb, s_nsa, s_win = _even_sample(matmul(xsb, w_in, 512), db, page_table, cache_sb,
                                                      cache_nsa, state_nsa_win, j, cmp_w, nsa_tabs)
            w_out = even_w_out[j]
            for key, val in (("sb_p", r_sb), ("sb_s", s_sb), ("nsa_p", r_nsa), ("nsa_s", s_nsa),
                             ("win_p", r_win), ("win_s", s_win)):
                outs[key].append(val)
        else:
            lam_init = 0.8 - 0.6 * math.exp(-0.3 * l)
            w_in = odd_w_in[j].astype(BF16)
            lam_rows = odd_lambda[j].astype(F32)
            sub_g = odd_subln_g[j][None].astype(F32)
            ap, r_diff = _odd_prompt(matmul(xpb, w_in, 512), b, t, bias0, c31, lam_rows, sub_g, lam_init)
            as_, s_diff = _odd_sample(matmul(xsb, w_in, 512), db, page_table, cache_diff, j, diff_tabs,
                                      lam_rows, sub_g, lam_init)
            half = DIFF_Q_W // 2
            a1, a2, s1, s2 = ap[:, :half], ap[:, half:], as_[:, :half], as_[:, half:]
            w_out = odd_w_out[j]
            outs["diff_p"].append(r_diff)
            outs["diff_s"].append(s_diff)
        xp, xpb = _channel_mixer(l, a1, a2, w_out, xp, params, MOE_TILE)
        xs, xsb = _channel_mixer(l, s1, s2, w_out, xs, params, db * s)
    stack = lambda key: jnp.stack(outs[key], axis=1)
    return (xp.reshape(b, t, d), xs.reshape(db, s, d), stack("sb_p"), stack("sb_s"), stack("nsa_p"),
            stack("nsa_s"), stack("diff_p"), stack("diff_s"), stack("win_p"), stack("win_s"))
```

```python
import functools
import math

import numpy as np
import jax
import jax.numpy as jnp
from jax import lax
from jax.experimental import pallas as pl
from jax.experimental.pallas import tpu as pltpu

F32 = jnp.float32
BF16 = jnp.bfloat16

D_MODEL = 1024
HEAD_DIM = 64
SB_HEADS = 8
NSA_HEADS = 8
NSA_GROUPS = 2
NSA_REP = NSA_HEADS // NSA_GROUPS
CMP_BLOCK = 32
SEL_BLOCK = 64
N_SEL = 16
WINDOW = 512
DIFF_HEADS = 8
DIFF_KV_HEADS = 4
DIFF_REP = DIFF_HEADS // DIFF_KV_HEADS
N_BUCKETS = 32
MAX_DISTANCE = 128
N_EXPERTS = 32
TOP_K = 4
SWIGLU_ALPHA = 1.702
SWIGLU_LIMIT = 7.0
LN_EPS = 1e-5
PAGE = 128
NEG = -1e30
FORCE = 1e9
ATT_SCALE = HEAD_DIM ** -0.5

SB_W = SB_HEADS * HEAD_DIM
NSA_Q_W = NSA_HEADS * HEAD_DIM
NSA_KV_W = NSA_GROUPS * HEAD_DIM
NSA_GATE_W = NSA_HEADS * 3
EVEN_IN = 3 * SB_W + NSA_Q_W + 6 * NSA_KV_W + NSA_GATE_W
EVEN_IN_PAD = -(-EVEN_IN // 128) * 128
DIFF_Q_W = DIFF_HEADS * 2 * HEAD_DIM
DIFF_KV_W = DIFF_KV_HEADS * 2 * HEAD_DIM
ODD_IN = DIFF_Q_W + 2 * DIFF_KV_W

LANES = 128
SUBLANES = 8
VMEM_LIMIT = 52 * 1024 * 1024

QB = 128
KT = 512
KPAD = KT - QB
DIFF_KT = 1024
MOE_CH = 160
DEC_S = 8


def _dot(a, b):
    return jnp.dot(a, b, preferred_element_type=F32)


def _dot_nt(a, b):
    return lax.dot_general(a, b, (((1,), (1,)), ((), ())), preferred_element_type=F32)


def _dot_tn(a, b):
    return lax.dot_general(a, b, (((0,), (0,)), ((), ())), preferred_element_type=F32)


def _iota(shape, dim):
    return lax.broadcasted_iota(jnp.int32, shape, dim)


def _params(sem, vmem=VMEM_LIMIT):
    return pltpu.CompilerParams(dimension_semantics=sem, vmem_limit_bytes=vmem)


def _bucket_np(dist):
    n = np.maximum(dist, 0)
    exact = N_BUCKETS // 2
    nf = np.maximum(n, 1).astype(np.float32)
    large = exact + (np.log(nf / np.float32(exact)) / np.float32(math.log(MAX_DISTANCE / exact))
                     * np.float32(N_BUCKETS - exact)).astype(np.int32)
    return np.where(n < exact, n, np.minimum(large, N_BUCKETS - 1)).astype(np.int32)


DIST_TABLE = 256


def _distance_table(rel_bias):
    onehot = np.eye(N_BUCKETS, dtype=np.float32)[_bucket_np(np.arange(DIST_TABLE))]
    return jnp.dot(jnp.asarray(onehot), rel_bias.astype(F32), precision=lax.Precision.HIGHEST).T


def _window(tab, start, n, step=1):
    stop = start + step * (n - 1)
    lo, hi = max(0, -start), max(0, stop + 1 - tab.shape[1])
    ext = jnp.pad(tab, ((0, 0), (lo, hi)), mode="edge")
    return ext[:, start + lo:stop + lo + 1:step]


def _hankel(v, rows, cols):
    n = rows + cols - 1
    flat = jnp.tile(v, (1, rows + 1))[:, :rows * (n + 1)]
    return flat.reshape(v.shape[0], rows, n + 1)[:, :, :cols]


def _mm_kernel(x_ref, w_ref, o_ref):
    o_ref[...] = _dot(x_ref[...], w_ref[...])


def matmul(x, w, tm):
    m, k = x.shape
    n = w.shape[1]
    tm = min(tm, m)
    return pl.pallas_call(
        _mm_kernel, grid=(m // tm,),
        in_specs=[pl.BlockSpec((tm, k), lambda i: (i, 0)), pl.BlockSpec((k, n), lambda i: (0, 0))],
        out_specs=pl.BlockSpec((tm, n), lambda i: (i, 0)),
        out_shape=jax.ShapeDtypeStruct((m, n), F32),
        compiler_params=_params(("arbitrary",)), name="in_proj")(x, w)


def _softmax_step(s, valid, m, l, acc, v, v_on_lanes=False):
    if valid is not None:
        s = jnp.where(valid, s, NEG)
    m_new = jnp.maximum(m, jnp.max(s, axis=-1, keepdims=True))
    p = jnp.exp(s - m_new)
    if valid is not None:
        p = jnp.where(valid, p, 0.0)
    alpha = jnp.exp(m - m_new)
    l = alpha * l + jnp.sum(p, axis=-1, keepdims=True)
    pv = _dot_nt(p.astype(BF16), v) if v_on_lanes else _dot(p.astype(BF16), v)
    return m_new, l, alpha * acc + pv


def _softmax_full(s, valid):
    if valid is not None:
        s = jnp.where(valid, s, NEG)
    m = jnp.max(s, axis=-1, keepdims=True)
    e = jnp.exp(s - m)
    if valid is not None:
        e = jnp.where(valid, e, 0.0)
    return e / jnp.maximum(jnp.sum(e, axis=-1, keepdims=True), 1e-30)


def _sb_step(z, valid, carry, acc, v, u, v_on_lanes=False):
    sp = jnp.maximum(z, 0.0) + jnp.log(1.0 + jnp.exp(-jnp.abs(z)))
    lk = -sp if valid is None else jnp.where(valid, -sp, 0.0)
    hi = lk.astype(BF16)
    lo = (lk - hi.astype(F32)).astype(BF16)
    later = _dot(hi, u) + _dot(lo, u) + carry
    w = jnp.exp(z - sp + later)
    if valid is not None:
        w = jnp.where(valid, w, 0.0)
    acc = acc + (_dot_nt(w.astype(BF16), v) if v_on_lanes else _dot(w.astype(BF16), v))
    carry = carry + jnp.sum(lk, axis=-1, keepdims=True)
    return carry, acc


def _softmax_step_t(s, valid, m, l, acc, vt):
    if valid is not None:
        s = jnp.where(valid, s, NEG)
    m_new = jnp.maximum(m, jnp.max(s, axis=0, keepdims=True))
    p = jnp.exp(s - m_new)
    if valid is not None:
        p = jnp.where(valid, p, 0.0)
    alpha = jnp.exp(m - m_new)
    l = alpha * l + jnp.sum(p, axis=0, keepdims=True)
    acc = alpha * acc + _dot(vt, p.astype(BF16))
    return m_new, l, acc


def _softmax_full_t(s, valid):
    s = jnp.where(valid, s, NEG)
    m = jnp.max(s, axis=0, keepdims=True)
    e = jnp.where(valid, jnp.exp(s - m), 0.0)
    return e / jnp.maximum(jnp.sum(e, axis=0, keepdims=True), 1e-30)


def _sb_step_t(z, valid, carry, acc, vt, lm):
    nq = z.shape[1]
    sp = jnp.maximum(z, 0.0) + jnp.log(1.0 + jnp.exp(-jnp.abs(z)))
    lk = jnp.where(valid, -sp, 0.0)
    hi = lk.astype(BF16)
    lo = (lk - hi.astype(F32)).astype(BF16)
    both = _dot(lm, jnp.concatenate([hi, lo], axis=1))
    later = both[:, :nq] + both[:, nq:] + carry
    w = jnp.where(valid, jnp.exp(z - sp + later), 0.0)
    acc = acc + _dot(vt, w.astype(BF16))
    carry = carry + jnp.sum(lk, axis=0, keepdims=True)
    return carry, acc


SB_DEAD = -104.0


def _later_matrix(n):
    j = np.arange(n)
    return jnp.asarray((j[:, None] > j[None, :]).astype(np.float32), dtype=BF16)


def _sb_prompt_kernel(q_ref, k_ref, vt_ref, lm_ref, o_ref):
    qi = pl.program_id(2)
    q0 = qi * QB
    q = (q_ref[0, 0].astype(F32) * ATT_SCALE).astype(BF16)
    lm = lm_ref[...]
    key = _iota((KT, QB), 0)
    qry = _iota((KT, QB), 1)

    def tile(j, carry, acc, first):
        start = pl.multiple_of(q0 - j * KT, QB)
        k = k_ref[0, 0, pl.ds(start, KT), :]
        vt = vt_ref[0, 0, :, pl.ds(start, KT)]
        z = _dot_nt(k, q)
        valid = key >= KPAD - start
        if first:
            valid = valid & (qry + KPAD - key > 0)
        return _sb_step_t(z, valid, carry, acc, vt, lm)

    carry, acc = tile(0, jnp.zeros((1, QB), F32), jnp.zeros((HEAD_DIM, QB), F32), True)
    n_tiles = qi // (KT // QB) + 1

    def more(state):
        j, carry, _ = state
        return (j < n_tiles) & (jnp.max(carry) > SB_DEAD)

    def step(state):
        j, carry, acc = state
        carry, acc = tile(j, carry, acc, False)
        return j + 1, carry, acc

    _, _, acc = lax.while_loop(more, step, (jnp.int32(1), carry, acc))
    o_ref[0, 0] = acc.T.astype(o_ref.dtype)


def sb_prompt(q, k, vt):
    b, h, t, _ = q.shape
    tp = k.shape[2]
    return pl.pallas_call(
        _sb_prompt_kernel, grid=(b, h, t // QB),
        in_specs=[pl.BlockSpec((1, 1, QB, HEAD_DIM), lambda b_, h_, i: (b_, h_, i, 0)),
                  pl.BlockSpec((1, 1, tp, HEAD_DIM), lambda b_, h_, i: (b_, h_, 0, 0)),
                  pl.BlockSpec((1, 1, HEAD_DIM, tp), lambda b_, h_, i: (b_, h_, 0, 0)),
                  pl.BlockSpec((KT, KT), lambda b_, h_, i: (0, 0))],
        out_specs=pl.BlockSpec((1, 1, QB, HEAD_DIM), lambda b_, h_, i: (b_, h_, i, 0)),
        out_shape=jax.ShapeDtypeStruct((b, h, t, HEAD_DIM), BF16),
        compiler_params=_params(("arbitrary",) * 3), name="sb_prompt")(q, k, vt, _later_matrix(KT).T)


def _diff_lambda(lam_ref, lam_init):
    lv = lam_ref[...].astype(F32)
    a = jnp.sum(lv[0:1] * lv[1:2], axis=-1, keepdims=True)
    b = jnp.sum(lv[2:3] * lv[3:4], axis=-1, keepdims=True)
    return jnp.exp(a) - jnp.exp(b) + lam_init


def _diff_finish(o1, l1, o2, l2, lam, sub_g, lam_init):
    a = o1 / jnp.maximum(l1, 1e-30) - lam * (o2 / jnp.maximum(l2, 1e-30))
    a = a * lax.rsqrt(jnp.mean(jnp.square(a), axis=-1, keepdims=True) + LN_EPS)
    return a * sub_g * (1.0 - lam_init)


def _diff_prompt_kernel(q_ref, k_ref, vt_ref, b0_ref, c31_ref, lam_ref, sg_ref, o_ref, *, lam_init):
    qi = pl.program_id(2)
    q0 = qi * QB
    w = DIFF_REP * QB
    kt = b0_ref.shape[1]
    kpad = kt - QB
    key = _iota((kt, w), 0)
    qry = _iota((kt, w), 1) % QB
    qcat = [jnp.concatenate([(q_ref[0, 0, r, c].astype(F32) * ATT_SCALE).astype(BF16)
                             for r in range(DIFF_REP)], axis=0) for c in range(2)]

    def tile(j, state, first, padded):
        start = pl.multiple_of(q0 - j * kt, QB)
        vt = vt_ref[0, 0, :, pl.ds(start, kt)]
        valid = (key >= kpad - start) if padded else None
        if first:
            valid = valid & (qry + kpad - key >= 0)
        out = []
        for c in range(2):
            k = k_ref[0, 0, c, pl.ds(start, kt), :]
            s = _dot_nt(k, qcat[c]) + (b0_ref[0] if first else c31_ref[0])
            out.append(_softmax_step_t(s, valid, *state[c], vt))
        return tuple(out)

    init = tuple((jnp.full((1, w), NEG, F32), jnp.zeros((1, w), F32),
                  jnp.zeros((2 * HEAD_DIM, w), F32)) for _ in range(2))
    state = tile(0, init, True, True)
    last = qi // (kt // QB)
    state = lax.fori_loop(1, last, lambda j, st: tile(j, st, False, False), state)
    state = lax.cond(last >= 1, lambda st: tile(last, st, False, True), lambda st: st, state)
    lam = _diff_lambda(lam_ref, lam_init)
    (_, l1, o1), (_, l2, o2) = state
    a = o1 / jnp.maximum(l1, 1e-30) - lam * (o2 / jnp.maximum(l2, 1e-30))
    a = a * lax.rsqrt(jnp.mean(jnp.square(a), axis=0, keepdims=True) + LN_EPS)
    a = a * sg_ref[...] * (1.0 - lam_init)
    o_ref[0] = jnp.concatenate([a[:, r * QB:(r + 1) * QB].T for r in range(DIFF_REP)],
                               axis=-1).astype(o_ref.dtype)


def _heads_on_lanes(table, group):
    h, k, q = table.shape
    return table.reshape(h // group, group, k, q).transpose(0, 2, 1, 3).reshape(h // group, k, group * q)


def diff_prompt(q, k, vt, bias0_t, c31, lam_rows, sub_g_col, lam_init):
    b, gk, _, _, t, _ = q.shape
    tp = k.shape[3]
    w = DIFF_REP * QB
    b0 = _heads_on_lanes(bias0_t, DIFF_REP)
    c31 = _heads_on_lanes(jnp.broadcast_to(c31[:, None, None], (DIFF_HEADS, 1, QB)), DIFF_REP)
    return pl.pallas_call(
        functools.partial(_diff_prompt_kernel, lam_init=lam_init), grid=(b, gk, t // QB),
        in_specs=[pl.BlockSpec((1, 1, DIFF_REP, 2, QB, HEAD_DIM), lambda b_, g_, i: (b_, g_, 0, 0, i, 0)),
                  pl.BlockSpec((1, 1, 2, tp, HEAD_DIM), lambda b_, g_, i: (b_, g_, 0, 0, 0)),
                  pl.BlockSpec((1, 1, 2 * HEAD_DIM, tp), lambda b_, g_, i: (b_, g_, 0, 0)),
                  pl.BlockSpec((1, b0.shape[1], w), lambda b_, g_, i: (g_, 0, 0)),
                  pl.BlockSpec((1, 1, w), lambda b_, g_, i: (g_, 0, 0)),
                  pl.BlockSpec((4, HEAD_DIM), lambda b_, g_, i: (0, 0)),
                  pl.BlockSpec((2 * HEAD_DIM, 1), lambda b_, g_, i: (0, 0))],
        out_specs=pl.BlockSpec((1, QB, DIFF_REP * 2 * HEAD_DIM), lambda b_, g_, i: (b_, i, g_)),
        out_shape=jax.ShapeDtypeStruct((b, t, DIFF_HEADS * 2 * HEAD_DIM), BF16),
        compiler_params=_params(("arbitrary",) * 3), name="diff_prompt")(
            q, k, vt, b0, c31, lam_rows, sub_g_col)


CMP_BAND_BACK = 4
CMP_BAND = 16


def _cmp_band_table(tab):
    first = [-((m - CMP_BAND_BACK) * CMP_BLOCK + CMP_BLOCK - 1) for m in range(CMP_BAND)]
    delta = jnp.stack([_window(tab, d0, QB) for d0 in first], axis=1) - tab[:, -1][:, None, None]
    hi = delta.astype(BF16)
    lo = (delta - hi.astype(F32)).astype(BF16)
    return jnp.stack([hi, lo], axis=1)


def _select_blocks(score_ref, n_rows, n_keep):
    sc = score_ref[...]
    blk = _iota(sc.shape, 0)

    def body(i, rank):
        r = score_ref[pl.ds(i, 1), :]
        better = (r > sc) | ((r == sc) & (i < blk))
        return rank + jnp.where(better, 1.0, 0.0)

    rank = lax.fori_loop(0, n_rows, body, jnp.zeros(sc.shape, F32), unroll=8)
    return rank < n_keep


def _nsa_prompt_kernel(q_ref, kc_ref, vct_ref, ks_ref, vst_ref, kw_ref, vwt_ref, bc_ref, b0_ref, c31_ref,
                       gate_ref, o_ref, impt_ref, score_ref, *, n_sel_blocks, n_keep):
    qi = pl.program_id(2)
    q0 = qi * QB
    n_tiles = qi // (KT // QB) + 1
    nc = kc_ref.shape[2]
    ns = nc // 2
    heads = range(NSA_REP)
    w = NSA_REP * QB
    qcat = jnp.concatenate([(q_ref[0, 0, r].astype(F32) * ATT_SCALE).astype(BF16) for r in heads], axis=0)
    on_lanes = lambda rows: jnp.concatenate(rows, axis=1)

    valid_c = _iota((nc, w), 1) % QB + q0 - (_iota((nc, w), 0) * CMP_BLOCK + (CMP_BLOCK - 1)) >= 0
    first_blk = qi * (QB // CMP_BLOCK) - CMP_BAND_BACK
    place = jnp.where(_iota((nc, CMP_BAND), 0) == _iota((nc, CMP_BAND), 1) + first_blk, 1.0, 0.0).astype(BF16)
    bias = c31_ref[0] + _dot(place, bc_ref[0, 0]) + _dot(place, bc_ref[0, 1])
    p = _softmax_full_t(_dot_nt(kc_ref[0, 0], qcat) + bias, valid_c)
    o_c = _dot(vct_ref[0, 0], p.astype(BF16))
    imp = p[:, :QB]
    for r in range(1, NSA_REP):
        imp = imp + p[:, r * QB:(r + 1) * QB]
    impt_ref[...] = imp
    imp_s = impt_ref[pl.ds(0, ns, stride=2), :] + impt_ref[pl.ds(1, ns, stride=2), :]
    blk = _iota((ns, QB), 0)
    qpos = _iota((ns, QB), 1) + q0
    forced = (blk == qpos // SEL_BLOCK) | (blk == 0)
    future = blk * SEL_BLOCK > qpos
    score = jnp.where(forced, FORCE, jnp.where(future, -FORCE, imp_s))
    score_ref[...] = jnp.where(blk < n_sel_blocks, score, -3e38)
    sel_t = jnp.where(_select_blocks(score_ref, ns, n_keep), 1.0, 0.0).astype(BF16)

    key = _iota((KT, w), 0)
    qry = _iota((KT, w), 1) % QB
    causal0 = qry + KPAD - key >= 0
    key_blk = _iota((KT, ns), 0) // SEL_BLOCK
    blk_e = _iota((KT, ns), 1)

    def branch(k_ref, vt_ref, hi, valid_fn):
        def tile(j, state, first):
            start = pl.multiple_of(q0 - j * KT, QB)
            k = k_ref[0, 0, pl.ds(start, KT), :]
            vt = vt_ref[0, 0, :, pl.ds(start, KT)]
            s = _dot_nt(k, qcat) + (b0_ref[0] if first else c31_ref[0])
            return _softmax_step_t(s, valid_fn(j, start, first), *state, vt)

        init = (jnp.full((1, w), NEG, F32), jnp.zeros((1, w), F32), jnp.zeros((HEAD_DIM, w), F32))
        state = tile(0, init, True)
        _, l, acc = lax.fori_loop(1, hi, lambda j, st: tile(j, st, False), state)
        return acc / jnp.maximum(l, 1e-30)

    def valid_sel(j, start, first):
        base = start // SEL_BLOCK - KPAD // SEL_BLOCK
        expand = jnp.where(blk_e == key_blk + base, 1.0, 0.0).astype(BF16)
        chosen = on_lanes([_dot(expand, sel_t)] * NSA_REP) > 0.5
        return (chosen & causal0) if first else chosen

    def valid_win(j, start, first):
        ok = key >= KPAD - start
        if first:
            return ok & causal0
        return ok & (qry + KPAD - key + j * KT < WINDOW)

    o_s = branch(ks_ref, vst_ref, n_tiles, valid_sel)
    o_w = branch(kw_ref, vwt_ref, jnp.minimum(n_tiles, (WINDOW + QB - 1) // KT + 1), valid_win)

    gate = jax.nn.sigmoid(gate_ref[0, 0].astype(F32))
    g0, g1, g2 = (on_lanes([gate[3 * r + i:3 * r + i + 1, :] for r in heads]) for i in range(3))
    o = g0 * o_c + g1 * o_s + g2 * o_w
    o_ref[0] = jnp.concatenate([o[:, r * QB:(r + 1) * QB] for r in heads], axis=0).T.astype(o_ref.dtype)


def nsa_prompt(q, kc, vct, ks, vst, kw, vwt, bias_c, bias0_t, c31, gates_t, n_sel_blocks):
    b, g, _, t, _ = q.shape
    tp = ks.shape[2]
    nc = kc.shape[2]
    k_spec = pl.BlockSpec((1, 1, tp, HEAD_DIM), lambda b_, g_, i: (b_, g_, 0, 0))
    vt_spec = pl.BlockSpec((1, 1, HEAD_DIM, tp), lambda b_, g_, i: (b_, g_, 0, 0))
    kern = functools.partial(_nsa_prompt_kernel, n_sel_blocks=n_sel_blocks,
                             n_keep=min(N_SEL, n_sel_blocks))
    w = NSA_REP * QB
    b0 = _heads_on_lanes(bias0_t, NSA_REP)
    c31 = _heads_on_lanes(jnp.broadcast_to(c31[:, None, None], (NSA_HEADS, 1, QB)), NSA_REP)
    band = _heads_on_lanes(bias_c.reshape(NSA_HEADS, 2 * CMP_BAND, QB), NSA_REP).reshape(g, 2, CMP_BAND, w)
    return pl.pallas_call(
        kern, grid=(b, g, t // QB),
        in_specs=[pl.BlockSpec((1, 1, NSA_REP, QB, HEAD_DIM), lambda b_, g_, i: (b_, g_, 0, i, 0)),
                  pl.BlockSpec((1, 1, nc, HEAD_DIM), lambda b_, g_, i: (b_, g_, 0, 0)),
                  pl.BlockSpec((1, 1, HEAD_DIM, nc), lambda b_, g_, i: (b_, g_, 0, 0)),
                  k_spec, vt_spec, k_spec, vt_spec,
                  pl.BlockSpec((1, 2, CMP_BAND, w), lambda b_, g_, i: (g_, 0, 0, 0)),
                  pl.BlockSpec((1, KT, w), lambda b_, g_, i: (g_, 0, 0)),
                  pl.BlockSpec((1, 1, w), lambda b_, g_, i: (g_, 0, 0)),
                  pl.BlockSpec((1, 1, 4 * NSA_REP, QB), lambda b_, g_, i: (b_, g_, 0, i))],
        out_specs=pl.BlockSpec((1, QB, NSA_REP * HEAD_DIM), lambda b_, g_, i: (b_, i, g_)),
        out_shape=jax.ShapeDtypeStruct((b, t, NSA_HEADS * HEAD_DIM), BF16),
        scratch_shapes=[pltpu.VMEM((nc, QB), F32), pltpu.VMEM((nc // 2, QB), F32)],
        compiler_params=_params(("arbitrary",) * 3), name="nsa_prompt")(
            q, kc, vct, ks, vst, kw, vwt, band, b0, c31, gates_t)


CMP_PP = 16
CMP_HALF = CMP_BLOCK // 2


def _compress_kernel(pt_ref, w_ref, pe_ref, *rest, pp, positions_on_lanes):
    pages, o_ref, lo_ref, hi_ref = rest[:pp], rest[pp], rest[pp + 1], rest[pp + 2]
    if positions_on_lanes:
        stage_ref = rest[pp + 3]
        for p in range(pp):
            stage_ref[p] = pages[p][...].T
        pages = [stage_ref.at[p] for p in range(pp)]
    width = 2 * HEAD_DIM
    acc = jnp.zeros((pp * SUBLANES, 2 * width), F32)
    for i in range(CMP_HALF):
        rows = [pages[p][pl.ds(i, SUBLANES, stride=CMP_HALF), :] + pe_ref[i] for p in range(pp)]
        acc = acc + _dot(jnp.concatenate(rows, axis=0).astype(BF16), w_ref[0, i])
    lo_ref[...] = acc[:, :width]
    hi_ref[...] = acc[:, width:]
    n = pp * SUBLANES // 2
    o_ref[0] = lo_ref[pl.ds(0, n, stride=2), :] + hi_ref[pl.ds(1, n, stride=2), :]


def compress_pages(page_table, src, layer, block, w_cat, pe_tiles, positions_on_lanes=False):
    nseq, n_pages = page_table.shape
    pp = min(CMP_PP, n_pages)
    width = 2 * HEAD_DIM

    def page_spec(p):
        def index(b, s, kv, pt):
            where = (block + kv, 0) if positions_on_lanes else (0, block + kv)
            return (pt[b, s * pp + p], layer) + where
        return pl.BlockSpec((None, None, PAGE, width), index)

    scratch = [pltpu.VMEM((pp * SUBLANES, width), F32), pltpu.VMEM((pp * SUBLANES, width), F32)]
    if positions_on_lanes:
        scratch.append(pltpu.VMEM((pp, PAGE, width), F32))
    grid_spec = pltpu.PrefetchScalarGridSpec(
        num_scalar_prefetch=1, grid=(nseq, n_pages // pp, 2),
        in_specs=[pl.BlockSpec((1, CMP_HALF, width, 2 * width), lambda b, s, kv, pt: (kv, 0, 0, 0)),
                  pl.BlockSpec((CMP_HALF, SUBLANES, width), lambda b, s, kv, pt: (0, 0, 0))]
        + [page_spec(p) for p in range(pp)],
        out_specs=pl.BlockSpec((1, pp * 4, width), lambda b, s, kv, pt: (b, s, kv)),
        scratch_shapes=scratch)
    return pl.pallas_call(
        functools.partial(_compress_kernel, pp=pp, positions_on_lanes=positions_on_lanes), grid_spec=grid_spec,
        out_shape=jax.ShapeDtypeStruct((nseq, n_pages * 4, 2 * width), F32),
        compiler_params=_params(("arbitrary",) * 3), name="compress")(
            page_table, w_cat, pe_tiles, *([src] * pp))


def _compress_weights(pe, wk, wv):
    def cat(w):
        w = w.reshape(CMP_BLOCK, HEAD_DIM, HEAD_DIM)
        z = jnp.zeros_like(w)
        full = jnp.concatenate([jnp.concatenate([w, z], -1), jnp.concatenate([z, w], -1)], axis=1)
        return jnp.concatenate([full[:CMP_HALF], full[CMP_HALF:]], axis=-1)

    w_cat = jnp.stack([cat(wk), cat(wv)]).astype(BF16)
    pe2 = jnp.tile(pe.astype(F32), (1, 2))
    pe_tiles = jnp.stack([pe2[:CMP_HALF], pe2[CMP_HALF:]], axis=1)
    pe_tiles = jnp.tile(pe_tiles, (1, SUBLANES // 2, 1))
    return w_cat, pe_tiles


SB_PP = 8
DIFF_PP = 8
SEL_PP = 16


def _page_specs(pp, n_pages, rows, row_block, layer):
    def spec(p):
        def index(b, s, pt):
            return (pt[b, n_pages - 1 - (s * pp + p)], layer, row_block, 0)
        return pl.BlockSpec((None, None, rows, LANES), index)
    return [spec(p) for p in range(pp)]


def _positions_on_lanes(cache):
    pool, layers = cache.shape[:2]
    return jnp.transpose(cache, (0, 1, 3, 4, 5, 2)).reshape(pool, layers, -1, PAGE)


def _token_of_row(shape):
    return _iota(shape, 0) % DEC_S


def _sb_sample_kernel(pt_ref, q_ref, new_ref, u_ref, cin_ref, ain_ref, *rest, pp, first):
    pages, (o_ref, cout_ref, aout_ref, carry_ref, acc_ref) = rest[:pp], rest[pp:]
    s = pl.program_id(1)
    q = q_ref[0]
    u = u_ref[...]
    rows = SB_HEADS * DEC_S

    @pl.when(s == 0)
    def _():
        if first:
            k = new_ref[0, :, :SB_W]
            v = new_ref[0, :, SB_W:]
            valid = _iota((rows, PAGE), 1) < _token_of_row((rows, PAGE))
            carry, acc = _sb_step(_dot_nt(q, k), valid, jnp.zeros((rows, 1), F32),
                                  jnp.zeros((rows, SB_W), F32), v, u)
        else:
            carry, acc = cin_ref[0], ain_ref[0]
        carry_ref[...] = carry
        acc_ref[...] = acc

    for p in range(pp):
        @pl.when(jnp.max(carry_ref[...]) > SB_DEAD)
        def _(p=p):
            kt = pages[p][:SB_W, :].astype(BF16)
            vt = pages[p][SB_W:, :].astype(BF16)
            carry, acc = _sb_step(_dot(q, kt), None, carry_ref[...], acc_ref[...], vt, u, v_on_lanes=True)
            carry_ref[...] = carry
            acc_ref[...] = acc

    @pl.when(s == pl.num_programs(1) - 1)
    def _():
        a = acc_ref[...]
        o_ref[0] = jnp.concatenate(
            [a[h * DEC_S:(h + 1) * DEC_S, h * HEAD_DIM:(h + 1) * HEAD_DIM] for h in range(SB_HEADS)],
            axis=-1).astype(o_ref.dtype)
        cout_ref[0] = carry_ref[...]
        aout_ref[0] = a


def _sb_sample_call(page_table, qbd, new_kv, cache, layer, carry, acc, skip, count, first):
    db = page_table.shape[0]
    pp = min(SB_PP, count)
    rows = SB_HEADS * DEC_S
    per_seq = lambda shape: pl.BlockSpec((1,) + shape, lambda b, s, pt: (b,) + (0,) * len(shape))
    grid_spec = pltpu.PrefetchScalarGridSpec(
        num_scalar_prefetch=1, grid=(db, count // pp),
        in_specs=[per_seq((rows, SB_W)), per_seq((PAGE, 2 * SB_W)),
                  pl.BlockSpec((PAGE, PAGE), lambda b, s, pt: (0, 0)), per_seq((rows, 1)), per_seq((rows, SB_W))]
        + _page_specs(pp, page_table.shape[1] - skip, 2 * SB_W, 0, layer),
        out_specs=[per_seq((DEC_S, SB_W)), per_seq((rows, 1)), per_seq((rows, SB_W))],
        scratch_shapes=[pltpu.VMEM((rows, 1), F32), pltpu.VMEM((rows, SB_W), F32)])
    return pl.pallas_call(
        functools.partial(_sb_sample_kernel, pp=pp, first=first), grid_spec=grid_spec,
        out_shape=[jax.ShapeDtypeStruct((db, DEC_S, SB_W), BF16), jax.ShapeDtypeStruct((db, rows, 1), F32),
                   jax.ShapeDtypeStruct((db, rows, SB_W), F32)],
        compiler_params=_params(("arbitrary",) * 2), name="sb_sample")(
            page_table, qbd, new_kv, _later_matrix(PAGE), carry, acc, *([cache] * pp))


def sb_sample(page_table, qbd, new_kv, cache, layer):
    db, n_pages = page_table.shape
    rows = SB_HEADS * DEC_S
    head = min(SB_PP, n_pages)
    zeros = (jnp.zeros((db, rows, 1), F32), jnp.zeros((db, rows, SB_W), F32))
    out, carry, acc = _sb_sample_call(page_table, qbd, new_kv, cache, layer, *zeros, 0, head, True)
    if n_pages == head:
        return out
    earlier = lambda: _sb_sample_call(page_table, qbd, new_kv, cache, layer, carry, acc, head,
                                      n_pages - head, False)[0]
    return lax.cond(jnp.max(carry) > SB_DEAD, earlier, lambda: out)


DIFF_GROUP_ROWS = DIFF_REP * 2 * DEC_S
DIFF_ROW_STRIDE = 2 * DIFF_KV_HEADS


def _diff_sample_tiles(q, loads, bias, valid, m, l, acc):
    heads = range(DIFF_KV_HEADS)
    rows = lambda g: slice(g * DIFF_GROUP_ROWS, (g + 1) * DIFF_GROUP_ROWS)
    s = jnp.concatenate(
        [jnp.concatenate([_dot_nt(q[rows(g)], load(g).astype(BF16)) for g in heads], axis=0)
         for load in loads], axis=1) + bias
    if valid is not None:
        s = jnp.where(valid, s, NEG)
    m_new = jnp.maximum(m, jnp.max(s, axis=-1, keepdims=True))
    p = jnp.exp(s - m_new)
    if valid is not None:
        p = jnp.where(valid, p, 0.0)
    alpha = jnp.exp(m - m_new)
    l = alpha * l + jnp.sum(p, axis=-1, keepdims=True)
    p = p.astype(BF16)
    pv = []
    for g in heads:
        terms = [_dot(p[rows(g), i * PAGE:(i + 1) * PAGE], load(DIFF_KV_HEADS + g).astype(BF16))
                 for i, load in enumerate(loads)]
        pv.append(functools.reduce(lambda a, b: a + b, terms))
    return m_new, l, alpha * acc + jnp.concatenate(pv, axis=0)


def _diff_sample_kernel(pt_ref, q_ref, new_ref, cvec_ref, blast_ref, bnew_ref, lam_ref, sg_ref, *rest,
                        pp, lam_init):
    pages, o_ref, m_ref, l_ref, acc_ref = rest[:pp], rest[pp], rest[pp + 1], rest[pp + 2], rest[pp + 3]
    s = pl.program_id(1)
    q = q_ref[0]
    rows = DIFF_HEADS * 2 * DEC_S
    strided = lambda ref: (lambda off: ref[pl.ds(off, PAGE, stride=DIFF_ROW_STRIDE), :])

    @pl.when(s == 0)
    def _():
        valid = _iota((rows, PAGE), 1) <= _token_of_row((rows, PAGE))
        m, l, acc = _diff_sample_tiles(q, [strided(new_ref)], bnew_ref[...], valid,
                                       jnp.full((rows, 1), NEG, F32), jnp.zeros((rows, 1), F32),
                                       jnp.zeros((rows, 2 * HEAD_DIM), F32))
        m_ref[...] = m
        l_ref[...] = l
        acc_ref[...] = acc

    bias = jnp.broadcast_to(cvec_ref[...], (rows, PAGE))
    bias = jnp.concatenate([jnp.where(s == 0, blast_ref[...], bias)] + [bias] * (pp - 1), axis=1)
    m, l, acc = _diff_sample_tiles(q, [strided(ref) for ref in pages], bias, None,
                                   m_ref[...], l_ref[...], acc_ref[...])
    m_ref[...] = m
    l_ref[...] = l
    acc_ref[...] = acc

    @pl.when(s == pl.num_programs(1) - 1)
    def _():
        a = acc_ref[...]
        ls = l_ref[...]
        lam = _diff_lambda(lam_ref, lam_init)
        outs = []
        for head in range(DIFF_HEADS):
            r0 = head * 2 * DEC_S
            outs.append(_diff_finish(a[r0:r0 + DEC_S], ls[r0:r0 + DEC_S], a[r0 + DEC_S:r0 + 2 * DEC_S],
                                     ls[r0 + DEC_S:r0 + 2 * DEC_S], lam, sg_ref[...], lam_init))
        o_ref[0] = jnp.concatenate(outs, axis=-1).astype(o_ref.dtype)


def diff_sample(page_table, qc, new_kv, cvec, blast, bnew, lam_rows, sub_g, cache, layer, lam_init):
    db, n_pages = page_table.shape
    pp = min(DIFF_PP, n_pages)
    rows = DIFF_HEADS * 2 * DEC_S
    page_rows = PAGE * DIFF_ROW_STRIDE
    const = lambda shape: pl.BlockSpec(shape, lambda b, s, pt: (0,) * len(shape))
    grid_spec = pltpu.PrefetchScalarGridSpec(
        num_scalar_prefetch=1, grid=(db, n_pages // pp),
        in_specs=[pl.BlockSpec((1, rows, 2 * HEAD_DIM), lambda b, s, pt: (b, 0, 0)),
                  pl.BlockSpec((None, page_rows, LANES), lambda b, s, pt: (b, 0, 0)),
                  const((rows, 1)), const((rows, PAGE)), const((rows, PAGE)),
                  const((4, HEAD_DIM)), const((1, 2 * HEAD_DIM))]
        + _page_specs(pp, n_pages, page_rows, 0, layer),
        out_specs=pl.BlockSpec((1, DEC_S, DIFF_HEADS * 2 * HEAD_DIM), lambda b, s, pt: (b, 0, 0)),
        scratch_shapes=[pltpu.VMEM((rows, 1), F32), pltpu.VMEM((rows, 1), F32),
                        pltpu.VMEM((rows, 2 * HEAD_DIM), F32)])
    return pl.pallas_call(
        functools.partial(_diff_sample_kernel, pp=pp, lam_init=lam_init), grid_spec=grid_spec,
        out_shape=jax.ShapeDtypeStruct((db, DEC_S, DIFF_HEADS * 2 * HEAD_DIM), BF16),
        compiler_params=_params(("arbitrary",) * 2), name="diff_sample")(
            page_table, qc, new_kv, cvec, blast, bnew, lam_rows, sub_g, *([cache] * pp))


NSA_ROWS = NSA_HEADS * DEC_S
NSA_GT = NSA_GROUPS * DEC_S


def _nsa_diag(a):
    pieces = []
    for r in range(NSA_REP):
        for g in range(NSA_GROUPS):
            r0 = r * NSA_GT + g * DEC_S
            pieces.append(a[r0:r0 + DEC_S, g * HEAD_DIM:(g + 1) * HEAD_DIM])
    return jnp.concatenate(pieces, axis=0)


def _nsa_sample_cw_kernel(q_ref, kcvc_ref, bc_ref, win_ref, bw_ref, oc_ref, ow_ref, sel_ref,
                          impt_ref, score_ref, *, win_len, n_keep):
    q = q_ref[0]
    nc = kcvc_ref.shape[1]
    ns = nc // 2
    kc = kcvc_ref[0, :, :NSA_KV_W].astype(BF16)
    vc = kcvc_ref[0, :, NSA_KV_W:].astype(BF16)
    p = _softmax_full(_dot_nt(q, kc) + bc_ref[...], None)
    oc_ref[0] = _nsa_diag(_dot(p.astype(BF16), vc))
    imp = p[0:NSA_GT]
    for r in range(1, NSA_REP):
        imp = imp + p[r * NSA_GT:(r + 1) * NSA_GT]
    imp = jnp.concatenate([imp, jnp.zeros((LANES - NSA_GT, nc), F32)], axis=0)
    impt_ref[...] = imp.T
    imp_s = impt_ref[pl.ds(0, ns, stride=2), :] + impt_ref[pl.ds(1, ns, stride=2), :]
    blk = _iota((ns, LANES), 0)
    score_ref[...] = jnp.where(blk == 0, FORCE, imp_s)
    sel_t = jnp.where(_select_blocks(score_ref, ns, n_keep), 1.0, 0.0)
    sel_ref[0] = sel_t.T[:NSA_GT].astype(sel_ref.dtype)

    wl = win_ref.shape[1]
    kw = win_ref[0, :, :NSA_KV_W]
    vw = win_ref[0, :, NSA_KV_W:]
    tok = _token_of_row((NSA_ROWS, wl))
    col = _iota((NSA_ROWS, wl), 1)
    valid = (col > tok + (win_len - WINDOW)) & (col <= tok + win_len) & (col < win_len + DEC_S)
    pw = _softmax_full(_dot_nt(q, kw) + bw_ref[...], valid)
    ow_ref[0] = _nsa_diag(_dot(pw.astype(BF16), vw))


def nsa_sample_cw(qbd, kcvc, bias_c, win_all, bias_w, win_len, n_keep):
    db = qbd.shape[0]
    nc = kcvc.shape[1]
    wl = win_all.shape[1]
    const = lambda shape: pl.BlockSpec(shape, lambda b: (0,) * len(shape))
    return pl.pallas_call(
        functools.partial(_nsa_sample_cw_kernel, win_len=win_len, n_keep=n_keep), grid=(db,),
        in_specs=[pl.BlockSpec((1, NSA_ROWS, NSA_KV_W), lambda b: (b, 0, 0)),
                  pl.BlockSpec((1, nc, 2 * NSA_KV_W), lambda b: (b, 0, 0)),
                  const((NSA_ROWS, nc)),
                  pl.BlockSpec((1, wl, 2 * NSA_KV_W), lambda b: (b, 0, 0)),
                  const((NSA_ROWS, wl))],
        out_specs=[pl.BlockSpec((1, NSA_ROWS, HEAD_DIM), lambda b: (b, 0, 0)),
                   pl.BlockSpec((1, NSA_ROWS, HEAD_DIM), lambda b: (b, 0, 0)),
                   pl.BlockSpec((1, NSA_GT, nc // 2), lambda b: (b, 0, 0))],
        out_shape=[jax.ShapeDtypeStruct((db, NSA_ROWS, HEAD_DIM), F32),
                   jax.ShapeDtypeStruct((db, NSA_ROWS, HEAD_DIM), F32),
                   jax.ShapeDtypeStruct((db, NSA_GT, nc // 2), BF16)],
        scratch_shapes=[pltpu.VMEM((nc, LANES), F32), pltpu.VMEM((nc // 2, LANES), F32)],
        compiler_params=_params(("arbitrary",)), name="nsa_sample_cw")(qbd, kcvc, bias_c, win_all, bias_w)


def _nsa_sample_sel_kernel(pt_ref, q_ref, new_ref, sel_ref, cvec_ref, blast_ref, bnew_ref, oc_ref, ow_ref,
                           gate_ref, *rest, pp, n_pages):
    pages, o_ref, m_ref, l_ref, acc_ref = rest[:pp], rest[pp], rest[pp + 1], rest[pp + 2], rest[pp + 3]
    s = pl.program_id(1)
    q = q_ref[0]
    sel = sel_ref[0]
    ns = sel.shape[1]

    @pl.when(s == 0)
    def _():
        k = new_ref[0, :, :NSA_KV_W]
        v = new_ref[0, :, NSA_KV_W:]
        valid = _iota((NSA_ROWS, PAGE), 1) <= _token_of_row((NSA_ROWS, PAGE))
        m, l, acc = _softmax_step(_dot_nt(q, k) + bnew_ref[...], valid, jnp.full((NSA_ROWS, 1), NEG, F32),
                                  jnp.zeros((NSA_ROWS, 1), F32), jnp.zeros((NSA_ROWS, NSA_KV_W), F32), v)
        m_ref[...] = m
        l_ref[...] = l
        acc_ref[...] = acc

    m, l, acc = m_ref[...], l_ref[...], acc_ref[...]
    blk_e = _iota((ns, PAGE), 0)
    col_e = _iota((ns, PAGE), 1) // SEL_BLOCK
    for p in range(pp):
        logical = n_pages - 1 - (s * pp + p)
        kt = pages[p][:NSA_KV_W, :].astype(BF16)
        vt = pages[p][NSA_KV_W:, :].astype(BF16)
        expand = jnp.where(blk_e == col_e + logical * (PAGE // SEL_BLOCK), 1.0, 0.0).astype(BF16)
        chosen = _dot(sel, expand) > 0.5
        valid = jnp.concatenate([chosen] * NSA_REP, axis=0)
        bias = cvec_ref[...]
        if p == 0:
            bias = jnp.where(s == 0, blast_ref[...], bias)
        m, l, acc = _softmax_step(_dot(q, kt) + bias, valid, m, l, acc, vt, v_on_lanes=True)
    m_ref[...] = m
    l_ref[...] = l
    acc_ref[...] = acc

    @pl.when(s == pl.num_programs(1) - 1)
    def _():
        o_s = _nsa_diag(acc_ref[...] / jnp.maximum(l_ref[...], 1e-30))
        gate = jax.nn.sigmoid(gate_ref[0].astype(F32))
        o = gate[:, 0:1] * oc_ref[0] + gate[:, 1:2] * o_s + gate[:, 2:3] * ow_ref[0]
        pieces = []
        for g in range(NSA_GROUPS):
            for r in range(NSA_REP):
                r0 = r * NSA_GT + g * DEC_S
                pieces.append(o[r0:r0 + DEC_S])
        o_ref[0] = jnp.concatenate(pieces, axis=-1).astype(o_ref.dtype)


def nsa_sample_sel(page_table, qbd, new_kv, sel, cvec, blast, bnew, o_c, o_w, gates, cache, layer):
    db, n_pages = page_table.shape
    pp = min(SEL_PP, n_pages)
    ns = sel.shape[2]
    const = lambda shape: pl.BlockSpec(shape, lambda b, s, pt: (0,) * len(shape))
    per_seq = lambda shape: pl.BlockSpec((1,) + shape, lambda b, s, pt: (b,) + (0,) * len(shape))
    grid_spec = pltpu.PrefetchScalarGridSpec(
        num_scalar_prefetch=1, grid=(db, n_pages // pp),
        in_specs=[per_seq((NSA_ROWS, NSA_KV_W)), per_seq((PAGE, 2 * NSA_KV_W)), per_seq((NSA_GT, ns)),
                  const((NSA_ROWS, 1)), const((NSA_ROWS, PAGE)), const((NSA_ROWS, PAGE)),
                  per_seq((NSA_ROWS, HEAD_DIM)), per_seq((NSA_ROWS, HEAD_DIM)), per_seq((NSA_ROWS, LANES))]
        + _page_specs(pp, n_pages, 2 * NSA_KV_W, 1, layer),
        out_specs=pl.BlockSpec((1, DEC_S, NSA_HEADS * HEAD_DIM), lambda b, s, pt: (b, 0, 0)),
        scratch_shapes=[pltpu.VMEM((NSA_ROWS, 1), F32), pltpu.VMEM((NSA_ROWS, 1), F32),
                        pltpu.VMEM((NSA_ROWS, NSA_KV_W), F32)])
    return pl.pallas_call(
        functools.partial(_nsa_sample_sel_kernel, pp=pp, n_pages=n_pages), grid_spec=grid_spec,
        out_shape=jax.ShapeDtypeStruct((db, DEC_S, NSA_HEADS * HEAD_DIM), BF16),
        compiler_params=_params(("arbitrary",) * 2), name="nsa_sample_sel")(
            page_table, qbd, new_kv, sel, cvec, blast, bnew, o_c, o_w, gates, *([cache] * pp))


def _layer_norm(y, g, b):
    mu = jnp.mean(y, axis=-1, keepdims=True)
    d = y - mu
    var = jnp.mean(d * d, axis=-1, keepdims=True)
    return d * lax.rsqrt(var + LN_EPS) * g + b


def _split_bf16(x):
    hi = x.astype(BF16)
    return hi, (x - hi.astype(F32)).astype(BF16)


def _mix_router_kernel(a1_ref, a2_ref, w1_ref, w2_ref, x_ref, g_ref, b_ref, wr_ref, br_ref, u_ref,
                       xn_ref, xb_ref, rt_ref, gt_ref, cnt_ref, *, alpha):
    mix = _dot(a1_ref[...], w1_ref[...]) + _dot(a2_ref[...], w2_ref[...])
    xn = _layer_norm(alpha * x_ref[...] + mix, g_ref[...], b_ref[...])
    xn_ref[...] = xn
    xb_ref[...] = xn.astype(BF16)
    xh, xl = _split_bf16(xn)
    wh, wl = _split_bf16(wr_ref[...])
    logits = _dot_nt(wh, xh) + _dot_nt(wh, xl) + _dot_nt(wl, xh) + br_ref[...]
    e_iota = _iota(logits.shape, 0)
    work = logits
    vals, sels = [], []
    for _ in range(TOP_K):
        top = jnp.max(work, axis=0, keepdims=True)
        idx = jnp.min(jnp.where(work == top, e_iota, N_EXPERTS), axis=0, keepdims=True)
        sel = e_iota == idx
        vals.append(top)
        sels.append(sel)
        work = jnp.where(sel, -jnp.inf, work)
    ex = [jnp.exp(v - vals[0]) for v in vals]
    den = ex[0] + ex[1] + ex[2] + ex[3]
    gate = jnp.zeros(logits.shape, F32)
    chosen = sels[0]
    for k in range(TOP_K):
        gate = jnp.where(sels[k], ex[k] / den, gate)
        chosen = chosen | sels[k]
    cf = jnp.where(chosen, 1.0, 0.0)
    rank = _dot(cf.astype(BF16), u_ref[...])
    rt_ref[0] = jnp.where(chosen, rank, -1.0)
    gt_ref[0] = gate
    cnt = jnp.sum(cf, axis=1, keepdims=True)
    cnt_ref[0] = jnp.broadcast_to(cnt, (N_EXPERTS, LANES)).astype(jnp.int32)


def mix_router(a1, a2, w1, w2, x, ln_g, ln_b, w_router_t, b_router, tt, alpha):
    n = x.shape[0]
    half = a1.shape[1]
    j = np.arange(tt)
    before = jnp.asarray((j[:, None] < j[None, :]).astype(np.float32), dtype=BF16)
    tile = lambda w: pl.BlockSpec((tt, w), lambda i: (i, 0))
    const = lambda shape: pl.BlockSpec(shape, lambda i: (0,) * len(shape))
    route = pl.BlockSpec((1, N_EXPERTS, tt), lambda i: (i, 0, 0))
    return pl.pallas_call(
        functools.partial(_mix_router_kernel, alpha=alpha), grid=(n // tt,),
        in_specs=[tile(half), tile(half), const((half, D_MODEL)), const((half, D_MODEL)), tile(D_MODEL),
                  const((1, D_MODEL)), const((1, D_MODEL)), const((N_EXPERTS, D_MODEL)),
                  const((N_EXPERTS, 1)), const((tt, tt))],
        out_specs=[tile(D_MODEL), tile(D_MODEL), route, route,
                   pl.BlockSpec((1, N_EXPERTS, LANES), lambda i: (i, 0, 0))],
        out_shape=[jax.ShapeDtypeStruct((n, D_MODEL), F32), jax.ShapeDtypeStruct((n, D_MODEL), BF16),
                   jax.ShapeDtypeStruct((n // tt, N_EXPERTS, tt), F32),
                   jax.ShapeDtypeStruct((n // tt, N_EXPERTS, tt), F32),
                   jax.ShapeDtypeStruct((n // tt, N_EXPERTS, LANES), jnp.int32)],
        compiler_params=_params(("arbitrary",)), name="mix_router")(
            a1, a2, w1, w2, x, ln_g, ln_b, w_router_t, b_router, before)


def _pair_permutation():
    c = np.arange(2 * LANES)
    dest = np.where(c % 2 == 0, c // 2, LANES + c // 2)
    return jnp.asarray((dest[:, None] == c[None, :]).astype(np.float32), dtype=BF16)


def _split_pairs_kernel(w_ref, s_ref, o_ref):
    s = s_ref[...]
    for blk in range(w_ref.shape[2] // (2 * LANES)):
        cols = slice(blk * 2 * LANES, (blk + 1) * 2 * LANES)
        o_ref[0, :, cols] = _dot(w_ref[0, :, cols].astype(BF16), s).astype(BF16)


def split_pairs(w_up):
    e, d, f2 = w_up.shape
    return pl.pallas_call(
        _split_pairs_kernel, grid=(e,),
        in_specs=[pl.BlockSpec((1, d, f2), lambda i: (i, 0, 0)),
                  pl.BlockSpec((2 * LANES, 2 * LANES), lambda i: (0, 0))],
        out_specs=pl.BlockSpec((1, d, f2), lambda i: (i, 0, 0)),
        out_shape=jax.ShapeDtypeStruct((e, d, f2), BF16),
        compiler_params=_params(("arbitrary",)), name="split_pairs")(w_up, _pair_permutation())


def _moe_kernel(cnt_ref, xb_ref, xn_ref, rt_ref, gt_ref, wu_ref, bu_ref, wd_ref, bd_ref,
                lg_ref, lb_ref, y_ref, yb_ref, acc_ref, *, alpha):
    i = pl.program_id(0)
    e = pl.program_id(1)
    tt = xb_ref.shape[0]

    @pl.when(e == 0)
    def _():
        acc_ref[...] = jnp.zeros_like(acc_ref)

    n_chunks = (cnt_ref[i, e] + MOE_CH - 1) // MOE_CH
    slot = rt_ref[0, pl.ds(e, 1), :]
    gate = gt_ref[0, pl.ds(e, 1), :]

    def chunk(c, _):
        want = (_iota((MOE_CH, tt), 0) + c * MOE_CH).astype(F32)
        hit = slot == want
        xe = _dot(jnp.where(hit, 1.0, 0.0).astype(BF16), xb_ref[...]).astype(BF16)
        h = _dot(xe, wu_ref[0]) + bu_ref[0]
        acts = []
        for blk in range(h.shape[1] // (2 * LANES)):
            h_glu = jnp.minimum(h[:, blk * 2 * LANES:blk * 2 * LANES + LANES], SWIGLU_LIMIT)
            h_lin = jnp.clip(h[:, blk * 2 * LANES + LANES:(blk + 1) * 2 * LANES], -SWIGLU_LIMIT, SWIGLU_LIMIT)
            acts.append((h_glu * jax.nn.sigmoid(SWIGLU_ALPHA * h_glu) * (h_lin + 1.0)).astype(BF16))
        y = _dot(jnp.concatenate(acts, axis=-1), wd_ref[0]) + bd_ref[0]
        back = jnp.where(hit, gate, 0.0).astype(BF16)
        acc_ref[...] += _dot_tn(back, y.astype(BF16))
        return 0

    lax.fori_loop(0, n_chunks, chunk, 0)

    @pl.when(e == pl.num_programs(1) - 1)
    def _():
        y = _layer_norm(alpha * xn_ref[...] + acc_ref[...], lg_ref[...], lb_ref[...])
        y_ref[...] = y
        yb_ref[...] = y.astype(BF16)


def moe_ln(cnt, xb, xn, rt, gt, w_up, b_up, w_down, b_down, ln_g, ln_b, tt, alpha):
    n = xn.shape[0]
    ff = w_down.shape[1]
    tile = pl.BlockSpec((tt, D_MODEL), lambda i, e, c: (i, 0))
    route = pl.BlockSpec((1, N_EXPERTS, tt), lambda i, e, c: (i, 0, 0))
    per_e = lambda shape: pl.BlockSpec((1,) + shape, lambda i, e, c: (e, 0, 0))
    const = pl.BlockSpec((1, D_MODEL), lambda i, e, c: (0, 0))
    grid_spec = pltpu.PrefetchScalarGridSpec(
        num_scalar_prefetch=1, grid=(n // tt, N_EXPERTS),
        in_specs=[tile, tile, route, route, per_e((D_MODEL, 2 * ff)), per_e((1, 2 * ff)),
                  per_e((ff, D_MODEL)), per_e((1, D_MODEL)), const, const],
        out_specs=[tile, tile],
        scratch_shapes=[pltpu.VMEM((tt, D_MODEL), F32)])
    return pl.pallas_call(
        functools.partial(_moe_kernel, alpha=alpha), grid_spec=grid_spec,
        out_shape=[jax.ShapeDtypeStruct((n, D_MODEL), F32), jax.ShapeDtypeStruct((n, D_MODEL), BF16)],
        compiler_params=_params(("arbitrary",) * 2), name="moe_ln")(
            cnt, xb, xn, rt, gt, w_up, b_up, w_down, b_down, ln_g, ln_b)


MOE_TILE = 1024


def _split_pairs_bias(b_up):
    e, f2 = b_up.shape
    b = b_up.astype(F32).reshape(e, f2 // (2 * LANES), LANES, 2).transpose(0, 1, 3, 2)
    return b.reshape(e, 1, f2)


def _front_pad(a, axis, rows=KPAD):
    pad = [(0, 0)] * a.ndim
    pad[axis] = (rows, 0)
    return jnp.pad(a, pad)


def _first_tile_bias(tab, kt):
    by_u = _hankel(_window(tab, -(QB - 1), QB + kt - 1), QB, kt)
    return jnp.flip(by_u, axis=2).transpose(0, 2, 1)


def _pad_rows(a, rows):
    return jnp.pad(a, ((0, 0), (0, rows - a.shape[1]), (0, 0)))


def _channel_mixer(layer, a1, a2, w_out, x, p, tt):
    alpha = p["alpha"]
    half = a1.shape[1]
    w_out = w_out.astype(BF16)
    xn, xb, rt, gt, cnt = mix_router(
        a1, a2, w_out[:half], w_out[half:], x, p["ln_mix_g"][layer][None], p["ln_mix_b"][layer][None],
        p["w_router"][layer].T.astype(F32), p["b_router"][layer][:, None].astype(F32), tt, alpha)
    return moe_ln(cnt[:, :, 0], xb, xn, rt, gt, *p["experts"][layer],
                  p["ln_ffn_g"][layer][None], p["ln_ffn_b"][layer][None], tt, alpha)


def _even_prompt(h, b, t, cmp_w, bias_c, bias0, c31):
    h3 = h.reshape(b, t, -1)
    heads = lambda a, nh: a.reshape(b, t, nh, HEAD_DIM).transpose(0, 2, 1, 3).astype(BF16)
    heads_t = lambda a, nh: a.reshape(b, t, nh, HEAD_DIM).transpose(0, 2, 3, 1).astype(BF16)
    o = 3 * SB_W
    o_sb = sb_prompt(heads(h3[..., :SB_W], SB_HEADS),
                     _front_pad(heads(h3[..., SB_W:2 * SB_W], SB_HEADS), 2),
                     _front_pad(heads_t(h3[..., 2 * SB_W:o], SB_HEADS), 3))
    o_sb = o_sb.transpose(0, 2, 1, 3).reshape(b * t, SB_W)
    nq = h3[..., o:o + NSA_Q_W].reshape(b, t, NSA_GROUPS, NSA_REP, HEAD_DIM).transpose(0, 2, 3, 1, 4)
    o += NSA_Q_W
    nkv = h3[..., o:o + 6 * NSA_KV_W].reshape(b, t, 6, NSA_GROUPS, HEAD_DIM)
    gates = h3[..., o + 6 * NSA_KV_W:o + 6 * NSA_KV_W + NSA_GATE_W]
    gates = gates.reshape(b, t, NSA_GROUPS, NSA_REP * 3).transpose(0, 2, 3, 1)
    gates = jnp.pad(gates, ((0, 0), (0, 0), (0, NSA_REP), (0, 0)))
    n_pages = t // PAGE
    table = jnp.arange(b * n_pages, dtype=jnp.int32).reshape(b, n_pages)
    kcvc = compress_pages(table, h.reshape(b * n_pages, 1, PAGE, h.shape[-1]), 0, o // (2 * HEAD_DIM), *cmp_w)
    nc = t // CMP_BLOCK
    ncp = -(-nc // LANES) * LANES
    kcvc = kcvc.reshape(b, nc, 2, NSA_GROUPS, HEAD_DIM).astype(BF16)
    kc = jnp.pad(kcvc[:, :, 0].transpose(0, 2, 1, 3), ((0, 0), (0, 0), (0, ncp - nc), (0, 0)))
    vct = jnp.pad(kcvc[:, :, 1].transpose(0, 2, 3, 1), ((0, 0), (0, 0), (0, 0), (0, ncp - nc)))
    grp = lambda i: _front_pad(nkv[:, :, i].transpose(0, 2, 1, 3).astype(BF16), 2)
    grp_t = lambda i: _front_pad(nkv[:, :, i].transpose(0, 2, 3, 1).astype(BF16), 3)
    o_nsa = nsa_prompt(nq.astype(BF16), kc, vct, grp(2), grp_t(3), grp(4), grp_t(5),
                       bias_c, bias0, c31, gates, -(-t // SEL_BLOCK))
    sb_rows = h3[..., SB_W:3 * SB_W].reshape(b, t, 2, SB_HEADS, HEAD_DIM)
    keep = min(WINDOW, t)
    return (o_sb, o_nsa.reshape(b * t, NSA_Q_W), sb_rows, nkv[:, :, :4], nkv[:, t - keep:, 4:])


def _even_sample(h, db, page_table, cache_sb, cache_nsa, win_state, layer, cmp_w, tabs):
    h3 = h.reshape(db, DEC_S, -1)
    past = page_table.shape[1] * PAGE
    o = 3 * SB_W
    q = h3[..., :SB_W].reshape(db, DEC_S, SB_HEADS, HEAD_DIM).transpose(0, 2, 1, 3) * ATT_SCALE
    eye = jnp.eye(SB_HEADS, dtype=F32)
    qbd = (q[:, :, :, None, :] * eye[None, :, None, :, None]).reshape(db, SB_HEADS * DEC_S, SB_W)
    o_sb = sb_sample(page_table, qbd.astype(BF16), _pad_rows(h3[..., SB_W:o], PAGE).astype(BF16),
                     cache_sb, layer)
    q = h3[..., o:o + NSA_Q_W].reshape(db, DEC_S, NSA_GROUPS, NSA_REP, HEAD_DIM).transpose(0, 3, 2, 1, 4)
    eye = jnp.eye(NSA_GROUPS, dtype=F32)
    qbd = (q[:, :, :, :, None, :] * ATT_SCALE * eye[None, None, :, None, :, None])
    qbd = qbd.reshape(db, NSA_ROWS, NSA_KV_W).astype(BF16)
    o += NSA_Q_W
    nkv = h3[..., o:o + 6 * NSA_KV_W]
    gates = h3[..., o + 6 * NSA_KV_W:o + 6 * NSA_KV_W + NSA_GATE_W]
    gates = gates.reshape(db, DEC_S, NSA_GROUPS, NSA_REP, 3).transpose(0, 3, 2, 1, 4).reshape(db, NSA_ROWS, 3)
    gates = jnp.pad(gates, ((0, 0), (0, 0), (0, LANES - 3)))
    kcvc = compress_pages(page_table, cache_nsa, layer, 0, *cmp_w, positions_on_lanes=True)
    win_len = win_state.shape[2]
    new_win = nkv[..., 4 * NSA_KV_W:]
    win_all = jnp.concatenate([win_state[:, layer].reshape(db, win_len, 2 * NSA_KV_W), new_win], axis=1)
    wl = tabs["bias_w"].shape[1]
    n_blocks = -(-(past + DEC_S) // SEL_BLOCK)
    o_c, o_w, sel = nsa_sample_cw(qbd, kcvc, tabs["bias_c"], _pad_rows(win_all, wl).astype(BF16),
                                  tabs["bias_w"], win_len, min(N_SEL, n_blocks) - 1)
    o_nsa = nsa_sample_sel(page_table, qbd, _pad_rows(nkv[..., 2 * NSA_KV_W:4 * NSA_KV_W], PAGE).astype(BF16),
                           sel, tabs["cvec"], tabs["blast"], tabs["bnew"], o_c, o_w, gates, cache_nsa, layer)
    sb_rows = h3[..., SB_W:3 * SB_W].reshape(db, DEC_S, 2, SB_HEADS, HEAD_DIM)
    nsa_rows = nkv[..., :4 * NSA_KV_W].reshape(db, DEC_S, 4, NSA_GROUPS, HEAD_DIM)
    win_rows = win_all[:, max(0, win_len + DEC_S - WINDOW):].reshape(db, -1, 2, NSA_GROUPS, HEAD_DIM)
    return (o_sb.reshape(db * DEC_S, SB_W), o_nsa.reshape(db * DEC_S, NSA_Q_W), sb_rows, nsa_rows, win_rows)


def _odd_prompt(h, b, t, bias0, c31, lam_rows, sub_g, lam_init):
    h3 = h.reshape(b, t, -1)
    q = h3[..., :DIFF_Q_W].reshape(b, t, DIFF_KV_HEADS, DIFF_REP, 2, HEAD_DIM).transpose(0, 2, 3, 4, 1, 5)
    kv = h3[..., DIFF_Q_W:].reshape(b, t, 2, DIFF_KV_HEADS, 2 * HEAD_DIM)
    k = kv[:, :, 0].reshape(b, t, DIFF_KV_HEADS, 2, HEAD_DIM).transpose(0, 2, 3, 1, 4)
    vt = kv[:, :, 1].transpose(0, 2, 3, 1)
    kpad = bias0.shape[1] - QB
    o = diff_prompt(q.astype(BF16), _front_pad(k.astype(BF16), 3, kpad), _front_pad(vt.astype(BF16), 3, kpad),
                    bias0, c31, lam_rows, sub_g.T, lam_init)
    return o.reshape(b * t, DIFF_Q_W), kv


def _odd_sample(h, db, page_table, cache, layer, tabs, lam_rows, sub_g, lam_init):
    h3 = h.reshape(db, DEC_S, -1)
    q = h3[..., :DIFF_Q_W].reshape(db, DEC_S, DIFF_KV_HEADS, DIFF_REP, 2, HEAD_DIM).transpose(0, 2, 3, 4, 1, 5)
    eye_c = jnp.eye(2, dtype=F32)
    qc = q[..., None, :] * ATT_SCALE * eye_c[None, None, None, :, None, :, None]
    qc = qc.reshape(db, DIFF_HEADS * 2 * DEC_S, 2 * HEAD_DIM).astype(BF16)
    kv = h3[..., DIFF_Q_W:].reshape(db, DEC_S, 2, DIFF_KV_HEADS, 2 * HEAD_DIM)
    new_kv = _pad_rows(kv.reshape(db, DEC_S * DIFF_ROW_STRIDE, 2 * HEAD_DIM), PAGE * DIFF_ROW_STRIDE)
    o = diff_sample(page_table, qc, new_kv, tabs["cvec"], tabs["blast"], tabs["bnew"], lam_rows, sub_g,
                    cache, layer, lam_init)
    return o.reshape(db * DEC_S, DIFF_Q_W), kv


def _sample_tables(tab, past, arrange, with_cmp):
    def table(first, n, step=1):
        rows = [jnp.flip(_window(tab, first(t) - step * (n - 1), n, step), axis=1) for t in range(DEC_S)]
        return arrange(jnp.stack(rows, axis=1))

    tabs = {
        "cvec": arrange(jnp.broadcast_to(tab[:, -1][:, None, None], (tab.shape[0], DEC_S, 1))),
        "blast": table(lambda t: t + PAGE, PAGE),
        "bnew": table(lambda t: t, PAGE),
    }
    if with_cmp:
        nc = past // CMP_BLOCK
        tabs["bias_c"] = table(lambda t: past + t - (CMP_BLOCK - 1), nc, CMP_BLOCK)
        win_len = min(WINDOW, past)
        wl = -(-(win_len + DEC_S) // LANES) * LANES
        tabs["bias_w"] = table(lambda t: t + win_len, wl)
    return tabs


def kernel(x_prompt, x_sample, cache_sb_kv, cache_nsa_kv, cache_diff_kv, state_nsa_win, page_table, rel_bias,
           even_w_in, even_cmp_pe, even_cmp_wk, even_cmp_wv, even_w_out, odd_w_in, odd_lambda, odd_subln_g,
           odd_w_out, ln_mix_g, ln_mix_b, ln_ffn_g, ln_ffn_b, moe_w_router, moe_b_router, moe_w_up, moe_b_up,
           moe_w_down, moe_b_down):
    b, t, d = x_prompt.shape
    db, s, _ = x_sample.shape
    depth = ln_mix_g.shape[0]
    n_pool = cache_sb_kv.shape[0]
    past = page_table.shape[1] * PAGE
    assert s == DEC_S and d == D_MODEL and t % QB == 0 and (b * t) % MOE_TILE == 0
    assert past % (SEL_BLOCK * LANES) == 0 and state_nsa_win.shape[2] == WINDOW

    params = {
        "alpha": (2.0 * depth) ** 0.25,
        "ln_mix_g": ln_mix_g.astype(F32), "ln_mix_b": ln_mix_b.astype(F32),
        "ln_ffn_g": ln_ffn_g.astype(F32), "ln_ffn_b": ln_ffn_b.astype(F32),
        "w_router": moe_w_router, "b_router": moe_b_router,
        "experts": [(split_pairs(moe_w_up[l]), _split_pairs_bias(moe_b_up[l]),
                     moe_w_down[l].astype(BF16), moe_b_down[l][:, None, :].astype(F32)) for l in range(depth)],
    }
    tab = _distance_table(rel_bias)
    c31 = tab[:, -1]
    bias0 = _first_tile_bias(tab, KT)
    bias0_diff = _first_tile_bias(tab, DIFF_KT)
    bias_c = _cmp_band_table(tab)
    nsa_arrange = lambda x: x.reshape(NSA_GROUPS, NSA_REP, DEC_S, -1).transpose(1, 0, 2, 3).reshape(NSA_ROWS, -1)
    diff_arrange = lambda x: jnp.broadcast_to(x[:, None], (DIFF_HEADS, 2) + x.shape[1:]).reshape(
        DIFF_HEADS * 2 * DEC_S, -1)
    nsa_tabs = _sample_tables(tab, past, nsa_arrange, True)
    diff_tabs = _sample_tables(tab, past, diff_arrange, False)

    cache_sb = _positions_on_lanes(cache_sb_kv)
    cache_nsa = _positions_on_lanes(cache_nsa_kv)
    cache_diff = cache_diff_kv.reshape(n_pool, -1, PAGE * DIFF_ROW_STRIDE, 2 * HEAD_DIM)

    xp = x_prompt.reshape(b * t, d).astype(F32)
    xs = x_sample.reshape(db * s, d).astype(F32)
    xpb, xsb = xp.astype(BF16), xs.astype(BF16)
    outs = {k: [] for k in ("sb_p", "sb_s", "nsa_p", "nsa_s", "win_p", "win_s", "diff_p", "diff_s")}
    for l in range(depth):
        j = l // 2
        if l % 2 == 0:
            w_in = jnp.pad(even_w_in[j], ((0, 0), (0, EVEN_IN_PAD - EVEN_IN))).astype(BF16)
            cmp_w = _compress_weights(even_cmp_pe[j], even_cmp_wk[j], even_cmp_wv[j])
            a1, a2, r_sb, r_nsa, r_win = _even_prompt(matmul(xpb, w_in, 512), b, t, cmp_w, bias_c, bias0, c31)
            s1, s2, s_sb, s_nsa, s_win = _even_sample(matmul(xsb, w_in, 512), db, page_table, cache_sb,
                                                      cache_nsa, state_nsa_win, j, cmp_w, nsa_tabs)
            w_out = even_w_out[j]
            for key, val in (("sb_p", r_sb), ("sb_s", s_sb), ("nsa_p", r_nsa), ("nsa_s", s_nsa),
                             ("win_p", r_win), ("win_s", s_win)):
                outs[key].append(val)
        else:
            lam_init = 0.8 - 0.6 * math.exp(-0.3 * l)
            w_in = odd_w_in[j].astype(BF16)
            lam_rows = odd_lambda[j].astype(F32)
            sub_g = odd_subln_g[j][None].astype(F32)
            ap, r_diff = _odd_prompt(matmul(xpb, w_in, 512), b, t, bias0_diff, c31, lam_rows, sub_g, lam_init)
            as_, s_diff = _odd_sample(matmul(xsb, w_in, 512), db, page_table, cache_diff, j, diff_tabs,
                                      lam_rows, sub_g, lam_init)
            half = DIFF_Q_W // 2
            a1, a2, s1, s2 = ap[:, :half], ap[:, half:], as_[:, :half], as_[:, half:]
            w_out = odd_w_out[j]
            outs["diff_p"].append(r_diff)
            outs["diff_s"].append(s_diff)
        xp, xpb = _channel_mixer(l, a1, a2, w_out, xp, params, MOE_TILE)
        xs, xsb = _channel_mixer(l, s1, s2, w_out, xs, params, db * s)
    stack = lambda key: jnp.stack(outs[key], axis=1)
    return (xp.reshape(b, t, d), xs.reshape(db, s, d), stack("sb_p"), stack("sb_s"), stack("nsa_p"),
            stack("nsa_s"), stack("diff_p"), stack("diff_s"), stack("win_p"), stack("win_s"))
```

```python
import functools
import math

import numpy as np
import jax
import jax.numpy as jnp
from jax import lax
from jax.experimental import pallas as pl
from jax.experimental.pallas import tpu as pltpu

F32 = jnp.float32
BF16 = jnp.bfloat16

D_MODEL = 1024
HEAD_DIM = 64
SB_HEADS = 8
NSA_HEADS = 8
NSA_GROUPS = 2
NSA_REP = NSA_HEADS // NSA_GROUPS
CMP_BLOCK = 32
SEL_BLOCK = 64
N_SEL = 16
WINDOW = 512
DIFF_HEADS = 8
DIFF_KV_HEADS = 4
DIFF_REP = DIFF_HEADS // DIFF_KV_HEADS
N_BUCKETS = 32
MAX_DISTANCE = 128
N_EXPERTS = 32
TOP_K = 4
SWIGLU_ALPHA = 1.702
SWIGLU_LIMIT = 7.0
LN_EPS = 1e-5
PAGE = 128
NEG = -1e30
FORCE = 1e9
ATT_SCALE = HEAD_DIM ** -0.5

SB_W = SB_HEADS * HEAD_DIM
NSA_Q_W = NSA_HEADS * HEAD_DIM
NSA_KV_W = NSA_GROUPS * HEAD_DIM
NSA_GATE_W = NSA_HEADS * 3
EVEN_IN = 3 * SB_W + NSA_Q_W + 6 * NSA_KV_W + NSA_GATE_W
EVEN_IN_PAD = -(-EVEN_IN // 128) * 128
DIFF_Q_W = DIFF_HEADS * 2 * HEAD_DIM
DIFF_KV_W = DIFF_KV_HEADS * 2 * HEAD_DIM
ODD_IN = DIFF_Q_W + 2 * DIFF_KV_W

LANES = 128
SUBLANES = 8
VMEM_LIMIT = 52 * 1024 * 1024

QB = 128
KT = 512
KPAD = KT - QB
DIFF_KT = 1024
MOE_CH = 144
MOE_GROUP = 8
DEC_S = 8


def _dot(a, b):
    return jnp.dot(a, b, preferred_element_type=F32)


def _dot_nt(a, b):
    return lax.dot_general(a, b, (((1,), (1,)), ((), ())), preferred_element_type=F32)


def _dot_tn(a, b):
    return lax.dot_general(a, b, (((0,), (0,)), ((), ())), preferred_element_type=F32)


def _iota(shape, dim):
    return lax.broadcasted_iota(jnp.int32, shape, dim)


def _params(sem, vmem=VMEM_LIMIT):
    return pltpu.CompilerParams(dimension_semantics=sem, vmem_limit_bytes=vmem)


def _bucket_np(dist):
    n = np.maximum(dist, 0)
    exact = N_BUCKETS // 2
    nf = np.maximum(n, 1).astype(np.float32)
    large = exact + (np.log(nf / np.float32(exact)) / np.float32(math.log(MAX_DISTANCE / exact))
                     * np.float32(N_BUCKETS - exact)).astype(np.int32)
    return np.where(n < exact, n, np.minimum(large, N_BUCKETS - 1)).astype(np.int32)


DIST_TABLE = 256


def _distance_table(rel_bias):
    onehot = np.eye(N_BUCKETS, dtype=np.float32)[_bucket_np(np.arange(DIST_TABLE))]
    return jnp.dot(jnp.asarray(onehot), rel_bias.astype(F32), precision=lax.Precision.HIGHEST).T


def _window(tab, start, n, step=1):
    stop = start + step * (n - 1)
    lo, hi = max(0, -start), max(0, stop + 1 - tab.shape[1])
    ext = jnp.pad(tab, ((0, 0), (lo, hi)), mode="edge")
    return ext[:, start + lo:stop + lo + 1:step]


def _hankel(v, rows, cols):
    n = rows + cols - 1
    flat = jnp.tile(v, (1, rows + 1))[:, :rows * (n + 1)]
    return flat.reshape(v.shape[0], rows, n + 1)[:, :, :cols]


def _mm_kernel(x_ref, w_ref, o_ref):
    o_ref[...] = _dot(x_ref[...], w_ref[...])


def matmul(x, w, tm):
    m, k = x.shape
    n = w.shape[1]
    tm = min(tm, m)
    return pl.pallas_call(
        _mm_kernel, grid=(m // tm,),
        in_specs=[pl.BlockSpec((tm, k), lambda i: (i, 0)), pl.BlockSpec((k, n), lambda i: (0, 0))],
        out_specs=pl.BlockSpec((tm, n), lambda i: (i, 0)),
        out_shape=jax.ShapeDtypeStruct((m, n), F32),
        compiler_params=_params(("arbitrary",)), name="in_proj")(x, w)


def _softmax_step(s, valid, m, l, acc, v, v_on_lanes=False):
    if valid is not None:
        s = jnp.where(valid, s, NEG)
    m_new = jnp.maximum(m, jnp.max(s, axis=-1, keepdims=True))
    p = jnp.exp(s - m_new)
    if valid is not None:
        p = jnp.where(valid, p, 0.0)
    alpha = jnp.exp(m - m_new)
    l = alpha * l + jnp.sum(p, axis=-1, keepdims=True)
    pv = _dot_nt(p.astype(BF16), v) if v_on_lanes else _dot(p.astype(BF16), v)
    return m_new, l, alpha * acc + pv


def _softmax_full(s, valid):
    if valid is not None:
        s = jnp.where(valid, s, NEG)
    m = jnp.max(s, axis=-1, keepdims=True)
    e = jnp.exp(s - m)
    if valid is not None:
        e = jnp.where(valid, e, 0.0)
    return e / jnp.maximum(jnp.sum(e, axis=-1, keepdims=True), 1e-30)


def _sb_step(z, valid, carry, acc, v, u, v_on_lanes=False):
    sp = jnp.maximum(z, 0.0) + jnp.log(1.0 + jnp.exp(-jnp.abs(z)))
    lk = -sp if valid is None else jnp.where(valid, -sp, 0.0)
    hi = lk.astype(BF16)
    lo = (lk - hi.astype(F32)).astype(BF16)
    later = _dot(hi, u) + _dot(lo, u) + carry
    w = jnp.exp(z - sp + later)
    if valid is not None:
        w = jnp.where(valid, w, 0.0)
    acc = acc + (_dot_nt(w.astype(BF16), v) if v_on_lanes else _dot(w.astype(BF16), v))
    carry = carry + jnp.sum(lk, axis=-1, keepdims=True)
    return carry, acc


def _softmax_step_t(s, valid, m, l, acc, vt):
    if valid is not None:
        s = jnp.where(valid, s, NEG)
    m_new = jnp.maximum(m, jnp.max(s, axis=0, keepdims=True))
    p = jnp.exp(s - m_new)
    if valid is not None:
        p = jnp.where(valid, p, 0.0)
    alpha = jnp.exp(m - m_new)
    l = alpha * l + jnp.sum(p, axis=0, keepdims=True)
    acc = alpha * acc + _dot(vt, p.astype(BF16))
    return m_new, l, acc


def _softmax_full_t(s, valid):
    s = jnp.where(valid, s, NEG)
    m = jnp.max(s, axis=0, keepdims=True)
    e = jnp.where(valid, jnp.exp(s - m), 0.0)
    return e / jnp.maximum(jnp.sum(e, axis=0, keepdims=True), 1e-30)


def _sb_step_t(z, valid, carry, acc, vt, lm):
    nq = z.shape[1]
    sp = jnp.maximum(z, 0.0) + jnp.log(1.0 + jnp.exp(-jnp.abs(z)))
    lk = jnp.where(valid, -sp, 0.0)
    hi = lk.astype(BF16)
    lo = (lk - hi.astype(F32)).astype(BF16)
    both = _dot(lm, jnp.concatenate([hi, lo], axis=1))
    later = both[:, :nq] + both[:, nq:] + carry
    w = jnp.where(valid, jnp.exp(z - sp + later), 0.0)
    acc = acc + _dot(vt, w.astype(BF16))
    carry = carry + jnp.sum(lk, axis=0, keepdims=True)
    return carry, acc


SB_DEAD = -104.0


def _later_matrix(n):
    j = np.arange(n)
    return jnp.asarray((j[:, None] > j[None, :]).astype(np.float32), dtype=BF16)


def _sb_prompt_kernel(q_ref, k_ref, vt_ref, lm_ref, o_ref):
    qi = pl.program_id(2)
    q0 = qi * QB
    q = (q_ref[0, 0].astype(F32) * ATT_SCALE).astype(BF16)
    lm = lm_ref[...]
    key = _iota((KT, QB), 0)
    qry = _iota((KT, QB), 1)

    def tile(j, carry, acc, first):
        start = pl.multiple_of(q0 - j * KT, QB)
        k = k_ref[0, 0, pl.ds(start, KT), :]
        vt = vt_ref[0, 0, :, pl.ds(start, KT)]
        z = _dot_nt(k, q)
        valid = key >= KPAD - start
        if first:
            valid = valid & (qry + KPAD - key > 0)
        return _sb_step_t(z, valid, carry, acc, vt, lm)

    carry, acc = tile(0, jnp.zeros((1, QB), F32), jnp.zeros((HEAD_DIM, QB), F32), True)
    n_tiles = qi // (KT // QB) + 1

    def more(state):
        j, carry, _ = state
        return (j < n_tiles) & (jnp.max(carry) > SB_DEAD)

    def step(state):
        j, carry, acc = state
        carry, acc = tile(j, carry, acc, False)
        return j + 1, carry, acc

    _, _, acc = lax.while_loop(more, step, (jnp.int32(1), carry, acc))
    o_ref[0, 0] = acc.T.astype(o_ref.dtype)


def sb_prompt(q, k, vt):
    b, h, t, _ = q.shape
    tp = k.shape[2]
    return pl.pallas_call(
        _sb_prompt_kernel, grid=(b, h, t // QB),
        in_specs=[pl.BlockSpec((1, 1, QB, HEAD_DIM), lambda b_, h_, i: (b_, h_, i, 0)),
                  pl.BlockSpec((1, 1, tp, HEAD_DIM), lambda b_, h_, i: (b_, h_, 0, 0)),
                  pl.BlockSpec((1, 1, HEAD_DIM, tp), lambda b_, h_, i: (b_, h_, 0, 0)),
                  pl.BlockSpec((KT, KT), lambda b_, h_, i: (0, 0))],
        out_specs=pl.BlockSpec((1, 1, QB, HEAD_DIM), lambda b_, h_, i: (b_, h_, i, 0)),
        out_shape=jax.ShapeDtypeStruct((b, h, t, HEAD_DIM), BF16),
        compiler_params=_params(("arbitrary",) * 3), name="sb_prompt")(q, k, vt, _later_matrix(KT).T)


def _diff_lambda(lam_ref, lam_init):
    lv = lam_ref[...].astype(F32)
    a = jnp.sum(lv[0:1] * lv[1:2], axis=-1, keepdims=True)
    b = jnp.sum(lv[2:3] * lv[3:4], axis=-1, keepdims=True)
    return jnp.exp(a) - jnp.exp(b) + lam_init


def _diff_finish(o1, l1, o2, l2, lam, sub_g, lam_init):
    a = o1 / jnp.maximum(l1, 1e-30) - lam * (o2 / jnp.maximum(l2, 1e-30))
    a = a * lax.rsqrt(jnp.mean(jnp.square(a), axis=-1, keepdims=True) + LN_EPS)
    return a * sub_g * (1.0 - lam_init)


def _diff_prompt_kernel(q_ref, k_ref, vt_ref, b0_ref, c31_ref, lam_ref, sg_ref, o_ref, *, lam_init):
    qi = pl.program_id(2)
    q0 = qi * QB
    w = DIFF_REP * QB
    kt = b0_ref.shape[1]
    kpad = kt - QB
    key = _iota((kt, w), 0)
    qry = _iota((kt, w), 1) % QB
    qcat = [jnp.concatenate([(q_ref[0, 0, r, c].astype(F32) * ATT_SCALE).astype(BF16)
                             for r in range(DIFF_REP)], axis=0) for c in range(2)]

    def tile(j, state, first, padded):
        start = pl.multiple_of(q0 - j * kt, QB)
        vt = vt_ref[0, 0, :, pl.ds(start, kt)]
        valid = (key >= kpad - start) if padded else None
        if first:
            valid = valid & (qry + kpad - key >= 0)
        out = []
        for c in range(2):
            k = k_ref[0, 0, c, pl.ds(start, kt), :]
            s = _dot_nt(k, qcat[c]) + (b0_ref[0] if first else c31_ref[0])
            out.append(_softmax_step_t(s, valid, *state[c], vt))
        return tuple(out)

    init = tuple((jnp.full((1, w), NEG, F32), jnp.zeros((1, w), F32),
                  jnp.zeros((2 * HEAD_DIM, w), F32)) for _ in range(2))
    state = tile(0, init, True, True)
    last = qi // (kt // QB)
    state = lax.fori_loop(1, last, lambda j, st: tile(j, st, False, False), state)
    state = lax.cond(last >= 1, lambda st: tile(last, st, False, True), lambda st: st, state)
    lam = _diff_lambda(lam_ref, lam_init)
    (_, l1, o1), (_, l2, o2) = state
    a = o1 / jnp.maximum(l1, 1e-30) - lam * (o2 / jnp.maximum(l2, 1e-30))
    a = a * lax.rsqrt(jnp.mean(jnp.square(a), axis=0, keepdims=True) + LN_EPS)
    a = a * sg_ref[...] * (1.0 - lam_init)
    o_ref[0] = jnp.concatenate([a[:, r * QB:(r + 1) * QB].T for r in range(DIFF_REP)],
                               axis=-1).astype(o_ref.dtype)


def _heads_on_lanes(table, group):
    h, k, q = table.shape
    return table.reshape(h // group, group, k, q).transpose(0, 2, 1, 3).reshape(h // group, k, group * q)


def diff_prompt(q, k, vt, bias0_t, c31, lam_rows, sub_g_col, lam_init):
    b, gk, _, _, t, _ = q.shape
    tp = k.shape[3]
    w = DIFF_REP * QB
    b0 = _heads_on_lanes(bias0_t, DIFF_REP)
    c31 = _heads_on_lanes(jnp.broadcast_to(c31[:, None, None], (DIFF_HEADS, 1, QB)), DIFF_REP)
    return pl.pallas_call(
        functools.partial(_diff_prompt_kernel, lam_init=lam_init), grid=(b, gk, t // QB),
        in_specs=[pl.BlockSpec((1, 1, DIFF_REP, 2, QB, HEAD_DIM), lambda b_, g_, i: (b_, g_, 0, 0, i, 0)),
                  pl.BlockSpec((1, 1, 2, tp, HEAD_DIM), lambda b_, g_, i: (b_, g_, 0, 0, 0)),
                  pl.BlockSpec((1, 1, 2 * HEAD_DIM, tp), lambda b_, g_, i: (b_, g_, 0, 0)),
                  pl.BlockSpec((1, b0.shape[1], w), lambda b_, g_, i: (g_, 0, 0)),
                  pl.BlockSpec((1, 1, w), lambda b_, g_, i: (g_, 0, 0)),
                  pl.BlockSpec((4, HEAD_DIM), lambda b_, g_, i: (0, 0)),
                  pl.BlockSpec((2 * HEAD_DIM, 1), lambda b_, g_, i: (0, 0))],
        out_specs=pl.BlockSpec((1, QB, DIFF_REP * 2 * HEAD_DIM), lambda b_, g_, i: (b_, i, g_)),
        out_shape=jax.ShapeDtypeStruct((b, t, DIFF_HEADS * 2 * HEAD_DIM), BF16),
        compiler_params=_params(("arbitrary",) * 3), name="diff_prompt")(
            q, k, vt, b0, c31, lam_rows, sub_g_col)


CMP_BAND_BACK = 4
CMP_BAND = 16


def _cmp_band_table(tab):
    first = [-((m - CMP_BAND_BACK) * CMP_BLOCK + CMP_BLOCK - 1) for m in range(CMP_BAND)]
    delta = jnp.stack([_window(tab, d0, QB) for d0 in first], axis=1) - tab[:, -1][:, None, None]
    hi = delta.astype(BF16)
    lo = (delta - hi.astype(F32)).astype(BF16)
    return jnp.stack([hi, lo], axis=1)


def _select_blocks(score_ref, n_rows, n_keep):
    sc = score_ref[...]
    blk = _iota(sc.shape, 0)

    def body(i, rank):
        r = score_ref[pl.ds(i, 1), :]
        better = (r > sc) | ((r == sc) & (i < blk))
        return rank + jnp.where(better, 1.0, 0.0)

    rank = lax.fori_loop(0, n_rows, body, jnp.zeros(sc.shape, F32), unroll=8)
    return rank < n_keep


def _nsa_prompt_kernel(q_ref, kc_ref, vct_ref, ks_ref, vst_ref, kw_ref, vwt_ref, bc_ref, b0_ref, c31_ref,
                       gate_ref, o_ref, impt_ref, score_ref, *, n_sel_blocks, n_keep):
    qi = pl.program_id(2)
    q0 = qi * QB
    n_tiles = qi // (KT // QB) + 1
    nc = kc_ref.shape[2]
    ns = nc // 2
    heads = range(NSA_REP)
    w = NSA_REP * QB
    qcat = jnp.concatenate([(q_ref[0, 0, r].astype(F32) * ATT_SCALE).astype(BF16) for r in heads], axis=0)
    on_lanes = lambda rows: jnp.concatenate(rows, axis=1)

    valid_c = _iota((nc, w), 1) % QB + q0 - (_iota((nc, w), 0) * CMP_BLOCK + (CMP_BLOCK - 1)) >= 0
    first_blk = qi * (QB // CMP_BLOCK) - CMP_BAND_BACK
    place = jnp.where(_iota((nc, CMP_BAND), 0) == _iota((nc, CMP_BAND), 1) + first_blk, 1.0, 0.0).astype(BF16)
    bias = c31_ref[0] + _dot(place, bc_ref[0, 0]) + _dot(place, bc_ref[0, 1])
    p = _softmax_full_t(_dot_nt(kc_ref[0, 0], qcat) + bias, valid_c)
    o_c = _dot(vct_ref[0, 0], p.astype(BF16))
    imp = p[:, :QB]
    for r in range(1, NSA_REP):
        imp = imp + p[:, r * QB:(r + 1) * QB]
    impt_ref[...] = imp
    imp_s = impt_ref[pl.ds(0, ns, stride=2), :] + impt_ref[pl.ds(1, ns, stride=2), :]
    blk = _iota((ns, QB), 0)
    qpos = _iota((ns, QB), 1) + q0
    forced = (blk == qpos // SEL_BLOCK) | (blk == 0)
    future = blk * SEL_BLOCK > qpos
    score = jnp.where(forced, FORCE, jnp.where(future, -FORCE, imp_s))
    score_ref[...] = jnp.where(blk < n_sel_blocks, score, -3e38)
    sel_t = jnp.where(_select_blocks(score_ref, ns, n_keep), 1.0, 0.0).astype(BF16)

    key = _iota((KT, w), 0)
    qry = _iota((KT, w), 1) % QB
    causal0 = qry + KPAD - key >= 0
    key_blk = _iota((KT, ns), 0) // SEL_BLOCK
    blk_e = _iota((KT, ns), 1)

    def branch(k_ref, vt_ref, hi, valid_fn):
        def tile(j, state, first):
            start = pl.multiple_of(q0 - j * KT, QB)
            k = k_ref[0, 0, pl.ds(start, KT), :]
            vt = vt_ref[0, 0, :, pl.ds(start, KT)]
            s = _dot_nt(k, qcat) + (b0_ref[0] if first else c31_ref[0])
            return _softmax_step_t(s, valid_fn(j, start, first), *state, vt)

        init = (jnp.full((1, w), NEG, F32), jnp.zeros((1, w), F32), jnp.zeros((HEAD_DIM, w), F32))
        state = tile(0, init, True)
        _, l, acc = lax.fori_loop(1, hi, lambda j, st: tile(j, st, False), state)
        return acc / jnp.maximum(l, 1e-30)

    def valid_sel(j, start, first):
        base = start // SEL_BLOCK - KPAD // SEL_BLOCK
        expand = jnp.where(blk_e == key_blk + base, 1.0, 0.0).astype(BF16)
        chosen = on_lanes([_dot(expand, sel_t)] * NSA_REP) > 0.5
        return (chosen & causal0) if first else chosen

    def valid_win(j, start, first):
        ok = key >= KPAD - start
        if first:
            return ok & causal0
        return ok & (qry + KPAD - key + j * KT < WINDOW)

    o_s = branch(ks_ref, vst_ref, n_tiles, valid_sel)
    o_w = branch(kw_ref, vwt_ref, jnp.minimum(n_tiles, (WINDOW + QB - 1) // KT + 1), valid_win)

    gate = jax.nn.sigmoid(gate_ref[0, 0].astype(F32))
    g0, g1, g2 = (on_lanes([gate[3 * r + i:3 * r + i + 1, :] for r in heads]) for i in range(3))
    o = g0 * o_c + g1 * o_s + g2 * o_w
    o_ref[0] = jnp.concatenate([o[:, r * QB:(r + 1) * QB] for r in heads], axis=0).T.astype(o_ref.dtype)


def nsa_prompt(q, kc, vct, ks, vst, kw, vwt, bias_c, bias0_t, c31, gates_t, n_sel_blocks):
    b, g, _, t, _ = q.shape
    tp = ks.shape[2]
    nc = kc.shape[2]
    k_spec = pl.BlockSpec((1, 1, tp, HEAD_DIM), lambda b_, g_, i: (b_, g_, 0, 0))
    vt_spec = pl.BlockSpec((1, 1, HEAD_DIM, tp), lambda b_, g_, i: (b_, g_, 0, 0))
    kern = functools.partial(_nsa_prompt_kernel, n_sel_blocks=n_sel_blocks,
                             n_keep=min(N_SEL, n_sel_blocks))
    w = NSA_REP * QB
    b0 = _heads_on_lanes(bias0_t, NSA_REP)
    c31 = _heads_on_lanes(jnp.broadcast_to(c31[:, None, None], (NSA_HEADS, 1, QB)), NSA_REP)
    band = _heads_on_lanes(bias_c.reshape(NSA_HEADS, 2 * CMP_BAND, QB), NSA_REP).reshape(g, 2, CMP_BAND, w)
    return pl.pallas_call(
        kern, grid=(b, g, t // QB),
        in_specs=[pl.BlockSpec((1, 1, NSA_REP, QB, HEAD_DIM), lambda b_, g_, i: (b_, g_, 0, i, 0)),
                  pl.BlockSpec((1, 1, nc, HEAD_DIM), lambda b_, g_, i: (b_, g_, 0, 0)),
                  pl.BlockSpec((1, 1, HEAD_DIM, nc), lambda b_, g_, i: (b_, g_, 0, 0)),
                  k_spec, vt_spec, k_spec, vt_spec,
                  pl.BlockSpec((1, 2, CMP_BAND, w), lambda b_, g_, i: (g_, 0, 0, 0)),
                  pl.BlockSpec((1, KT, w), lambda b_, g_, i: (g_, 0, 0)),
                  pl.BlockSpec((1, 1, w), lambda b_, g_, i: (g_, 0, 0)),
                  pl.BlockSpec((1, 1, 4 * NSA_REP, QB), lambda b_, g_, i: (b_, g_, 0, i))],
        out_specs=pl.BlockSpec((1, QB, NSA_REP * HEAD_DIM), lambda b_, g_, i: (b_, i, g_)),
        out_shape=jax.ShapeDtypeStruct((b, t, NSA_HEADS * HEAD_DIM), BF16),
        scratch_shapes=[pltpu.VMEM((nc, QB), F32), pltpu.VMEM((nc // 2, QB), F32)],
        compiler_params=_params(("arbitrary",) * 3), name="nsa_prompt")(
            q, kc, vct, ks, vst, kw, vwt, band, b0, c31, gates_t)


CMP_PP = 16
CMP_HALF = CMP_BLOCK // 2


def _compress_kernel(pt_ref, w_ref, pe_ref, *rest, pp, positions_on_lanes):
    pages, o_ref, lo_ref, hi_ref = rest[:pp], rest[pp], rest[pp + 1], rest[pp + 2]
    if positions_on_lanes:
        stage_ref = rest[pp + 3]
        for p in range(pp):
            stage_ref[p] = pages[p][...].T
        pages = [stage_ref.at[p] for p in range(pp)]
    width = 2 * HEAD_DIM
    acc = jnp.zeros((pp * SUBLANES, 2 * width), F32)
    for i in range(CMP_HALF):
        rows = [pages[p][pl.ds(i, SUBLANES, stride=CMP_HALF), :] + pe_ref[i] for p in range(pp)]
        acc = acc + _dot(jnp.concatenate(rows, axis=0).astype(BF16), w_ref[0, i])
    lo_ref[...] = acc[:, :width]
    hi_ref[...] = acc[:, width:]
    n = pp * SUBLANES // 2
    o_ref[0] = lo_ref[pl.ds(0, n, stride=2), :] + hi_ref[pl.ds(1, n, stride=2), :]


def compress_pages(page_table, src, layer, block, w_cat, pe_tiles, positions_on_lanes=False):
    nseq, n_pages = page_table.shape
    pp = min(CMP_PP, n_pages)
    width = 2 * HEAD_DIM

    def page_spec(p):
        def index(b, s, kv, pt):
            where = (block + kv, 0) if positions_on_lanes else (0, block + kv)
            return (pt[b, s * pp + p], layer) + where
        return pl.BlockSpec((None, None, PAGE, width), index)

    scratch = [pltpu.VMEM((pp * SUBLANES, width), F32), pltpu.VMEM((pp * SUBLANES, width), F32)]
    if positions_on_lanes:
        scratch.append(pltpu.VMEM((pp, PAGE, width), F32))
    grid_spec = pltpu.PrefetchScalarGridSpec(
        num_scalar_prefetch=1, grid=(nseq, n_pages // pp, 2),
        in_specs=[pl.BlockSpec((1, CMP_HALF, width, 2 * width), lambda b, s, kv, pt: (kv, 0, 0, 0)),
                  pl.BlockSpec((CMP_HALF, SUBLANES, width), lambda b, s, kv, pt: (0, 0, 0))]
        + [page_spec(p) for p in range(pp)],
        out_specs=pl.BlockSpec((1, pp * 4, width), lambda b, s, kv, pt: (b, s, kv)),
        scratch_shapes=scratch)
    return pl.pallas_call(
        functools.partial(_compress_kernel, pp=pp, positions_on_lanes=positions_on_lanes), grid_spec=grid_spec,
        out_shape=jax.ShapeDtypeStruct((nseq, n_pages * 4, 2 * width), F32),
        compiler_params=_params(("arbitrary",) * 3), name="compress")(
            page_table, w_cat, pe_tiles, *([src] * pp))


def _compress_weights(pe, wk, wv):
    def cat(w):
        w = w.reshape(CMP_BLOCK, HEAD_DIM, HEAD_DIM)
        z = jnp.zeros_like(w)
        full = jnp.concatenate([jnp.concatenate([w, z], -1), jnp.concatenate([z, w], -1)], axis=1)
        return jnp.concatenate([full[:CMP_HALF], full[CMP_HALF:]], axis=-1)

    w_cat = jnp.stack([cat(wk), cat(wv)]).astype(BF16)
    pe2 = jnp.tile(pe.astype(F32), (1, 2))
    pe_tiles = jnp.stack([pe2[:CMP_HALF], pe2[CMP_HALF:]], axis=1)
    pe_tiles = jnp.tile(pe_tiles, (1, SUBLANES // 2, 1))
    return w_cat, pe_tiles


SB_PP = 8
DIFF_PP = 8
SEL_PP = 16


def _page_specs(pp, n_pages, rows, row_block, layer):
    def spec(p):
        def index(b, s, pt):
            return (pt[b, n_pages - 1 - (s * pp + p)], layer, row_block, 0)
        return pl.BlockSpec((None, None, rows, LANES), index)
    return [spec(p) for p in range(pp)]


def _positions_on_lanes(cache):
    pool, layers = cache.shape[:2]
    return jnp.transpose(cache, (0, 1, 3, 4, 5, 2)).reshape(pool, layers, -1, PAGE)


def _token_of_row(shape):
    return _iota(shape, 0) % DEC_S


def _sb_sample_kernel(pt_ref, q_ref, new_ref, u_ref, cin_ref, ain_ref, *rest, pp, first):
    pages, (o_ref, cout_ref, aout_ref, carry_ref, acc_ref) = rest[:pp], rest[pp:]
    s = pl.program_id(1)
    q = q_ref[0]
    u = u_ref[...]
    rows = SB_HEADS * DEC_S

    @pl.when(s == 0)
    def _():
        if first:
            k = new_ref[0, :, :SB_W]
            v = new_ref[0, :, SB_W:]
            valid = _iota((rows, PAGE), 1) < _token_of_row((rows, PAGE))
            carry, acc = _sb_step(_dot_nt(q, k), valid, jnp.zeros((rows, 1), F32),
                                  jnp.zeros((rows, SB_W), F32), v, u)
        else:
            carry, acc = cin_ref[0], ain_ref[0]
        carry_ref[...] = carry
        acc_ref[...] = acc

    for p in range(pp):
        @pl.when(jnp.max(carry_ref[...]) > SB_DEAD)
        def _(p=p):
            kt = pages[p][:SB_W, :].astype(BF16)
            vt = pages[p][SB_W:, :].astype(BF16)
            carry, acc = _sb_step(_dot(q, kt), None, carry_ref[...], acc_ref[...], vt, u, v_on_lanes=True)
            carry_ref[...] = carry
            acc_ref[...] = acc

    @pl.when(s == pl.num_programs(1) - 1)
    def _():
        a = acc_ref[...]
        o_ref[0] = jnp.concatenate(
            [a[h * DEC_S:(h + 1) * DEC_S, h * HEAD_DIM:(h + 1) * HEAD_DIM] for h in range(SB_HEADS)],
            axis=-1).astype(o_ref.dtype)
        cout_ref[0] = carry_ref[...]
        aout_ref[0] = a


def _sb_sample_call(page_table, qbd, new_kv, cache, layer, carry, acc, skip, count, first):
    db = page_table.shape[0]
    pp = min(SB_PP, count)
    rows = SB_HEADS * DEC_S
    per_seq = lambda shape: pl.BlockSpec((1,) + shape, lambda b, s, pt: (b,) + (0,) * len(shape))
    grid_spec = pltpu.PrefetchScalarGridSpec(
        num_scalar_prefetch=1, grid=(db, count // pp),
        in_specs=[per_seq((rows, SB_W)), per_seq((PAGE, 2 * SB_W)),
                  pl.BlockSpec((PAGE, PAGE), lambda b, s, pt: (0, 0)), per_seq((rows, 1)), per_seq((rows, SB_W))]
        + _page_specs(pp, page_table.shape[1] - skip, 2 * SB_W, 0, layer),
        out_specs=[per_seq((DEC_S, SB_W)), per_seq((rows, 1)), per_seq((rows, SB_W))],
        scratch_shapes=[pltpu.VMEM((rows, 1), F32), pltpu.VMEM((rows, SB_W), F32)])
    return pl.pallas_call(
        functools.partial(_sb_sample_kernel, pp=pp, first=first), grid_spec=grid_spec,
        out_shape=[jax.ShapeDtypeStruct((db, DEC_S, SB_W), BF16), jax.ShapeDtypeStruct((db, rows, 1), F32),
                   jax.ShapeDtypeStruct((db, rows, SB_W), F32)],
        compiler_params=_params(("arbitrary",) * 2), name="sb_sample")(
            page_table, qbd, new_kv, _later_matrix(PAGE), carry, acc, *([cache] * pp))


def sb_sample(page_table, qbd, new_kv, cache, layer):
    db, n_pages = page_table.shape
    rows = SB_HEADS * DEC_S
    head = min(SB_PP, n_pages)
    zeros = (jnp.zeros((db, rows, 1), F32), jnp.zeros((db, rows, SB_W), F32))
    out, carry, acc = _sb_sample_call(page_table, qbd, new_kv, cache, layer, *zeros, 0, head, True)
    if n_pages == head:
        return out
    earlier = lambda: _sb_sample_call(page_table, qbd, new_kv, cache, layer, carry, acc, head,
                                      n_pages - head, False)[0]
    return lax.cond(jnp.max(carry) > SB_DEAD, earlier, lambda: out)


DIFF_GROUP_ROWS = DIFF_REP * 2 * DEC_S
DIFF_ROW_STRIDE = 2 * DIFF_KV_HEADS


def _diff_sample_tiles(q, loads, bias, valid, m, l, acc):
    heads = range(DIFF_KV_HEADS)
    rows = lambda g: slice(g * DIFF_GROUP_ROWS, (g + 1) * DIFF_GROUP_ROWS)
    s = jnp.concatenate(
        [jnp.concatenate([_dot_nt(q[rows(g)], load(g).astype(BF16)) for g in heads], axis=0)
         for load in loads], axis=1) + bias
    if valid is not None:
        s = jnp.where(valid, s, NEG)
    m_new = jnp.maximum(m, jnp.max(s, axis=-1, keepdims=True))
    p = jnp.exp(s - m_new)
    if valid is not None:
        p = jnp.where(valid, p, 0.0)
    alpha = jnp.exp(m - m_new)
    l = alpha * l + jnp.sum(p, axis=-1, keepdims=True)
    p = p.astype(BF16)
    pv = []
    for g in heads:
        terms = [_dot(p[rows(g), i * PAGE:(i + 1) * PAGE], load(DIFF_KV_HEADS + g).astype(BF16))
                 for i, load in enumerate(loads)]
        pv.append(functools.reduce(lambda a, b: a + b, terms))
    return m_new, l, alpha * acc + jnp.concatenate(pv, axis=0)


def _diff_sample_kernel(pt_ref, q_ref, new_ref, cvec_ref, blast_ref, bnew_ref, lam_ref, sg_ref, *rest,
                        pp, lam_init):
    pages, o_ref, m_ref, l_ref, acc_ref = rest[:pp], rest[pp], rest[pp + 1], rest[pp + 2], rest[pp + 3]
    s = pl.program_id(1)
    q = q_ref[0]
    rows = DIFF_HEADS * 2 * DEC_S
    strided = lambda ref: (lambda off: ref[pl.ds(off, PAGE, stride=DIFF_ROW_STRIDE), :])

    @pl.when(s == 0)
    def _():
        valid = _iota((rows, PAGE), 1) <= _token_of_row((rows, PAGE))
        m, l, acc = _diff_sample_tiles(q, [strided(new_ref)], bnew_ref[...], valid,
                                       jnp.full((rows, 1), NEG, F32), jnp.zeros((rows, 1), F32),
                                       jnp.zeros((rows, 2 * HEAD_DIM), F32))
        m_ref[...] = m
        l_ref[...] = l
        acc_ref[...] = acc

    bias = jnp.broadcast_to(cvec_ref[...], (rows, PAGE))
    bias = jnp.concatenate([jnp.where(s == 0, blast_ref[...], bias)] + [bias] * (pp - 1), axis=1)
    m, l, acc = _diff_sample_tiles(q, [strided(ref) for ref in pages], bias, None,
                                   m_ref[...], l_ref[...], acc_ref[...])
    m_ref[...] = m
    l_ref[...] = l
    acc_ref[...] = acc

    @pl.when(s == pl.num_programs(1) - 1)
    def _():
        a = acc_ref[...]
        ls = l_ref[...]
        lam = _diff_lambda(lam_ref, lam_init)
        outs = []
        for head in range(DIFF_HEADS):
            r0 = head * 2 * DEC_S
            outs.append(_diff_finish(a[r0:r0 + DEC_S], ls[r0:r0 + DEC_S], a[r0 + DEC_S:r0 + 2 * DEC_S],
                                     ls[r0 + DEC_S:r0 + 2 * DEC_S], lam, sg_ref[...], lam_init))
        o_ref[0] = jnp.concatenate(outs, axis=-1).astype(o_ref.dtype)


def diff_sample(page_table, qc, new_kv, cvec, blast, bnew, lam_rows, sub_g, cache, layer, lam_init):
    db, n_pages = page_table.shape
    pp = min(DIFF_PP, n_pages)
    rows = DIFF_HEADS * 2 * DEC_S
    page_rows = PAGE * DIFF_ROW_STRIDE
    const = lambda shape: pl.BlockSpec(shape, lambda b, s, pt: (0,) * len(shape))
    grid_spec = pltpu.PrefetchScalarGridSpec(
        num_scalar_prefetch=1, grid=(db, n_pages // pp),
        in_specs=[pl.BlockSpec((1, rows, 2 * HEAD_DIM), lambda b, s, pt: (b, 0, 0)),
                  pl.BlockSpec((None, page_rows, LANES), lambda b, s, pt: (b, 0, 0)),
                  const((rows, 1)), const((rows, PAGE)), const((rows, PAGE)),
                  const((4, HEAD_DIM)), const((1, 2 * HEAD_DIM))]
        + _page_specs(pp, n_pages, page_rows, 0, layer),
        out_specs=pl.BlockSpec((1, DEC_S, DIFF_HEADS * 2 * HEAD_DIM), lambda b, s, pt: (b, 0, 0)),
        scratch_shapes=[pltpu.VMEM((rows, 1), F32), pltpu.VMEM((rows, 1), F32),
                        pltpu.VMEM((rows, 2 * HEAD_DIM), F32)])
    return pl.pallas_call(
        functools.partial(_diff_sample_kernel, pp=pp, lam_init=lam_init), grid_spec=grid_spec,
        out_shape=jax.ShapeDtypeStruct((db, DEC_S, DIFF_HEADS * 2 * HEAD_DIM), BF16),
        compiler_params=_params(("arbitrary",) * 2), name="diff_sample")(
            page_table, qc, new_kv, cvec, blast, bnew, lam_rows, sub_g, *([cache] * pp))


NSA_ROWS = NSA_HEADS * DEC_S
NSA_GT = NSA_GROUPS * DEC_S


def _nsa_diag(a):
    pieces = []
    for r in range(NSA_REP):
        for g in range(NSA_GROUPS):
            r0 = r * NSA_GT + g * DEC_S
            pieces.append(a[r0:r0 + DEC_S, g * HEAD_DIM:(g + 1) * HEAD_DIM])
    return jnp.concatenate(pieces, axis=0)


def _nsa_sample_cw_kernel(q_ref, kcvc_ref, bc_ref, win_ref, bw_ref, oc_ref, ow_ref, sel_ref,
                          impt_ref, score_ref, *, win_len, n_keep):
    q = q_ref[0]
    nc = kcvc_ref.shape[1]
    ns = nc // 2
    kc = kcvc_ref[0, :, :NSA_KV_W].astype(BF16)
    vc = kcvc_ref[0, :, NSA_KV_W:].astype(BF16)
    p = _softmax_full(_dot_nt(q, kc) + bc_ref[...], None)
    oc_ref[0] = _nsa_diag(_dot(p.astype(BF16), vc))
    imp = p[0:NSA_GT]
    for r in range(1, NSA_REP):
        imp = imp + p[r * NSA_GT:(r + 1) * NSA_GT]
    imp = jnp.concatenate([imp, jnp.zeros((LANES - NSA_GT, nc), F32)], axis=0)
    impt_ref[...] = imp.T
    imp_s = impt_ref[pl.ds(0, ns, stride=2), :] + impt_ref[pl.ds(1, ns, stride=2), :]
    blk = _iota((ns, LANES), 0)
    score_ref[...] = jnp.where(blk == 0, FORCE, imp_s)
    sel_t = jnp.where(_select_blocks(score_ref, ns, n_keep), 1.0, 0.0)
    sel_ref[0] = sel_t.T[:NSA_GT].astype(sel_ref.dtype)

    wl = win_ref.shape[1]
    kw = win_ref[0, :, :NSA_KV_W]
    vw = win_ref[0, :, NSA_KV_W:]
    tok = _token_of_row((NSA_ROWS, wl))
    col = _iota((NSA_ROWS, wl), 1)
    valid = (col > tok + (win_len - WINDOW)) & (col <= tok + win_len) & (col < win_len + DEC_S)
    pw = _softmax_full(_dot_nt(q, kw) + bw_ref[...], valid)
    ow_ref[0] = _nsa_diag(_dot(pw.astype(BF16), vw))


def nsa_sample_cw(qbd, kcvc, bias_c, win_all, bias_w, win_len, n_keep):
    db = qbd.shape[0]
    nc = kcvc.shape[1]
    wl = win_all.shape[1]
    const = lambda shape: pl.BlockSpec(shape, lambda b: (0,) * len(shape))
    return pl.pallas_call(
        functools.partial(_nsa_sample_cw_kernel, win_len=win_len, n_keep=n_keep), grid=(db,),
        in_specs=[pl.BlockSpec((1, NSA_ROWS, NSA_KV_W), lambda b: (b, 0, 0)),
                  pl.BlockSpec((1, nc, 2 * NSA_KV_W), lambda b: (b, 0, 0)),
                  const((NSA_ROWS, nc)),
                  pl.BlockSpec((1, wl, 2 * NSA_KV_W), lambda b: (b, 0, 0)),
                  const((NSA_ROWS, wl))],
        out_specs=[pl.BlockSpec((1, NSA_ROWS, HEAD_DIM), lambda b: (b, 0, 0)),
                   pl.BlockSpec((1, NSA_ROWS, HEAD_DIM), lambda b: (b, 0, 0)),
                   pl.BlockSpec((1, NSA_GT, nc // 2), lambda b: (b, 0, 0))],
        out_shape=[jax.ShapeDtypeStruct((db, NSA_ROWS, HEAD_DIM), F32),
                   jax.ShapeDtypeStruct((db, NSA_ROWS, HEAD_DIM), F32),
                   jax.ShapeDtypeStruct((db, NSA_GT, nc // 2), BF16)],
        scratch_shapes=[pltpu.VMEM((nc, LANES), F32), pltpu.VMEM((nc // 2, LANES), F32)],
        compiler_params=_params(("arbitrary",)), name="nsa_sample_cw")(qbd, kcvc, bias_c, win_all, bias_w)


def _nsa_sample_sel_kernel(pt_ref, q_ref, new_ref, sel_ref, cvec_ref, blast_ref, bnew_ref, oc_ref, ow_ref,
                           gate_ref, *rest, pp, n_pages):
    pages, o_ref, m_ref, l_ref, acc_ref = rest[:pp], rest[pp], rest[pp + 1], rest[pp + 2], rest[pp + 3]
    s = pl.program_id(1)
    q = q_ref[0]
    sel = sel_ref[0]
    ns = sel.shape[1]

    @pl.when(s == 0)
    def _():
        k = new_ref[0, :, :NSA_KV_W]
        v = new_ref[0, :, NSA_KV_W:]
        valid = _iota((NSA_ROWS, PAGE), 1) <= _token_of_row((NSA_ROWS, PAGE))
        m, l, acc = _softmax_step(_dot_nt(q, k) + bnew_ref[...], valid, jnp.full((NSA_ROWS, 1), NEG, F32),
                                  jnp.zeros((NSA_ROWS, 1), F32), jnp.zeros((NSA_ROWS, NSA_KV_W), F32), v)
        m_ref[...] = m
        l_ref[...] = l
        acc_ref[...] = acc

    latest = n_pages - 1 - s * pp
    lane = _iota((ns, pp * PAGE), 1)
    blk_of_lane = (latest - lane // PAGE) * (PAGE // SEL_BLOCK) + (lane % PAGE) // SEL_BLOCK
    expand = jnp.where(_iota((ns, pp * PAGE), 0) == blk_of_lane, 1.0, 0.0).astype(BF16)
    valid = jnp.concatenate([_dot(sel, expand)] * NSA_REP, axis=0) > 0.5
    bias = jnp.broadcast_to(cvec_ref[...], (NSA_ROWS, PAGE))
    bias = jnp.concatenate([jnp.where(s == 0, blast_ref[...], bias)] + [bias] * (pp - 1), axis=1)
    scores = jnp.concatenate([_dot(q, pages[p][:NSA_KV_W, :].astype(BF16)) for p in range(pp)], axis=1)
    scores = jnp.where(valid, scores + bias, NEG)
    m = m_ref[...]
    m_new = jnp.maximum(m, jnp.max(scores, axis=-1, keepdims=True))
    prob = jnp.where(valid, jnp.exp(scores - m_new), 0.0)
    alpha = jnp.exp(m - m_new)
    m_ref[...] = m_new
    l_ref[...] = alpha * l_ref[...] + jnp.sum(prob, axis=-1, keepdims=True)
    prob = prob.astype(BF16)
    pv = [_dot_nt(prob[:, p * PAGE:(p + 1) * PAGE], pages[p][NSA_KV_W:, :].astype(BF16)) for p in range(pp)]
    acc_ref[...] = alpha * acc_ref[...] + functools.reduce(lambda a, b: a + b, pv)

    @pl.when(s == pl.num_programs(1) - 1)
    def _():
        o_s = _nsa_diag(acc_ref[...] / jnp.maximum(l_ref[...], 1e-30))
        gate = jax.nn.sigmoid(gate_ref[0].astype(F32))
        o = gate[:, 0:1] * oc_ref[0] + gate[:, 1:2] * o_s + gate[:, 2:3] * ow_ref[0]
        pieces = []
        for g in range(NSA_GROUPS):
            for r in range(NSA_REP):
                r0 = r * NSA_GT + g * DEC_S
                pieces.append(o[r0:r0 + DEC_S])
        o_ref[0] = jnp.concatenate(pieces, axis=-1).astype(o_ref.dtype)


def nsa_sample_sel(page_table, qbd, new_kv, sel, cvec, blast, bnew, o_c, o_w, gates, cache, layer):
    db, n_pages = page_table.shape
    pp = min(SEL_PP, n_pages)
    ns = sel.shape[2]
    const = lambda shape: pl.BlockSpec(shape, lambda b, s, pt: (0,) * len(shape))
    per_seq = lambda shape: pl.BlockSpec((1,) + shape, lambda b, s, pt: (b,) + (0,) * len(shape))
    grid_spec = pltpu.PrefetchScalarGridSpec(
        num_scalar_prefetch=1, grid=(db, n_pages // pp),
        in_specs=[per_seq((NSA_ROWS, NSA_KV_W)), per_seq((PAGE, 2 * NSA_KV_W)), per_seq((NSA_GT, ns)),
                  const((NSA_ROWS, 1)), const((NSA_ROWS, PAGE)), const((NSA_ROWS, PAGE)),
                  per_seq((NSA_ROWS, HEAD_DIM)), per_seq((NSA_ROWS, HEAD_DIM)), per_seq((NSA_ROWS, LANES))]
        + _page_specs(pp, n_pages, 2 * NSA_KV_W, 1, layer),
        out_specs=pl.BlockSpec((1, DEC_S, NSA_HEADS * HEAD_DIM), lambda b, s, pt: (b, 0, 0)),
        scratch_shapes=[pltpu.VMEM((NSA_ROWS, 1), F32), pltpu.VMEM((NSA_ROWS, 1), F32),
                        pltpu.VMEM((NSA_ROWS, NSA_KV_W), F32)])
    return pl.pallas_call(
        functools.partial(_nsa_sample_sel_kernel, pp=pp, n_pages=n_pages), grid_spec=grid_spec,
        out_shape=jax.ShapeDtypeStruct((db, DEC_S, NSA_HEADS * HEAD_DIM), BF16),
        compiler_params=_params(("arbitrary",) * 2), name="nsa_sample_sel")(
            page_table, qbd, new_kv, sel, cvec, blast, bnew, o_c, o_w, gates, *([cache] * pp))


def _layer_norm(y, g, b):
    mu = jnp.mean(y, axis=-1, keepdims=True)
    d = y - mu
    var = jnp.mean(d * d, axis=-1, keepdims=True)
    return d * lax.rsqrt(var + LN_EPS) * g + b


def _split_bf16(x):
    hi = x.astype(BF16)
    return hi, (x - hi.astype(F32)).astype(BF16)


def _mix_router_kernel(a1_ref, a2_ref, w1_ref, w2_ref, x_ref, g_ref, b_ref, wr_ref, br_ref, u_ref,
                       xn_ref, xb_ref, rt_ref, gt_ref, cnt_ref, *, alpha):
    mix = _dot(a1_ref[...], w1_ref[...]) + _dot(a2_ref[...], w2_ref[...])
    xn = _layer_norm(alpha * x_ref[...] + mix, g_ref[...], b_ref[...])
    xn_ref[...] = xn
    xb_ref[...] = xn.astype(BF16)
    xh, xl = _split_bf16(xn)
    wh, wl = _split_bf16(wr_ref[...])
    logits = _dot_nt(wh, xh) + _dot_nt(wh, xl) + _dot_nt(wl, xh) + br_ref[...]
    e_iota = _iota(logits.shape, 0)
    work = logits
    vals, sels = [], []
    for _ in range(TOP_K):
        top = jnp.max(work, axis=0, keepdims=True)
        idx = jnp.min(jnp.where(work == top, e_iota, N_EXPERTS), axis=0, keepdims=True)
        sel = e_iota == idx
        vals.append(top)
        sels.append(sel)
        work = jnp.where(sel, -jnp.inf, work)
    ex = [jnp.exp(v - vals[0]) for v in vals]
    den = ex[0] + ex[1] + ex[2] + ex[3]
    gate = jnp.zeros(logits.shape, F32)
    chosen = sels[0]
    for k in range(TOP_K):
        gate = jnp.where(sels[k], ex[k] / den, gate)
        chosen = chosen | sels[k]
    cf = jnp.where(chosen, 1.0, 0.0)
    rank = _dot(cf.astype(BF16), u_ref[...])
    rt_ref[0] = jnp.where(chosen, rank, -1.0)
    gt_ref[0] = gate
    cnt = jnp.sum(cf, axis=1, keepdims=True)
    cnt_ref[0] = jnp.broadcast_to(cnt, (N_EXPERTS, LANES)).astype(jnp.int32)


def mix_router(a1, a2, w1, w2, x, ln_g, ln_b, w_router_t, b_router, tt, alpha):
    n = x.shape[0]
    half = a1.shape[1]
    j = np.arange(tt)
    before = jnp.asarray((j[:, None] < j[None, :]).astype(np.float32), dtype=BF16)
    tile = lambda w: pl.BlockSpec((tt, w), lambda i: (i, 0))
    const = lambda shape: pl.BlockSpec(shape, lambda i: (0,) * len(shape))
    route = pl.BlockSpec((1, N_EXPERTS, tt), lambda i: (i, 0, 0))
    return pl.pallas_call(
        functools.partial(_mix_router_kernel, alpha=alpha), grid=(n // tt,),
        in_specs=[tile(half), tile(half), const((half, D_MODEL)), const((half, D_MODEL)), tile(D_MODEL),
                  const((1, D_MODEL)), const((1, D_MODEL)), const((N_EXPERTS, D_MODEL)),
                  const((N_EXPERTS, 1)), const((tt, tt))],
        out_specs=[tile(D_MODEL), tile(D_MODEL), route, route,
                   pl.BlockSpec((1, N_EXPERTS, LANES), lambda i: (i, 0, 0))],
        out_shape=[jax.ShapeDtypeStruct((n, D_MODEL), F32), jax.ShapeDtypeStruct((n, D_MODEL), BF16),
                   jax.ShapeDtypeStruct((n // tt, N_EXPERTS, tt), F32),
                   jax.ShapeDtypeStruct((n // tt, N_EXPERTS, tt), F32),
                   jax.ShapeDtypeStruct((n // tt, N_EXPERTS, LANES), jnp.int32)],
        compiler_params=_params(("arbitrary",)), name="mix_router")(
            a1, a2, w1, w2, x, ln_g, ln_b, w_router_t, b_router, before)


def _pair_permutation():
    c = np.arange(2 * LANES)
    dest = np.where(c % 2 == 0, c // 2, LANES + c // 2)
    return jnp.asarray((dest[:, None] == c[None, :]).astype(np.float32), dtype=BF16)


def _split_pairs_kernel(w_ref, s_ref, o_ref):
    s = s_ref[...]
    for blk in range(w_ref.shape[2] // (2 * LANES)):
        cols = slice(blk * 2 * LANES, (blk + 1) * 2 * LANES)
        o_ref[0, :, cols] = _dot(w_ref[0, :, cols].astype(BF16), s).astype(BF16)


def split_pairs(w_up):
    e, d, f2 = w_up.shape
    return pl.pallas_call(
        _split_pairs_kernel, grid=(e,),
        in_specs=[pl.BlockSpec((1, d, f2), lambda i: (i, 0, 0)),
                  pl.BlockSpec((2 * LANES, 2 * LANES), lambda i: (0, 0))],
        out_specs=pl.BlockSpec((1, d, f2), lambda i: (i, 0, 0)),
        out_shape=jax.ShapeDtypeStruct((e, d, f2), BF16),
        compiler_params=_params(("arbitrary",)), name="split_pairs")(w_up, _pair_permutation())


def _moe_kernel(cnt_ref, xb_ref, xn_ref, rt_ref, gt_ref, wu_ref, bu_ref, wd_ref, bd_ref,
                lg_ref, lb_ref, y_ref, yb_ref, acc_ref, ybuf_ref, gbuf_ref, *, alpha):
    i = pl.program_id(0)
    e = pl.program_id(1)
    tt = xb_ref.shape[0]

    @pl.when(e == 0)
    def _():
        acc_ref[...] = jnp.zeros_like(acc_ref)

    n_chunks = (cnt_ref[i, e] + MOE_CH - 1) // MOE_CH
    slot = rt_ref[0, pl.ds(e, 1), :]
    gate = gt_ref[0, pl.ds(e, 1), :]

    def chunk(c):
        want = (_iota((MOE_CH, tt), 0) + c * MOE_CH).astype(F32)
        hit = slot == want
        xe = _dot(jnp.where(hit, 1.0, 0.0).astype(BF16), xb_ref[...]).astype(BF16)
        h = _dot(xe, wu_ref[0]) + bu_ref[0]
        acts = []
        for blk in range(h.shape[1] // (2 * LANES)):
            h_glu = jnp.minimum(h[:, blk * 2 * LANES:blk * 2 * LANES + LANES], SWIGLU_LIMIT)
            h_lin = jnp.clip(h[:, blk * 2 * LANES + LANES:(blk + 1) * 2 * LANES], -SWIGLU_LIMIT, SWIGLU_LIMIT)
            acts.append((h_glu * jax.nn.sigmoid(SWIGLU_ALPHA * h_glu) * (h_lin + 1.0)).astype(BF16))
        y = _dot(jnp.concatenate(acts, axis=-1), wd_ref[0]) + bd_ref[0]
        return y.astype(BF16), jnp.where(hit, gate, 0.0).astype(BF16)

    park = pl.multiple_of((e % MOE_GROUP) * MOE_CH, MOE_CH)
    y0, back0 = chunk(0)
    ybuf_ref[pl.ds(park, MOE_CH), :] = y0
    gbuf_ref[pl.ds(park, MOE_CH), :] = back0

    def extra(c, _):
        y, back = chunk(c)
        acc_ref[...] += _dot_tn(back, y)
        return 0

    lax.fori_loop(1, n_chunks, extra, 0)

    @pl.when(e % MOE_GROUP == MOE_GROUP - 1)
    def _():
        acc_ref[...] += _dot_tn(gbuf_ref[...], ybuf_ref[...])

    @pl.when(e == pl.num_programs(1) - 1)
    def _():
        y = _layer_norm(alpha * xn_ref[...] + acc_ref[...], lg_ref[...], lb_ref[...])
        y_ref[...] = y
        yb_ref[...] = y.astype(BF16)


def moe_ln(cnt, xb, xn, rt, gt, w_up, b_up, w_down, b_down, ln_g, ln_b, tt, alpha):
    n = xn.shape[0]
    ff = w_down.shape[1]
    tile = pl.BlockSpec((tt, D_MODEL), lambda i, e, c: (i, 0))
    route = pl.BlockSpec((1, N_EXPERTS, tt), lambda i, e, c: (i, 0, 0))
    per_e = lambda shape: pl.BlockSpec((1,) + shape, lambda i, e, c: (e, 0, 0))
    const = pl.BlockSpec((1, D_MODEL), lambda i, e, c: (0, 0))
    grid_spec = pltpu.PrefetchScalarGridSpec(
        num_scalar_prefetch=1, grid=(n // tt, N_EXPERTS),
        in_specs=[tile, tile, route, route, per_e((D_MODEL, 2 * ff)), per_e((1, 2 * ff)),
                  per_e((ff, D_MODEL)), per_e((1, D_MODEL)), const, const],
        out_specs=[tile, tile],
        scratch_shapes=[pltpu.VMEM((tt, D_MODEL), F32), pltpu.VMEM((MOE_GROUP * MOE_CH, D_MODEL), BF16),
                        pltpu.VMEM((MOE_GROUP * MOE_CH, tt), BF16)])
    return pl.pallas_call(
        functools.partial(_moe_kernel, alpha=alpha), grid_spec=grid_spec,
        out_shape=[jax.ShapeDtypeStruct((n, D_MODEL), F32), jax.ShapeDtypeStruct((n, D_MODEL), BF16)],
        compiler_params=_params(("arbitrary",) * 2), name="moe_ln")(
            cnt, xb, xn, rt, gt, w_up, b_up, w_down, b_down, ln_g, ln_b)


MOE_TILE = 1024


def _split_pairs_bias(b_up):
    e, f2 = b_up.shape
    b = b_up.astype(F32).reshape(e, f2 // (2 * LANES), LANES, 2).transpose(0, 1, 3, 2)
    return b.reshape(e, 1, f2)


def _front_pad(a, axis, rows=KPAD):
    pad = [(0, 0)] * a.ndim
    pad[axis] = (rows, 0)
    return jnp.pad(a, pad)


def _first_tile_bias(tab, kt):
    by_u = _hankel(_window(tab, -(QB - 1), QB + kt - 1), QB, kt)
    return jnp.flip(by_u, axis=2).transpose(0, 2, 1)


def _pad_rows(a, rows):
    return jnp.pad(a, ((0, 0), (0, rows - a.shape[1]), (0, 0)))


def _channel_mixer(layer, a1, a2, w_out, x, p, tt):
    alpha = p["alpha"]
    half = a1.shape[1]
    w_out = w_out.astype(BF16)
    xn, xb, rt, gt, cnt = mix_router(
        a1, a2, w_out[:half], w_out[half:], x, p["ln_mix_g"][layer][None], p["ln_mix_b"][layer][None],
        p["w_router"][layer].T.astype(F32), p["b_router"][layer][:, None].astype(F32), tt, alpha)
    return moe_ln(cnt[:, :, 0], xb, xn, rt, gt, *p["experts"][layer],
                  p["ln_ffn_g"][layer][None], p["ln_ffn_b"][layer][None], tt, alpha)


def _even_prompt(h, b, t, cmp_w, bias_c, bias0, c31):
    h3 = h.reshape(b, t, -1)
    heads = lambda a, nh: a.reshape(b, t, nh, HEAD_DIM).transpose(0, 2, 1, 3).astype(BF16)
    heads_t = lambda a, nh: a.reshape(b, t, nh, HEAD_DIM).transpose(0, 2, 3, 1).astype(BF16)
    o = 3 * SB_W
    o_sb = sb_prompt(heads(h3[..., :SB_W], SB_HEADS),
                     _front_pad(heads(h3[..., SB_W:2 * SB_W], SB_HEADS), 2),
                     _front_pad(heads_t(h3[..., 2 * SB_W:o], SB_HEADS), 3))
    o_sb = o_sb.transpose(0, 2, 1, 3).reshape(b * t, SB_W)
    nq = h3[..., o:o + NSA_Q_W].reshape(b, t, NSA_GROUPS, NSA_REP, HEAD_DIM).transpose(0, 2, 3, 1, 4)
    o += NSA_Q_W
    nkv = h3[..., o:o + 6 * NSA_KV_W].reshape(b, t, 6, NSA_GROUPS, HEAD_DIM)
    gates = h3[..., o + 6 * NSA_KV_W:o + 6 * NSA_KV_W + NSA_GATE_W]
    gates = gates.reshape(b, t, NSA_GROUPS, NSA_REP * 3).transpose(0, 2, 3, 1)
    gates = jnp.pad(gates, ((0, 0), (0, 0), (0, NSA_REP), (0, 0)))
    n_pages = t // PAGE
    table = jnp.arange(b * n_pages, dtype=jnp.int32).reshape(b, n_pages)
    kcvc = compress_pages(table, h.reshape(b * n_pages, 1, PAGE, h.shape[-1]), 0, o // (2 * HEAD_DIM), *cmp_w)
    nc = t // CMP_BLOCK
    ncp = -(-nc // LANES) * LANES
    kcvc = kcvc.reshape(b, nc, 2, NSA_GROUPS, HEAD_DIM).astype(BF16)
    kc = jnp.pad(kcvc[:, :, 0].transpose(0, 2, 1, 3), ((0, 0), (0, 0), (0, ncp - nc), (0, 0)))
    vct = jnp.pad(kcvc[:, :, 1].transpose(0, 2, 3, 1), ((0, 0), (0, 0), (0, 0), (0, ncp - nc)))
    grp = lambda i: _front_pad(nkv[:, :, i].transpose(0, 2, 1, 3).astype(BF16), 2)
    grp_t = lambda i: _front_pad(nkv[:, :, i].transpose(0, 2, 3, 1).astype(BF16), 3)
    o_nsa = nsa_prompt(nq.astype(BF16), kc, vct, grp(2), grp_t(3), grp(4), grp_t(5),
                       bias_c, bias0, c31, gates, -(-t // SEL_BLOCK))
    sb_rows = h3[..., SB_W:3 * SB_W].reshape(b, t, 2, SB_HEADS, HEAD_DIM)
    keep = min(WINDOW, t)
    return (o_sb, o_nsa.reshape(b * t, NSA_Q_W), sb_rows, nkv[:, :, :4], nkv[:, t - keep:, 4:])


def _even_sample(h, db, page_table, cache_sb, cache_nsa, win_state, layer, cmp_w, tabs):
    h3 = h.reshape(db, DEC_S, -1)
    past = page_table.shape[1] * PAGE
    o = 3 * SB_W
    q = h3[..., :SB_W].reshape(db, DEC_S, SB_HEADS, HEAD_DIM).transpose(0, 2, 1, 3) * ATT_SCALE
    eye = jnp.eye(SB_HEADS, dtype=F32)
    qbd = (q[:, :, :, None, :] * eye[None, :, None, :, None]).reshape(db, SB_HEADS * DEC_S, SB_W)
    o_sb = sb_sample(page_table, qbd.astype(BF16), _pad_rows(h3[..., SB_W:o], PAGE).astype(BF16),
                     cache_sb, layer)
    q = h3[..., o:o + NSA_Q_W].reshape(db, DEC_S, NSA_GROUPS, NSA_REP, HEAD_DIM).transpose(0, 3, 2, 1, 4)
    eye = jnp.eye(NSA_GROUPS, dtype=F32)
    qbd = (q[:, :, :, :, None, :] * ATT_SCALE * eye[None, None, :, None, :, None])
    qbd = qbd.reshape(db, NSA_ROWS, NSA_KV_W).astype(BF16)
    o += NSA_Q_W
    nkv = h3[..., o:o + 6 * NSA_KV_W]
    gates = h3[..., o + 6 * NSA_KV_W:o + 6 * NSA_KV_W + NSA_GATE_W]
    gates = gates.reshape(db, DEC_S, NSA_GROUPS, NSA_REP, 3).transpose(0, 3, 2, 1, 4).reshape(db, NSA_ROWS, 3)
    gates = jnp.pad(gates, ((0, 0), (0, 0), (0, LANES - 3)))
    kcvc = compress_pages(page_table, cache_nsa, layer, 0, *cmp_w, positions_on_lanes=True)
    win_len = win_state.shape[2]
    new_win = nkv[..., 4 * NSA_KV_W:]
    win_all = jnp.concatenate([win_state[:, layer].reshape(db, win_len, 2 * NSA_KV_W), new_win], axis=1)
    wl = tabs["bias_w"].shape[1]
    n_blocks = -(-(past + DEC_S) // SEL_BLOCK)
    o_c, o_w, sel = nsa_sample_cw(qbd, kcvc, tabs["bias_c"], _pad_rows(win_all, wl).astype(BF16),
                                  tabs["bias_w"], win_len, min(N_SEL, n_blocks) - 1)
    o_nsa = nsa_sample_sel(page_table, qbd, _pad_rows(nkv[..., 2 * NSA_KV_W:4 * NSA_KV_W], PAGE).astype(BF16),
                           sel, tabs["cvec"], tabs["blast"], tabs["bnew"], o_c, o_w, gates, cache_nsa, layer)
    sb_rows = h3[..., SB_W:3 * SB_W].reshape(db, DEC_S, 2, SB_HEADS, HEAD_DIM)
    nsa_rows = nkv[..., :4 * NSA_KV_W].reshape(db, DEC_S, 4, NSA_GROUPS, HEAD_DIM)
    win_rows = win_all[:, max(0, win_len + DEC_S - WINDOW):].reshape(db, -1, 2, NSA_GROUPS, HEAD_DIM)
    return (o_sb.reshape(db * DEC_S, SB_W), o_nsa.reshape(db * DEC_S, NSA_Q_W), sb_rows, nsa_rows, win_rows)


def _odd_prompt(h, b, t, bias0, c31, lam_rows, sub_g, lam_init):
    h3 = h.reshape(b, t, -1)
    q = h3[..., :DIFF_Q_W].reshape(b, t, DIFF_KV_HEADS, DIFF_REP, 2, HEAD_DIM).transpose(0, 2, 3, 4, 1, 5)
    kv = h3[..., DIFF_Q_W:].reshape(b, t, 2, DIFF_KV_HEADS, 2 * HEAD_DIM)
    k = kv[:, :, 0].reshape(b, t, DIFF_KV_HEADS, 2, HEAD_DIM).transpose(0, 2, 3, 1, 4)
    vt = kv[:, :, 1].transpose(0, 2, 3, 1)
    kpad = bias0.shape[1] - QB
    o = diff_prompt(q.astype(BF16), _front_pad(k.astype(BF16), 3, kpad), _front_pad(vt.astype(BF16), 3, kpad),
                    bias0, c31, lam_rows, sub_g.T, lam_init)
    return o.reshape(b * t, DIFF_Q_W), kv


def _odd_sample(h, db, page_table, cache, layer, tabs, lam_rows, sub_g, lam_init):
    h3 = h.reshape(db, DEC_S, -1)
    q = h3[..., :DIFF_Q_W].reshape(db, DEC_S, DIFF_KV_HEADS, DIFF_REP, 2, HEAD_DIM).transpose(0, 2, 3, 4, 1, 5)
    eye_c = jnp.eye(2, dtype=F32)
    qc = q[..., None, :] * ATT_SCALE * eye_c[None, None, None, :, None, :, None]
    qc = qc.reshape(db, DIFF_HEADS * 2 * DEC_S, 2 * HEAD_DIM).astype(BF16)
    kv = h3[..., DIFF_Q_W:].reshape(db, DEC_S, 2, DIFF_KV_HEADS, 2 * HEAD_DIM)
    new_kv = _pad_rows(kv.reshape(db, DEC_S * DIFF_ROW_STRIDE, 2 * HEAD_DIM), PAGE * DIFF_ROW_STRIDE)
    o = diff_sample(page_table, qc, new_kv, tabs["cvec"], tabs["blast"], tabs["bnew"], lam_rows, sub_g,
                    cache, layer, lam_init)
    return o.reshape(db * DEC_S, DIFF_Q_W), kv


def _sample_tables(tab, past, arrange, with_cmp):
    def table(first, n, step=1):
        rows = [jnp.flip(_window(tab, first(t) - step * (n - 1), n, step), axis=1) for t in range(DEC_S)]
        return arrange(jnp.stack(rows, axis=1))

    tabs = {
        "cvec": arrange(jnp.broadcast_to(tab[:, -1][:, None, None], (tab.shape[0], DEC_S, 1))),
        "blast": table(lambda t: t + PAGE, PAGE),
        "bnew": table(lambda t: t, PAGE),
    }
    if with_cmp:
        nc = past // CMP_BLOCK
        tabs["bias_c"] = table(lambda t: past + t - (CMP_BLOCK - 1), nc, CMP_BLOCK)
        win_len = min(WINDOW, past)
        wl = -(-(win_len + DEC_S) // LANES) * LANES
        tabs["bias_w"] = table(lambda t: t + win_len, wl)
    return tabs


def kernel(x_prompt, x_sample, cache_sb_kv, cache_nsa_kv, cache_diff_kv, state_nsa_win, page_table, rel_bias,
           even_w_in, even_cmp_pe, even_cmp_wk, even_cmp_wv, even_w_out, odd_w_in, odd_lambda, odd_subln_g,
           odd_w_out, ln_mix_g, ln_mix_b, ln_ffn_g, ln_ffn_b, moe_w_router, moe_b_router, moe_w_up, moe_b_up,
           moe_w_down, moe_b_down):
    b, t, d = x_prompt.shape
    db, s, _ = x_sample.shape
    depth = ln_mix_g.shape[0]
    n_pool = cache_sb_kv.shape[0]
    past = page_table.shape[1] * PAGE
    assert s == DEC_S and d == D_MODEL and t % QB == 0 and (b * t) % MOE_TILE == 0
    assert past % (SEL_BLOCK * LANES) == 0 and state_nsa_win.shape[2] == WINDOW

    params = {
        "alpha": (2.0 * depth) ** 0.25,
        "ln_mix_g": ln_mix_g.astype(F32), "ln_mix_b": ln_mix_b.astype(F32),
        "ln_ffn_g": ln_ffn_g.astype(F32), "ln_ffn_b": ln_ffn_b.astype(F32),
        "w_router": moe_w_router, "b_router": moe_b_router,
        "experts": [(split_pairs(moe_w_up[l]), _split_pairs_bias(moe_b_up[l]),
                     moe_w_down[l].astype(BF16), moe_b_down[l][:, None, :].astype(F32)) for l in range(depth)],
    }
    tab = _distance_table(rel_bias)
    c31 = tab[:, -1]
    bias0 = _first_tile_bias(tab, KT)
    bias0_diff = _first_tile_bias(tab, DIFF_KT)
    bias_c = _cmp_band_table(tab)
    nsa_arrange = lambda x: x.reshape(NSA_GROUPS, NSA_REP, DEC_S, -1).transpose(1, 0, 2, 3).reshape(NSA_ROWS, -1)
    diff_arrange = lambda x: jnp.broadcast_to(x[:, None], (DIFF_HEADS, 2) + x.shape[1:]).reshape(
        DIFF_HEADS * 2 * DEC_S, -1)
    nsa_tabs = _sample_tables(tab, past, nsa_arrange, True)
    diff_tabs = _sample_tables(tab, past, diff_arrange, False)

    cache_sb = _positions_on_lanes(cache_sb_kv)
    cache_nsa = _positions_on_lanes(cache_nsa_kv)
    cache_diff = cache_diff_kv.reshape(n_pool, -1, PAGE * DIFF_ROW_STRIDE, 2 * HEAD_DIM)

    xp = x_prompt.reshape(b * t, d).astype(F32)
    xs = x_sample.reshape(db * s, d).astype(F32)
    xpb, xsb = xp.astype(BF16), xs.astype(BF16)
    outs = {k: [] for k in ("sb_p", "sb_s", "nsa_p", "nsa_s", "win_p", "win_s", "diff_p", "diff_s")}
    for l in range(depth):
        j = l // 2
        if l % 2 == 0:
            w_in = jnp.pad(even_w_in[j], ((0, 0), (0, EVEN_IN_PAD - EVEN_IN))).astype(BF16)
            cmp_w = _compress_weights(even_cmp_pe[j], even_cmp_wk[j], even_cmp_wv[j])
            a1, a2, r_sb, r_nsa, r_win = _even_prompt(matmul(xpb, w_in, 512), b, t, cmp_w, bias_c, bias0, c31)
            s1, s2, s_sb, s_nsa, s_win = _even_sample(matmul(xsb, w_in, 512), db, page_table, cache_sb,
                                                      cache_nsa, state_nsa_win, j, cmp_w, nsa_tabs)
            w_out = even_w_out[j]
            for key, val in (("sb_p", r_sb), ("sb_s", s_sb), ("nsa_p", r_nsa), ("nsa_s", s_nsa),
                             ("win_p", r_win), ("win_s", s_win)):
                outs[key].append(val)
        else:
            lam_init = 0.8 - 0.6 * math.exp(-0.3 * l)
            w_in = odd_w_in[j].astype(BF16)
            lam_rows = odd_lambda[j].astype(F32)
            sub_g = odd_subln_g[j][None].astype(F32)
            ap, r_diff = _odd_prompt(matmul(xpb, w_in, 512), b, t, bias0_diff, c31, lam_rows, sub_g, lam_init)
            as_, s_diff = _odd_sample(matmul(xsb, w_in, 512), db, page_table, cache_diff, j, diff_tabs,
                                      lam_rows, sub_g, lam_init)
            half = DIFF_Q_W // 2
            a1, a2, s1, s2 = ap[:, :half], ap[:, half:], as_[:, :half], as_[:, half:]
            w_out = odd_w_out[j]
            outs["diff_p"].append(r_diff)
            outs["diff_s"].append(s_diff)
        xp, xpb = _channel_mixer(l, a1, a2, w_out, xp, params, MOE_TILE)
        xs, xsb = _channel_mixer(l, s1, s2, w_out, xs, params, db * s)
    stack = lambda key: jnp.stack(outs[key], axis=1)
    return (xp.reshape(b, t, d), xs.reshape(db, s, d), stack("sb_p"), stack("sb_s"), stack("nsa_p"),
            stack("nsa_s"), stack("diff_p"), stack("diff_s"), stack("win_p"), stack("win_s"))
```

```python
import functools
import math

import numpy as np
import jax
import jax.numpy as jnp
from jax import lax
from jax.experimental import pallas as pl
from jax.experimental.pallas import tpu as pltpu

F32 = jnp.float32
BF16 = jnp.bfloat16

D_MODEL = 1024
HEAD_DIM = 64
SB_HEADS = 8
NSA_HEADS = 8
NSA_GROUPS = 2
NSA_REP = NSA_HEADS // NSA_GROUPS
CMP_BLOCK = 32
SEL_BLOCK = 64
N_SEL = 16
WINDOW = 512
DIFF_HEADS = 8
DIFF_KV_HEADS = 4
DIFF_REP = DIFF_HEADS // DIFF_KV_HEADS
N_BUCKETS = 32
MAX_DISTANCE = 128
N_EXPERTS = 32
TOP_K = 4
SWIGLU_ALPHA = 1.702
SWIGLU_LIMIT = 7.0
LN_EPS = 1e-5
PAGE = 128
NEG = -1e30
FORCE = 1e9
ATT_SCALE = HEAD_DIM ** -0.5

SB_W = SB_HEADS * HEAD_DIM
NSA_Q_W = NSA_HEADS * HEAD_DIM
NSA_KV_W = NSA_GROUPS * HEAD_DIM
NSA_GATE_W = NSA_HEADS * 3
EVEN_IN = 3 * SB_W + NSA_Q_W + 6 * NSA_KV_W + NSA_GATE_W
EVEN_IN_PAD = -(-EVEN_IN // 128) * 128
DIFF_Q_W = DIFF_HEADS * 2 * HEAD_DIM
DIFF_KV_W = DIFF_KV_HEADS * 2 * HEAD_DIM
ODD_IN = DIFF_Q_W + 2 * DIFF_KV_W

LANES = 128
SUBLANES = 8
VMEM_LIMIT = 52 * 1024 * 1024

QB = 128
KT = 512
KPAD = KT - QB
DIFF_KT = 1024
MOE_CH = 144
DEC_S = 8


def _dot(a, b):
    return jnp.dot(a, b, preferred_element_type=F32)


def _dot_nt(a, b):
    return lax.dot_general(a, b, (((1,), (1,)), ((), ())), preferred_element_type=F32)


def _dot_tn(a, b):
    return lax.dot_general(a, b, (((0,), (0,)), ((), ())), preferred_element_type=F32)


def _iota(shape, dim):
    return lax.broadcasted_iota(jnp.int32, shape, dim)


def _params(sem, vmem=VMEM_LIMIT):
    return pltpu.CompilerParams(dimension_semantics=sem, vmem_limit_bytes=vmem)


def _bucket_np(dist):
    n = np.maximum(dist, 0)
    exact = N_BUCKETS // 2
    nf = np.maximum(n, 1).astype(np.float32)
    large = exact + (np.log(nf / np.float32(exact)) / np.float32(math.log(MAX_DISTANCE / exact))
                     * np.float32(N_BUCKETS - exact)).astype(np.int32)
    return np.where(n < exact, n, np.minimum(large, N_BUCKETS - 1)).astype(np.int32)


DIST_TABLE = 256


def _distance_table(rel_bias):
    onehot = np.eye(N_BUCKETS, dtype=np.float32)[_bucket_np(np.arange(DIST_TABLE))]
    return jnp.dot(jnp.asarray(onehot), rel_bias.astype(F32), precision=lax.Precision.HIGHEST).T


def _window(tab, start, n, step=1):
    stop = start + step * (n - 1)
    lo, hi = max(0, -start), max(0, stop + 1 - tab.shape[1])
    ext = jnp.pad(tab, ((0, 0), (lo, hi)), mode="edge")
    return ext[:, start + lo:stop + lo + 1:step]


def _hankel(v, rows, cols):
    n = rows + cols - 1
    flat = jnp.tile(v, (1, rows + 1))[:, :rows * (n + 1)]
    return flat.reshape(v.shape[0], rows, n + 1)[:, :, :cols]


def _mm_kernel(x_ref, w_ref, o_ref):
    o_ref[...] = _dot(x_ref[...], w_ref[...])


def matmul(x, w, tm):
    m, k = x.shape
    n = w.shape[1]
    tm = min(tm, m)
    return pl.pallas_call(
        _mm_kernel, grid=(m // tm,),
        in_specs=[pl.BlockSpec((tm, k), lambda i: (i, 0)), pl.BlockSpec((k, n), lambda i: (0, 0))],
        out_specs=pl.BlockSpec((tm, n), lambda i: (i, 0)),
        out_shape=jax.ShapeDtypeStruct((m, n), F32),
        compiler_params=_params(("arbitrary",)), name="in_proj")(x, w)


def _softmax_step(s, valid, m, l, acc, v, v_on_lanes=False):
    if valid is not None:
        s = jnp.where(valid, s, NEG)
    m_new = jnp.maximum(m, jnp.max(s, axis=-1, keepdims=True))
    p = jnp.exp(s - m_new)
    if valid is not None:
        p = jnp.where(valid, p, 0.0)
    alpha = jnp.exp(m - m_new)
    l = alpha * l + jnp.sum(p, axis=-1, keepdims=True)
    pv = _dot_nt(p.astype(BF16), v) if v_on_lanes else _dot(p.astype(BF16), v)
    return m_new, l, alpha * acc + pv


def _softmax_full(s, valid):
    if valid is not None:
        s = jnp.where(valid, s, NEG)
    m = jnp.max(s, axis=-1, keepdims=True)
    e = jnp.exp(s - m)
    if valid is not None:
        e = jnp.where(valid, e, 0.0)
    return e / jnp.maximum(jnp.sum(e, axis=-1, keepdims=True), 1e-30)


def _sb_step(z, valid, carry, acc, v, u, v_on_lanes=False):
    sp = jnp.maximum(z, 0.0) + jnp.log(1.0 + jnp.exp(-jnp.abs(z)))
    lk = -sp if valid is None else jnp.where(valid, -sp, 0.0)
    hi = lk.astype(BF16)
    lo = (lk - hi.astype(F32)).astype(BF16)
    later = _dot(hi, u) + _dot(lo, u) + carry
    w = jnp.exp(z - sp + later)
    if valid is not None:
        w = jnp.where(valid, w, 0.0)
    acc = acc + (_dot_nt(w.astype(BF16), v) if v_on_lanes else _dot(w.astype(BF16), v))
    carry = carry + jnp.sum(lk, axis=-1, keepdims=True)
    return carry, acc


def _softmax_step_t(s, valid, m, l, acc, vt):
    if valid is not None:
        s = jnp.where(valid, s, NEG)
    m_new = jnp.maximum(m, jnp.max(s, axis=0, keepdims=True))
    p = jnp.exp(s - m_new)
    if valid is not None:
        p = jnp.where(valid, p, 0.0)
    alpha = jnp.exp(m - m_new)
    l = alpha * l + jnp.sum(p, axis=0, keepdims=True)
    acc = alpha * acc + _dot(vt, p.astype(BF16))
    return m_new, l, acc


def _softmax_full_t(s, valid):
    s = jnp.where(valid, s, NEG)
    m = jnp.max(s, axis=0, keepdims=True)
    e = jnp.where(valid, jnp.exp(s - m), 0.0)
    return e / jnp.maximum(jnp.sum(e, axis=0, keepdims=True), 1e-30)


def _sb_step_t(z, valid, carry, acc, vt, lm):
    nq = z.shape[1]
    sp = jnp.maximum(z, 0.0) + jnp.log(1.0 + jnp.exp(-jnp.abs(z)))
    lk = jnp.where(valid, -sp, 0.0)
    hi = lk.astype(BF16)
    lo = (lk - hi.astype(F32)).astype(BF16)
    both = _dot(lm, jnp.concatenate([hi, lo], axis=1))
    later = both[:, :nq] + both[:, nq:] + carry
    w = jnp.where(valid, jnp.exp(z - sp + later), 0.0)
    acc = acc + _dot(vt, w.astype(BF16))
    carry = carry + jnp.sum(lk, axis=0, keepdims=True)
    return carry, acc


SB_DEAD = -104.0


def _later_matrix(n):
    j = np.arange(n)
    return jnp.asarray((j[:, None] > j[None, :]).astype(np.float32), dtype=BF16)


def _sb_prompt_kernel(q_ref, k_ref, vt_ref, lm_ref, o_ref):
    qi = pl.program_id(2)
    q0 = qi * QB
    q = (q_ref[0, 0].astype(F32) * ATT_SCALE).astype(BF16)
    lm = lm_ref[...]
    key = _iota((KT, QB), 0)
    qry = _iota((KT, QB), 1)

    def tile(j, carry, acc, first):
        start = pl.multiple_of(q0 - j * KT, QB)
        k = k_ref[0, 0, pl.ds(start, KT), :]
        vt = vt_ref[0, 0, :, pl.ds(start, KT)]
        z = _dot_nt(k, q)
        valid = key >= KPAD - start
        if first:
            valid = valid & (qry + KPAD - key > 0)
        return _sb_step_t(z, valid, carry, acc, vt, lm)

    carry, acc = tile(0, jnp.zeros((1, QB), F32), jnp.zeros((HEAD_DIM, QB), F32), True)
    n_tiles = qi // (KT // QB) + 1

    def more(state):
        j, carry, _ = state
        return (j < n_tiles) & (jnp.max(carry) > SB_DEAD)

    def step(state):
        j, carry, acc = state
        carry, acc = tile(j, carry, acc, False)
        return j + 1, carry, acc

    _, _, acc = lax.while_loop(more, step, (jnp.int32(1), carry, acc))
    o_ref[0, 0] = acc.T.astype(o_ref.dtype)


def sb_prompt(q, k, vt):
    b, h, t, _ = q.shape
    tp = k.shape[2]
    return pl.pallas_call(
        _sb_prompt_kernel, grid=(b, h, t // QB),
        in_specs=[pl.BlockSpec((1, 1, QB, HEAD_DIM), lambda b_, h_, i: (b_, h_, i, 0)),
                  pl.BlockSpec((1, 1, tp, HEAD_DIM), lambda b_, h_, i: (b_, h_, 0, 0)),
                  pl.BlockSpec((1, 1, HEAD_DIM, tp), lambda b_, h_, i: (b_, h_, 0, 0)),
                  pl.BlockSpec((KT, KT), lambda b_, h_, i: (0, 0))],
        out_specs=pl.BlockSpec((1, 1, QB, HEAD_DIM), lambda b_, h_, i: (b_, h_, i, 0)),
        out_shape=jax.ShapeDtypeStruct((b, h, t, HEAD_DIM), BF16),
        compiler_params=_params(("arbitrary",) * 3), name="sb_prompt")(q, k, vt, _later_matrix(KT).T)


def _diff_lambda(lam_ref, lam_init):
    lv = lam_ref[...].astype(F32)
    a = jnp.sum(lv[0:1] * lv[1:2], axis=-1, keepdims=True)
    b = jnp.sum(lv[2:3] * lv[3:4], axis=-1, keepdims=True)
    return jnp.exp(a) - jnp.exp(b) + lam_init


def _diff_finish(o1, l1, o2, l2, lam, sub_g, lam_init):
    a = o1 / jnp.maximum(l1, 1e-30) - lam * (o2 / jnp.maximum(l2, 1e-30))
    a = a * lax.rsqrt(jnp.mean(jnp.square(a), axis=-1, keepdims=True) + LN_EPS)
    return a * sub_g * (1.0 - lam_init)


def _diff_prompt_kernel(q_ref, k_ref, vt_ref, b0_ref, c31_ref, lam_ref, sg_ref, o_ref, *, lam_init):
    qi = pl.program_id(2)
    q0 = qi * QB
    w = DIFF_REP * QB
    kt = b0_ref.shape[1]
    kpad = kt - QB
    key = _iota((kt, w), 0)
    qry = _iota((kt, w), 1) % QB
    qcat = [jnp.concatenate([(q_ref[0, 0, r, c].astype(F32) * ATT_SCALE).astype(BF16)
                             for r in range(DIFF_REP)], axis=0) for c in range(2)]

    def tile(j, state, first, padded):
        start = pl.multiple_of(q0 - j * kt, QB)
        vt = vt_ref[0, 0, :, pl.ds(start, kt)]
        valid = (key >= kpad - start) if padded else None
        if first:
            valid = valid & (qry + kpad - key >= 0)
        out = []
        for c in range(2):
            k = k_ref[0, 0, c, pl.ds(start, kt), :]
            s = _dot_nt(k, qcat[c]) + (b0_ref[0] if first else c31_ref[0])
            out.append(_softmax_step_t(s, valid, *state[c], vt))
        return tuple(out)

    init = tuple((jnp.full((1, w), NEG, F32), jnp.zeros((1, w), F32),
                  jnp.zeros((2 * HEAD_DIM, w), F32)) for _ in range(2))
    state = tile(0, init, True, True)
    last = qi // (kt // QB)
    state = lax.fori_loop(1, last, lambda j, st: tile(j, st, False, False), state)
    state = lax.cond(last >= 1, lambda st: tile(last, st, False, True), lambda st: st, state)
    lam = _diff_lambda(lam_ref, lam_init)
    (_, l1, o1), (_, l2, o2) = state
    a = o1 / jnp.maximum(l1, 1e-30) - lam * (o2 / jnp.maximum(l2, 1e-30))
    a = a * lax.rsqrt(jnp.mean(jnp.square(a), axis=0, keepdims=True) + LN_EPS)
    a = a * sg_ref[...] * (1.0 - lam_init)
    o_ref[0] = jnp.concatenate([a[:, r * QB:(r + 1) * QB].T for r in range(DIFF_REP)],
                               axis=-1).astype(o_ref.dtype)


def _heads_on_lanes(table, group):
    h, k, q = table.shape
    return table.reshape(h // group, group, k, q).transpose(0, 2, 1, 3).reshape(h // group, k, group * q)


def diff_prompt(q, k, vt, bias0_t, c31, lam_rows, sub_g_col, lam_init):
    b, gk, _, _, t, _ = q.shape
    tp = k.shape[3]
    w = DIFF_REP * QB
    b0 = _heads_on_lanes(bias0_t, DIFF_REP)
    c31 = _heads_on_lanes(jnp.broadcast_to(c31[:, None, None], (DIFF_HEADS, 1, QB)), DIFF_REP)
    return pl.pallas_call(
        functools.partial(_diff_prompt_kernel, lam_init=lam_init), grid=(b, gk, t // QB),
        in_specs=[pl.BlockSpec((1, 1, DIFF_REP, 2, QB, HEAD_DIM), lambda b_, g_, i: (b_, g_, 0, 0, i, 0)),
                  pl.BlockSpec((1, 1, 2, tp, HEAD_DIM), lambda b_, g_, i: (b_, g_, 0, 0, 0)),
                  pl.BlockSpec((1, 1, 2 * HEAD_DIM, tp), lambda b_, g_, i: (b_, g_, 0, 0)),
                  pl.BlockSpec((1, b0.shape[1], w), lambda b_, g_, i: (g_, 0, 0)),
                  pl.BlockSpec((1, 1, w), lambda b_, g_, i: (g_, 0, 0)),
                  pl.BlockSpec((4, HEAD_DIM), lambda b_, g_, i: (0, 0)),
                  pl.BlockSpec((2 * HEAD_DIM, 1), lambda b_, g_, i: (0, 0))],
        out_specs=pl.BlockSpec((1, QB, DIFF_REP * 2 * HEAD_DIM), lambda b_, g_, i: (b_, i, g_)),
        out_shape=jax.ShapeDtypeStruct((b, t, DIFF_HEADS * 2 * HEAD_DIM), BF16),
        compiler_params=_params(("arbitrary",) * 3), name="diff_prompt")(
            q, k, vt, b0, c31, lam_rows, sub_g_col)


CMP_BAND_BACK = 4
CMP_BAND = 16


def _cmp_band_table(tab):
    first = [-((m - CMP_BAND_BACK) * CMP_BLOCK + CMP_BLOCK - 1) for m in range(CMP_BAND)]
    delta = jnp.stack([_window(tab, d0, QB) for d0 in first], axis=1) - tab[:, -1][:, None, None]
    hi = delta.astype(BF16)
    lo = (delta - hi.astype(F32)).astype(BF16)
    return jnp.stack([hi, lo], axis=1)


def _select_blocks(score_ref, n_rows, n_keep):
    sc = score_ref[...]
    blk = _iota(sc.shape, 0)

    def body(i, rank):
        r = score_ref[pl.ds(i, 1), :]
        better = (r > sc) | ((r == sc) & (i < blk))
        return rank + jnp.where(better, 1.0, 0.0)

    rank = lax.fori_loop(0, n_rows, body, jnp.zeros(sc.shape, F32), unroll=8)
    return rank < n_keep


def _nsa_prompt_kernel(q_ref, kc_ref, vct_ref, ks_ref, vst_ref, kw_ref, vwt_ref, bc_ref, b0_ref, c31_ref,
                       gate_ref, o_ref, impt_ref, score_ref, *, n_sel_blocks, n_keep):
    qi = pl.program_id(2)
    q0 = qi * QB
    n_tiles = qi // (KT // QB) + 1
    nc = kc_ref.shape[2]
    ns = nc // 2
    heads = range(NSA_REP)
    w = NSA_REP * QB
    qcat = jnp.concatenate([(q_ref[0, 0, r].astype(F32) * ATT_SCALE).astype(BF16) for r in heads], axis=0)
    on_lanes = lambda rows: jnp.concatenate(rows, axis=1)

    valid_c = _iota((nc, w), 1) % QB + q0 - (_iota((nc, w), 0) * CMP_BLOCK + (CMP_BLOCK - 1)) >= 0
    first_blk = qi * (QB // CMP_BLOCK) - CMP_BAND_BACK
    place = jnp.where(_iota((nc, CMP_BAND), 0) == _iota((nc, CMP_BAND), 1) + first_blk, 1.0, 0.0).astype(BF16)
    bias = c31_ref[0] + _dot(place, bc_ref[0, 0]) + _dot(place, bc_ref[0, 1])
    p = _softmax_full_t(_dot_nt(kc_ref[0, 0], qcat) + bias, valid_c)
    o_c = _dot(vct_ref[0, 0], p.astype(BF16))
    imp = p[:, :QB]
    for r in range(1, NSA_REP):
        imp = imp + p[:, r * QB:(r + 1) * QB]
    impt_ref[...] = imp
    imp_s = impt_ref[pl.ds(0, ns, stride=2), :] + impt_ref[pl.ds(1, ns, stride=2), :]
    blk = _iota((ns, QB), 0)
    qpos = _iota((ns, QB), 1) + q0
    forced = (blk == qpos // SEL_BLOCK) | (blk == 0)
    future = blk * SEL_BLOCK > qpos
    score = jnp.where(forced, FORCE, jnp.where(future, -FORCE, imp_s))
    score_ref[...] = jnp.where(blk < n_sel_blocks, score, -3e38)
    sel_t = jnp.where(_select_blocks(score_ref, ns, n_keep), 1.0, 0.0).astype(BF16)

    key = _iota((KT, w), 0)
    qry = _iota((KT, w), 1) % QB
    causal0 = qry + KPAD - key >= 0
    key_blk = _iota((KT, ns), 0) // SEL_BLOCK
    blk_e = _iota((KT, ns), 1)

    def branch(k_ref, vt_ref, hi, valid_fn):
        def tile(j, state, first):
            start = pl.multiple_of(q0 - j * KT, QB)
            k = k_ref[0, 0, pl.ds(start, KT), :]
            vt = vt_ref[0, 0, :, pl.ds(start, KT)]
            s = _dot_nt(k, qcat) + (b0_ref[0] if first else c31_ref[0])
            return _softmax_step_t(s, valid_fn(j, start, first), *state, vt)

        init = (jnp.full((1, w), NEG, F32), jnp.zeros((1, w), F32), jnp.zeros((HEAD_DIM, w), F32))
        state = tile(0, init, True)
        _, l, acc = lax.fori_loop(1, hi, lambda j, st: tile(j, st, False), state)
        return acc / jnp.maximum(l, 1e-30)

    def valid_sel(j, start, first):
        base = start // SEL_BLOCK - KPAD // SEL_BLOCK
        expand = jnp.where(blk_e == key_blk + base, 1.0, 0.0).astype(BF16)
        chosen = on_lanes([_dot(expand, sel_t)] * NSA_REP) > 0.5
        return (chosen & causal0) if first else chosen

    def valid_win(j, start, first):
        ok = key >= KPAD - start
        if first:
            return ok & causal0
        return ok & (qry + KPAD - key + j * KT < WINDOW)

    o_s = branch(ks_ref, vst_ref, n_tiles, valid_sel)
    o_w = branch(kw_ref, vwt_ref, jnp.minimum(n_tiles, (WINDOW + QB - 1) // KT + 1), valid_win)

    gate = jax.nn.sigmoid(gate_ref[0, 0].astype(F32))
    g0, g1, g2 = (on_lanes([gate[3 * r + i:3 * r + i + 1, :] for r in heads]) for i in range(3))
    o = g0 * o_c + g1 * o_s + g2 * o_w
    o_ref[0] = jnp.concatenate([o[:, r * QB:(r + 1) * QB] for r in heads], axis=0).T.astype(o_ref.dtype)


def nsa_prompt(q, kc, vct, ks, vst, kw, vwt, bias_c, bias0_t, c31, gates_t, n_sel_blocks):
    b, g, _, t, _ = q.shape
    tp = ks.shape[2]
    nc = kc.shape[2]
    k_spec = pl.BlockSpec((1, 1, tp, HEAD_DIM), lambda b_, g_, i: (b_, g_, 0, 0))
    vt_spec = pl.BlockSpec((1, 1, HEAD_DIM, tp), lambda b_, g_, i: (b_, g_, 0, 0))
    kern = functools.partial(_nsa_prompt_kernel, n_sel_blocks=n_sel_blocks,
                             n_keep=min(N_SEL, n_sel_blocks))
    w = NSA_REP * QB
    b0 = _heads_on_lanes(bias0_t, NSA_REP)
    c31 = _heads_on_lanes(jnp.broadcast_to(c31[:, None, None], (NSA_HEADS, 1, QB)), NSA_REP)
    band = _heads_on_lanes(bias_c.reshape(NSA_HEADS, 2 * CMP_BAND, QB), NSA_REP).reshape(g, 2, CMP_BAND, w)
    return pl.pallas_call(
        kern, grid=(b, g, t // QB),
        in_specs=[pl.BlockSpec((1, 1, NSA_REP, QB, HEAD_DIM), lambda b_, g_, i: (b_, g_, 0, i, 0)),
                  pl.BlockSpec((1, 1, nc, HEAD_DIM), lambda b_, g_, i: (b_, g_, 0, 0)),
                  pl.BlockSpec((1, 1, HEAD_DIM, nc), lambda b_, g_, i: (b_, g_, 0, 0)),
                  k_spec, vt_spec, k_spec, vt_spec,
                  pl.BlockSpec((1, 2, CMP_BAND, w), lambda b_, g_, i: (g_, 0, 0, 0)),
                  pl.BlockSpec((1, KT, w), lambda b_, g_, i: (g_, 0, 0)),
                  pl.BlockSpec((1, 1, w), lambda b_, g_, i: (g_, 0, 0)),
                  pl.BlockSpec((1, 1, 4 * NSA_REP, QB), lambda b_, g_, i: (b_, g_, 0, i))],
        out_specs=pl.BlockSpec((1, QB, NSA_REP * HEAD_DIM), lambda b_, g_, i: (b_, i, g_)),
        out_shape=jax.ShapeDtypeStruct((b, t, NSA_HEADS * HEAD_DIM), BF16),
        scratch_shapes=[pltpu.VMEM((nc, QB), F32), pltpu.VMEM((nc // 2, QB), F32)],
        compiler_params=_params(("arbitrary",) * 3), name="nsa_prompt")(
            q, kc, vct, ks, vst, kw, vwt, band, b0, c31, gates_t)


CMP_PP = 16
CMP_HALF = CMP_BLOCK // 2


def _compress_kernel(pt_ref, w_ref, pe_ref, *rest, pp, positions_on_lanes):
    pages, o_ref, lo_ref, hi_ref = rest[:pp], rest[pp], rest[pp + 1], rest[pp + 2]
    if positions_on_lanes:
        stage_ref = rest[pp + 3]
        for p in range(pp):
            stage_ref[p] = pages[p][...].T
        pages = [stage_ref.at[p] for p in range(pp)]
    width = 2 * HEAD_DIM
    acc = jnp.zeros((pp * SUBLANES, 2 * width), F32)
    for i in range(CMP_HALF):
        rows = [pages[p][pl.ds(i, SUBLANES, stride=CMP_HALF), :] + pe_ref[i] for p in range(pp)]
        acc = acc + _dot(jnp.concatenate(rows, axis=0).astype(BF16), w_ref[0, i])
    lo_ref[...] = acc[:, :width]
    hi_ref[...] = acc[:, width:]
    n = pp * SUBLANES // 2
    o_ref[0] = lo_ref[pl.ds(0, n, stride=2), :] + hi_ref[pl.ds(1, n, stride=2), :]


def compress_pages(page_table, src, layer, block, w_cat, pe_tiles, positions_on_lanes=False):
    nseq, n_pages = page_table.shape
    pp = min(CMP_PP, n_pages)
    width = 2 * HEAD_DIM

    def page_spec(p):
        def index(b, s, kv, pt):
            where = (block + kv, 0) if positions_on_lanes else (0, block + kv)
            return (pt[b, s * pp + p], layer) + where
        return pl.BlockSpec((None, None, PAGE, width), index)

    scratch = [pltpu.VMEM((pp * SUBLANES, width), F32), pltpu.VMEM((pp * SUBLANES, width), F32)]
    if positions_on_lanes:
        scratch.append(pltpu.VMEM((pp, PAGE, width), F32))
    grid_spec = pltpu.PrefetchScalarGridSpec(
        num_scalar_prefetch=1, grid=(nseq, n_pages // pp, 2),
        in_specs=[pl.BlockSpec((1, CMP_HALF, width, 2 * width), lambda b, s, kv, pt: (kv, 0, 0, 0)),
                  pl.BlockSpec((CMP_HALF, SUBLANES, width), lambda b, s, kv, pt: (0, 0, 0))]
        + [page_spec(p) for p in range(pp)],
        out_specs=pl.BlockSpec((1, pp * 4, width), lambda b, s, kv, pt: (b, s, kv)),
        scratch_shapes=scratch)
    return pl.pallas_call(
        functools.partial(_compress_kernel, pp=pp, positions_on_lanes=positions_on_lanes), grid_spec=grid_spec,
        out_shape=jax.ShapeDtypeStruct((nseq, n_pages * 4, 2 * width), F32),
        compiler_params=_params(("arbitrary",) * 3), name="compress")(
            page_table, w_cat, pe_tiles, *([src] * pp))


def _compress_weights(pe, wk, wv):
    def cat(w):
        w = w.reshape(CMP_BLOCK, HEAD_DIM, HEAD_DIM)
        z = jnp.zeros_like(w)
        full = jnp.concatenate([jnp.concatenate([w, z], -1), jnp.concatenate([z, w], -1)], axis=1)
        return jnp.concatenate([full[:CMP_HALF], full[CMP_HALF:]], axis=-1)

    w_cat = jnp.stack([cat(wk), cat(wv)]).astype(BF16)
    pe2 = jnp.tile(pe.astype(F32), (1, 2))
    pe_tiles = jnp.stack([pe2[:CMP_HALF], pe2[CMP_HALF:]], axis=1)
    pe_tiles = jnp.tile(pe_tiles, (1, SUBLANES // 2, 1))
    return w_cat, pe_tiles


SB_PP = 8
DIFF_PP = 8
SEL_PP = 16


def _page_specs(pp, n_pages, rows, row_block, layer):
    def spec(p):
        def index(b, s, pt):
            return (pt[b, n_pages - 1 - (s * pp + p)], layer, row_block, 0)
        return pl.BlockSpec((None, None, rows, LANES), index)
    return [spec(p) for p in range(pp)]


def _positions_on_lanes(cache):
    pool, layers = cache.shape[:2]
    return jnp.transpose(cache, (0, 1, 3, 4, 5, 2)).reshape(pool, layers, -1, PAGE)


def _token_of_row(shape):
    return _iota(shape, 0) % DEC_S


def _sb_sample_kernel(pt_ref, q_ref, new_ref, u_ref, cin_ref, ain_ref, *rest, pp, first):
    pages, (o_ref, cout_ref, aout_ref, carry_ref, acc_ref) = rest[:pp], rest[pp:]
    s = pl.program_id(1)
    q = q_ref[0]
    u = u_ref[...]
    rows = SB_HEADS * DEC_S

    @pl.when(s == 0)
    def _():
        if first:
            k = new_ref[0, :, :SB_W]
            v = new_ref[0, :, SB_W:]
            valid = _iota((rows, PAGE), 1) < _token_of_row((rows, PAGE))
            carry, acc = _sb_step(_dot_nt(q, k), valid, jnp.zeros((rows, 1), F32),
                                  jnp.zeros((rows, SB_W), F32), v, u)
        else:
            carry, acc = cin_ref[0], ain_ref[0]
        carry_ref[...] = carry
        acc_ref[...] = acc

    for p in range(pp):
        @pl.when(jnp.max(carry_ref[...]) > SB_DEAD)
        def _(p=p):
            kt = pages[p][:SB_W, :].astype(BF16)
            vt = pages[p][SB_W:, :].astype(BF16)
            carry, acc = _sb_step(_dot(q, kt), None, carry_ref[...], acc_ref[...], vt, u, v_on_lanes=True)
            carry_ref[...] = carry
            acc_ref[...] = acc

    @pl.when(s == pl.num_programs(1) - 1)
    def _():
        a = acc_ref[...]
        o_ref[0] = jnp.concatenate(
            [a[h * DEC_S:(h + 1) * DEC_S, h * HEAD_DIM:(h + 1) * HEAD_DIM] for h in range(SB_HEADS)],
            axis=-1).astype(o_ref.dtype)
        cout_ref[0] = carry_ref[...]
        aout_ref[0] = a


def _sb_sample_call(page_table, qbd, new_kv, cache, layer, carry, acc, skip, count, first):
    db = page_table.shape[0]
    pp = min(SB_PP, count)
    rows = SB_HEADS * DEC_S
    per_seq = lambda shape: pl.BlockSpec((1,) + shape, lambda b, s, pt: (b,) + (0,) * len(shape))
    grid_spec = pltpu.PrefetchScalarGridSpec(
        num_scalar_prefetch=1, grid=(db, count // pp),
        in_specs=[per_seq((rows, SB_W)), per_seq((PAGE, 2 * SB_W)),
                  pl.BlockSpec((PAGE, PAGE), lambda b, s, pt: (0, 0)), per_seq((rows, 1)), per_seq((rows, SB_W))]
        + _page_specs(pp, page_table.shape[1] - skip, 2 * SB_W, 0, layer),
        out_specs=[per_seq((DEC_S, SB_W)), per_seq((rows, 1)), per_seq((rows, SB_W))],
        scratch_shapes=[pltpu.VMEM((rows, 1), F32), pltpu.VMEM((rows, SB_W), F32)])
    return pl.pallas_call(
        functools.partial(_sb_sample_kernel, pp=pp, first=first), grid_spec=grid_spec,
        out_shape=[jax.ShapeDtypeStruct((db, DEC_S, SB_W), BF16), jax.ShapeDtypeStruct((db, rows, 1), F32),
                   jax.ShapeDtypeStruct((db, rows, SB_W), F32)],
        compiler_params=_params(("arbitrary",) * 2), name="sb_sample")(
            page_table, qbd, new_kv, _later_matrix(PAGE), carry, acc, *([cache] * pp))


def sb_sample(page_table, qbd, new_kv, cache, layer):
    db, n_pages = page_table.shape
    rows = SB_HEADS * DEC_S
    head = min(SB_PP, n_pages)
    zeros = (jnp.zeros((db, rows, 1), F32), jnp.zeros((db, rows, SB_W), F32))
    out, carry, acc = _sb_sample_call(page_table, qbd, new_kv, cache, layer, *zeros, 0, head, True)
    if n_pages == head:
        return out
    earlier = lambda: _sb_sample_call(page_table, qbd, new_kv, cache, layer, carry, acc, head,
                                      n_pages - head, False)[0]
    return lax.cond(jnp.max(carry) > SB_DEAD, earlier, lambda: out)


DIFF_GROUP_ROWS = DIFF_REP * 2 * DEC_S
DIFF_ROW_STRIDE = 2 * DIFF_KV_HEADS


def _diff_sample_tiles(q, loads, bias, valid, m, l, acc):
    heads = range(DIFF_KV_HEADS)
    rows = lambda g: slice(g * DIFF_GROUP_ROWS, (g + 1) * DIFF_GROUP_ROWS)
    s = jnp.concatenate(
        [jnp.concatenate([_dot_nt(q[rows(g)], load(g).astype(BF16)) for g in heads], axis=0)
         for load in loads], axis=1) + bias
    if valid is not None:
        s = jnp.where(valid, s, NEG)
    m_new = jnp.maximum(m, jnp.max(s, axis=-1, keepdims=True))
    p = jnp.exp(s - m_new)
    if valid is not None:
        p = jnp.where(valid, p, 0.0)
    alpha = jnp.exp(m - m_new)
    l = alpha * l + jnp.sum(p, axis=-1, keepdims=True)
    p = p.astype(BF16)
    pv = []
    for g in heads:
        terms = [_dot(p[rows(g), i * PAGE:(i + 1) * PAGE], load(DIFF_KV_HEADS + g).astype(BF16))
                 for i, load in enumerate(loads)]
        pv.append(functools.reduce(lambda a, b: a + b, terms))
    return m_new, l, alpha * acc + jnp.concatenate(pv, axis=0)


def _diff_sample_kernel(pt_ref, q_ref, new_ref, cvec_ref, blast_ref, bnew_ref, lam_ref, sg_ref, *rest,
                        pp, lam_init):
    pages, o_ref, m_ref, l_ref, acc_ref = rest[:pp], rest[pp], rest[pp + 1], rest[pp + 2], rest[pp + 3]
    s = pl.program_id(1)
    q = q_ref[0]
    rows = DIFF_HEADS * 2 * DEC_S
    strided = lambda ref: (lambda off: ref[pl.ds(off, PAGE, stride=DIFF_ROW_STRIDE), :])

    @pl.when(s == 0)
    def _():
        valid = _iota((rows, PAGE), 1) <= _token_of_row((rows, PAGE))
        m, l, acc = _diff_sample_tiles(q, [strided(new_ref)], bnew_ref[...], valid,
                                       jnp.full((rows, 1), NEG, F32), jnp.zeros((rows, 1), F32),
                                       jnp.zeros((rows, 2 * HEAD_DIM), F32))
        m_ref[...] = m
        l_ref[...] = l
        acc_ref[...] = acc

    bias = jnp.broadcast_to(cvec_ref[...], (rows, PAGE))
    bias = jnp.concatenate([jnp.where(s == 0, blast_ref[...], bias)] + [bias] * (pp - 1), axis=1)
    m, l, acc = _diff_sample_tiles(q, [strided(ref) for ref in pages], bias, None,
                                   m_ref[...], l_ref[...], acc_ref[...])
    m_ref[...] = m
    l_ref[...] = l
    acc_ref[...] = acc

    @pl.when(s == pl.num_programs(1) - 1)
    def _():
        a = acc_ref[...]
        ls = l_ref[...]
        lam = _diff_lambda(lam_ref, lam_init)
        outs = []
        for head in range(DIFF_HEADS):
            r0 = head * 2 * DEC_S
            outs.append(_diff_finish(a[r0:r0 + DEC_S], ls[r0:r0 + DEC_S], a[r0 + DEC_S:r0 + 2 * DEC_S],
                                     ls[r0 + DEC_S:r0 + 2 * DEC_S], lam, sg_ref[...], lam_init))
        o_ref[0] = jnp.concatenate(outs, axis=-1).astype(o_ref.dtype)


def diff_sample(page_table, qc, new_kv, cvec, blast, bnew, lam_rows, sub_g, cache, layer, lam_init):
    db, n_pages = page_table.shape
    pp = min(DIFF_PP, n_pages)
    rows = DIFF_HEADS * 2 * DEC_S
    page_rows = PAGE * DIFF_ROW_STRIDE
    const = lambda shape: pl.BlockSpec(shape, lambda b, s, pt: (0,) * len(shape))
    grid_spec = pltpu.PrefetchScalarGridSpec(
        num_scalar_prefetch=1, grid=(db, n_pages // pp),
        in_specs=[pl.BlockSpec((1, rows, 2 * HEAD_DIM), lambda b, s, pt: (b, 0, 0)),
                  pl.BlockSpec((None, page_rows, LANES), lambda b, s, pt: (b, 0, 0)),
                  const((rows, 1)), const((rows, PAGE)), const((rows, PAGE)),
                  const((4, HEAD_DIM)), const((1, 2 * HEAD_DIM))]
        + _page_specs(pp, n_pages, page_rows, 0, layer),
        out_specs=pl.BlockSpec((1, DEC_S, DIFF_HEADS * 2 * HEAD_DIM), lambda b, s, pt: (b, 0, 0)),
        scratch_shapes=[pltpu.VMEM((rows, 1), F32), pltpu.VMEM((rows, 1), F32),
                        pltpu.VMEM((rows, 2 * HEAD_DIM), F32)])
    return pl.pallas_call(
        functools.partial(_diff_sample_kernel, pp=pp, lam_init=lam_init), grid_spec=grid_spec,
        out_shape=jax.ShapeDtypeStruct((db, DEC_S, DIFF_HEADS * 2 * HEAD_DIM), BF16),
        compiler_params=_params(("arbitrary",) * 2), name="diff_sample")(
            page_table, qc, new_kv, cvec, blast, bnew, lam_rows, sub_g, *([cache] * pp))


NSA_ROWS = NSA_HEADS * DEC_S
NSA_GT = NSA_GROUPS * DEC_S


def _nsa_diag(a):
    pieces = []
    for r in range(NSA_REP):
        for g in range(NSA_GROUPS):
            r0 = r * NSA_GT + g * DEC_S
            pieces.append(a[r0:r0 + DEC_S, g * HEAD_DIM:(g + 1) * HEAD_DIM])
    return jnp.concatenate(pieces, axis=0)


def _nsa_sample_cw_kernel(q_ref, kcvc_ref, bc_ref, win_ref, bw_ref, oc_ref, ow_ref, sel_ref,
                          impt_ref, score_ref, *, win_len, n_keep):
    q = q_ref[0]
    nc = kcvc_ref.shape[1]
    ns = nc // 2
    kc = kcvc_ref[0, :, :NSA_KV_W].astype(BF16)
    vc = kcvc_ref[0, :, NSA_KV_W:].astype(BF16)
    p = _softmax_full(_dot_nt(q, kc) + bc_ref[...], None)
    oc_ref[0] = _nsa_diag(_dot(p.astype(BF16), vc))
    imp = p[0:NSA_GT]
    for r in range(1, NSA_REP):
        imp = imp + p[r * NSA_GT:(r + 1) * NSA_GT]
    imp = jnp.concatenate([imp, jnp.zeros((LANES - NSA_GT, nc), F32)], axis=0)
    impt_ref[...] = imp.T
    imp_s = impt_ref[pl.ds(0, ns, stride=2), :] + impt_ref[pl.ds(1, ns, stride=2), :]
    blk = _iota((ns, LANES), 0)
    score_ref[...] = jnp.where(blk == 0, FORCE, imp_s)
    sel_t = jnp.where(_select_blocks(score_ref, ns, n_keep), 1.0, 0.0)
    sel_ref[0] = sel_t.T[:NSA_GT].astype(sel_ref.dtype)

    wl = win_ref.shape[1]
    kw = win_ref[0, :, :NSA_KV_W]
    vw = win_ref[0, :, NSA_KV_W:]
    tok = _token_of_row((NSA_ROWS, wl))
    col = _iota((NSA_ROWS, wl), 1)
    valid = (col > tok + (win_len - WINDOW)) & (col <= tok + win_len) & (col < win_len + DEC_S)
    pw = _softmax_full(_dot_nt(q, kw) + bw_ref[...], valid)
    ow_ref[0] = _nsa_diag(_dot(pw.astype(BF16), vw))


def nsa_sample_cw(qbd, kcvc, bias_c, win_all, bias_w, win_len, n_keep):
    db = qbd.shape[0]
    nc = kcvc.shape[1]
    wl = win_all.shape[1]
    const = lambda shape: pl.BlockSpec(shape, lambda b: (0,) * len(shape))
    return pl.pallas_call(
        functools.partial(_nsa_sample_cw_kernel, win_len=win_len, n_keep=n_keep), grid=(db,),
        in_specs=[pl.BlockSpec((1, NSA_ROWS, NSA_KV_W), lambda b: (b, 0, 0)),
                  pl.BlockSpec((1, nc, 2 * NSA_KV_W), lambda b: (b, 0, 0)),
                  const((NSA_ROWS, nc)),
                  pl.BlockSpec((1, wl, 2 * NSA_KV_W), lambda b: (b, 0, 0)),
                  const((NSA_ROWS, wl))],
        out_specs=[pl.BlockSpec((1, NSA_ROWS, HEAD_DIM), lambda b: (b, 0, 0)),
                   pl.BlockSpec((1, NSA_ROWS, HEAD_DIM), lambda b: (b, 0, 0)),
                   pl.BlockSpec((1, NSA_GT, nc // 2), lambda b: (b, 0, 0))],
        out_shape=[jax.ShapeDtypeStruct((db, NSA_ROWS, HEAD_DIM), F32),
                   jax.ShapeDtypeStruct((db, NSA_ROWS, HEAD_DIM), F32),
                   jax.ShapeDtypeStruct((db, NSA_GT, nc // 2), BF16)],
        scratch_shapes=[pltpu.VMEM((nc, LANES), F32), pltpu.VMEM((nc // 2, LANES), F32)],
        compiler_params=_params(("arbitrary",)), name="nsa_sample_cw")(qbd, kcvc, bias_c, win_all, bias_w)


def _nsa_sample_sel_kernel(pt_ref, q_ref, new_ref, sel_ref, cvec_ref, blast_ref, bnew_ref, oc_ref, ow_ref,
                           gate_ref, *rest, pp, n_pages):
    pages, o_ref, m_ref, l_ref, acc_ref = rest[:pp], rest[pp], rest[pp + 1], rest[pp + 2], rest[pp + 3]
    s = pl.program_id(1)
    q = q_ref[0]
    sel = sel_ref[0]
    ns = sel.shape[1]

    @pl.when(s == 0)
    def _():
        k = new_ref[0, :, :NSA_KV_W]
        v = new_ref[0, :, NSA_KV_W:]
        valid = _iota((NSA_ROWS, PAGE), 1) <= _token_of_row((NSA_ROWS, PAGE))
        m, l, acc = _softmax_step(_dot_nt(q, k) + bnew_ref[...], valid, jnp.full((NSA_ROWS, 1), NEG, F32),
                                  jnp.zeros((NSA_ROWS, 1), F32), jnp.zeros((NSA_ROWS, NSA_KV_W), F32), v)
        m_ref[...] = m
        l_ref[...] = l
        acc_ref[...] = acc

    latest = n_pages - 1 - s * pp
    lane = _iota((ns, pp * PAGE), 1)
    blk_of_lane = (latest - lane // PAGE) * (PAGE // SEL_BLOCK) + (lane % PAGE) // SEL_BLOCK
    expand = jnp.where(_iota((ns, pp * PAGE), 0) == blk_of_lane, 1.0, 0.0).astype(BF16)
    valid = jnp.concatenate([_dot(sel, expand)] * NSA_REP, axis=0) > 0.5
    bias = jnp.broadcast_to(cvec_ref[...], (NSA_ROWS, PAGE))
    bias = jnp.concatenate([jnp.where(s == 0, blast_ref[...], bias)] + [bias] * (pp - 1), axis=1)
    scores = jnp.concatenate([_dot(q, pages[p][:NSA_KV_W, :].astype(BF16)) for p in range(pp)], axis=1)
    scores = jnp.where(valid, scores + bias, NEG)
    m = m_ref[...]
    m_new = jnp.maximum(m, jnp.max(scores, axis=-1, keepdims=True))
    prob = jnp.where(valid, jnp.exp(scores - m_new), 0.0)
    alpha = jnp.exp(m - m_new)
    m_ref[...] = m_new
    l_ref[...] = alpha * l_ref[...] + jnp.sum(prob, axis=-1, keepdims=True)
    prob = prob.astype(BF16)
    pv = [_dot_nt(prob[:, p * PAGE:(p + 1) * PAGE], pages[p][NSA_KV_W:, :].astype(BF16)) for p in range(pp)]
    acc_ref[...] = alpha * acc_ref[...] + functools.reduce(lambda a, b: a + b, pv)

    @pl.when(s == pl.num_programs(1) - 1)
    def _():
        o_s = _nsa_diag(acc_ref[...] / jnp.maximum(l_ref[...], 1e-30))
        gate = jax.nn.sigmoid(gate_ref[0].astype(F32))
        o = gate[:, 0:1] * oc_ref[0] + gate[:, 1:2] * o_s + gate[:, 2:3] * ow_ref[0]
        pieces = []
        for g in range(NSA_GROUPS):
            for r in range(NSA_REP):
                r0 = r * NSA_GT + g * DEC_S
                pieces.append(o[r0:r0 + DEC_S])
        o_ref[0] = jnp.concatenate(pieces, axis=-1).astype(o_ref.dtype)


def nsa_sample_sel(page_table, qbd, new_kv, sel, cvec, blast, bnew, o_c, o_w, gates, cache, layer):
    db, n_pages = page_table.shape
    pp = min(SEL_PP, n_pages)
    ns = sel.shape[2]
    const = lambda shape: pl.BlockSpec(shape, lambda b, s, pt: (0,) * len(shape))
    per_seq = lambda shape: pl.BlockSpec((1,) + shape, lambda b, s, pt: (b,) + (0,) * len(shape))
    grid_spec = pltpu.PrefetchScalarGridSpec(
        num_scalar_prefetch=1, grid=(db, n_pages // pp),
        in_specs=[per_seq((NSA_ROWS, NSA_KV_W)), per_seq((PAGE, 2 * NSA_KV_W)), per_seq((NSA_GT, ns)),
                  const((NSA_ROWS, 1)), const((NSA_ROWS, PAGE)), const((NSA_ROWS, PAGE)),
                  per_seq((NSA_ROWS, HEAD_DIM)), per_seq((NSA_ROWS, HEAD_DIM)), per_seq((NSA_ROWS, LANES))]
        + _page_specs(pp, n_pages, 2 * NSA_KV_W, 1, layer),
        out_specs=pl.BlockSpec((1, DEC_S, NSA_HEADS * HEAD_DIM), lambda b, s, pt: (b, 0, 0)),
        scratch_shapes=[pltpu.VMEM((NSA_ROWS, 1), F32), pltpu.VMEM((NSA_ROWS, 1), F32),
                        pltpu.VMEM((NSA_ROWS, NSA_KV_W), F32)])
    return pl.pallas_call(
        functools.partial(_nsa_sample_sel_kernel, pp=pp, n_pages=n_pages), grid_spec=grid_spec,
        out_shape=jax.ShapeDtypeStruct((db, DEC_S, NSA_HEADS * HEAD_DIM), BF16),
        compiler_params=_params(("arbitrary",) * 2), name="nsa_sample_sel")(
            page_table, qbd, new_kv, sel, cvec, blast, bnew, o_c, o_w, gates, *([cache] * pp))


def _layer_norm(y, g, b):
    mu = jnp.mean(y, axis=-1, keepdims=True)
    d = y - mu
    var = jnp.mean(d * d, axis=-1, keepdims=True)
    return d * lax.rsqrt(var + LN_EPS) * g + b


def _split_bf16(x):
    hi = x.astype(BF16)
    return hi, (x - hi.astype(F32)).astype(BF16)


def _mix_router_kernel(a1_ref, a2_ref, w1_ref, w2_ref, x_ref, g_ref, b_ref, wr_ref, br_ref, u_ref,
                       xn_ref, xb_ref, rt_ref, gt_ref, cnt_ref, *, alpha):
    mix = _dot(a1_ref[...], w1_ref[...]) + _dot(a2_ref[...], w2_ref[...])
    xn = _layer_norm(alpha * x_ref[...] + mix, g_ref[...], b_ref[...])
    xn_ref[...] = xn
    xb_ref[...] = xn.astype(BF16)
    xh, xl = _split_bf16(xn)
    wh, wl = _split_bf16(wr_ref[...])
    logits = _dot_nt(wh, xh) + _dot_nt(wh, xl) + _dot_nt(wl, xh) + br_ref[...]
    e_iota = _iota(logits.shape, 0)
    work = logits
    vals, sels = [], []
    for _ in range(TOP_K):
        top = jnp.max(work, axis=0, keepdims=True)
        idx = jnp.min(jnp.where(work == top, e_iota, N_EXPERTS), axis=0, keepdims=True)
        sel = e_iota == idx
        vals.append(top)
        sels.append(sel)
        work = jnp.where(sel, -jnp.inf, work)
    ex = [jnp.exp(v - vals[0]) for v in vals]
    den = ex[0] + ex[1] + ex[2] + ex[3]
    gate = jnp.zeros(logits.shape, F32)
    chosen = sels[0]
    for k in range(TOP_K):
        gate = jnp.where(sels[k], ex[k] / den, gate)
        chosen = chosen | sels[k]
    cf = jnp.where(chosen, 1.0, 0.0)
    rank = _dot(cf.astype(BF16), u_ref[...])
    rt_ref[0] = jnp.where(chosen, rank, -1.0)
    gt_ref[0] = gate
    cnt = jnp.sum(cf, axis=1, keepdims=True)
    cnt_ref[0] = jnp.broadcast_to(cnt, (N_EXPERTS, LANES)).astype(jnp.int32)


def mix_router(a1, a2, w1, w2, x, ln_g, ln_b, w_router_t, b_router, tt, alpha):
    n = x.shape[0]
    half = a1.shape[1]
    j = np.arange(tt)
    before = jnp.asarray((j[:, None] < j[None, :]).astype(np.float32), dtype=BF16)
    tile = lambda w: pl.BlockSpec((tt, w), lambda i: (i, 0))
    const = lambda shape: pl.BlockSpec(shape, lambda i: (0,) * len(shape))
    route = pl.BlockSpec((1, N_EXPERTS, tt), lambda i: (i, 0, 0))
    return pl.pallas_call(
        functools.partial(_mix_router_kernel, alpha=alpha), grid=(n // tt,),
        in_specs=[tile(half), tile(half), const((half, D_MODEL)), const((half, D_MODEL)), tile(D_MODEL),
                  const((1, D_MODEL)), const((1, D_MODEL)), const((N_EXPERTS, D_MODEL)),
                  const((N_EXPERTS, 1)), const((tt, tt))],
        out_specs=[tile(D_MODEL), tile(D_MODEL), route, route,
                   pl.BlockSpec((1, N_EXPERTS, LANES), lambda i: (i, 0, 0))],
        out_shape=[jax.ShapeDtypeStruct((n, D_MODEL), F32), jax.ShapeDtypeStruct((n, D_MODEL), BF16),
                   jax.ShapeDtypeStruct((n // tt, N_EXPERTS, tt), F32),
                   jax.ShapeDtypeStruct((n // tt, N_EXPERTS, tt), F32),
                   jax.ShapeDtypeStruct((n // tt, N_EXPERTS, LANES), jnp.int32)],
        compiler_params=_params(("arbitrary",)), name="mix_router")(
            a1, a2, w1, w2, x, ln_g, ln_b, w_router_t, b_router, before)


def _pair_permutation():
    c = np.arange(2 * LANES)
    dest = np.where(c % 2 == 0, c // 2, LANES + c // 2)
    return jnp.asarray((dest[:, None] == c[None, :]).astype(np.float32), dtype=BF16)


def _split_pairs_kernel(w_ref, s_ref, o_ref):
    s = s_ref[...]
    for blk in range(w_ref.shape[2] // (2 * LANES)):
        cols = slice(blk * 2 * LANES, (blk + 1) * 2 * LANES)
        o_ref[0, :, cols] = _dot(w_ref[0, :, cols].astype(BF16), s).astype(BF16)


def split_pairs(w_up):
    e, d, f2 = w_up.shape
    return pl.pallas_call(
        _split_pairs_kernel, grid=(e,),
        in_specs=[pl.BlockSpec((1, d, f2), lambda i: (i, 0, 0)),
                  pl.BlockSpec((2 * LANES, 2 * LANES), lambda i: (0, 0))],
        out_specs=pl.BlockSpec((1, d, f2), lambda i: (i, 0, 0)),
        out_shape=jax.ShapeDtypeStruct((e, d, f2), BF16),
        compiler_params=_params(("arbitrary",)), name="split_pairs")(w_up, _pair_permutation())


def _moe_kernel(cnt_ref, xb_ref, xn_ref, rt_ref, gt_ref, wu_ref, bu_ref, wd_ref, bd_ref,
                lg_ref, lb_ref, y_ref, yb_ref, acc_ref, *, alpha):
    i = pl.program_id(0)
    e = pl.program_id(1)
    tt = xb_ref.shape[0]

    @pl.when(e == 0)
    def _():
        acc_ref[...] = jnp.zeros_like(acc_ref)

    n_chunks = (cnt_ref[i, e] + MOE_CH - 1) // MOE_CH
    slot = rt_ref[0, pl.ds(e, 1), :]
    gate = gt_ref[0, pl.ds(e, 1), :]

    def chunk(c):
        want = (_iota((MOE_CH, tt), 0) + c * MOE_CH).astype(F32)
        hit = slot == want
        xe = _dot(jnp.where(hit, 1.0, 0.0).astype(BF16), xb_ref[...]).astype(BF16)
        h = _dot(xe, wu_ref[0]) + bu_ref[0]
        acts = []
        for blk in range(h.shape[1] // (2 * LANES)):
            h_glu = jnp.minimum(h[:, blk * 2 * LANES:blk * 2 * LANES + LANES], SWIGLU_LIMIT)
            h_lin = jnp.clip(h[:, blk * 2 * LANES + LANES:(blk + 1) * 2 * LANES], -SWIGLU_LIMIT, SWIGLU_LIMIT)
            acts.append((h_glu * jax.nn.sigmoid(SWIGLU_ALPHA * h_glu) * (h_lin + 1.0)).astype(BF16))
        y = _dot(jnp.concatenate(acts, axis=-1), wd_ref[0]) + bd_ref[0]
        return y.astype(BF16), jnp.where(hit, gate, 0.0).astype(BF16)

    def scatter_back(c, _):
        y, back = chunk(c)
        acc_ref[...] += _dot_tn(back, y)
        return 0

    lax.fori_loop(0, n_chunks, scatter_back, 0)

    @pl.when(e == pl.num_programs(1) - 1)
    def _():
        y = _layer_norm(alpha * xn_ref[...] + acc_ref[...], lg_ref[...], lb_ref[...])
        y_ref[...] = y
        yb_ref[...] = y.astype(BF16)


def moe_ln(cnt, xb, xn, rt, gt, w_up, b_up, w_down, b_down, ln_g, ln_b, tt, alpha):
    n = xn.shape[0]
    ff = w_down.shape[1]
    tile = pl.BlockSpec((tt, D_MODEL), lambda i, e, c: (i, 0))
    route = pl.BlockSpec((1, N_EXPERTS, tt), lambda i, e, c: (i, 0, 0))
    per_e = lambda shape: pl.BlockSpec((1,) + shape, lambda i, e, c: (e, 0, 0))
    const = pl.BlockSpec((1, D_MODEL), lambda i, e, c: (0, 0))
    grid_spec = pltpu.PrefetchScalarGridSpec(
        num_scalar_prefetch=1, grid=(n // tt, N_EXPERTS),
        in_specs=[tile, tile, route, route, per_e((D_MODEL, 2 * ff)), per_e((1, 2 * ff)),
                  per_e((ff, D_MODEL)), per_e((1, D_MODEL)), const, const],
        out_specs=[tile, tile],
        scratch_shapes=[pltpu.VMEM((tt, D_MODEL), F32)])
    return pl.pallas_call(
        functools.partial(_moe_kernel, alpha=alpha), grid_spec=grid_spec,
        out_shape=[jax.ShapeDtypeStruct((n, D_MODEL), F32), jax.ShapeDtypeStruct((n, D_MODEL), BF16)],
        compiler_params=_params(("arbitrary",) * 2), name="moe_ln")(
            cnt, xb, xn, rt, gt, w_up, b_up, w_down, b_down, ln_g, ln_b)


MOE_TILE = 1024


def _split_pairs_bias(b_up):
    e, f2 = b_up.shape
    b = b_up.astype(F32).reshape(e, f2 // (2 * LANES), LANES, 2).transpose(0, 1, 3, 2)
    return b.reshape(e, 1, f2)


def _front_pad(a, axis, rows=KPAD):
    pad = [(0, 0)] * a.ndim
    pad[axis] = (rows, 0)
    return jnp.pad(a, pad)


def _first_tile_bias(tab, kt):
    by_u = _hankel(_window(tab, -(QB - 1), QB + kt - 1), QB, kt)
    return jnp.flip(by_u, axis=2).transpose(0, 2, 1)


def _pad_rows(a, rows):
    return jnp.pad(a, ((0, 0), (0, rows - a.shape[1]), (0, 0)))


def _channel_mixer(layer, a1, a2, w_out, x, p, tt):
    alpha = p["alpha"]
    half = a1.shape[1]
    w_out = w_out.astype(BF16)
    xn, xb, rt, gt, cnt = mix_router(
        a1, a2, w_out[:half], w_out[half:], x, p["ln_mix_g"][layer][None], p["ln_mix_b"][layer][None],
        p["w_router"][layer].T.astype(F32), p["b_router"][layer][:, None].astype(F32), tt, alpha)
    return moe_ln(cnt[:, :, 0], xb, xn, rt, gt, *p["experts"][layer],
                  p["ln_ffn_g"][layer][None], p["ln_ffn_b"][layer][None], tt, alpha)


def _even_prompt(h, b, t, cmp_w, bias_c, bias0, c31):
    h3 = h.reshape(b, t, -1)
    heads = lambda a, nh: a.reshape(b, t, nh, HEAD_DIM).transpose(0, 2, 1, 3).astype(BF16)
    heads_t = lambda a, nh: a.reshape(b, t, nh, HEAD_DIM).transpose(0, 2, 3, 1).astype(BF16)
    o = 3 * SB_W
    o_sb = sb_prompt(heads(h3[..., :SB_W], SB_HEADS),
                     _front_pad(heads(h3[..., SB_W:2 * SB_W], SB_HEADS), 2),
                     _front_pad(heads_t(h3[..., 2 * SB_W:o], SB_HEADS), 3))
    o_sb = o_sb.transpose(0, 2, 1, 3).reshape(b * t, SB_W)
    nq = h3[..., o:o + NSA_Q_W].reshape(b, t, NSA_GROUPS, NSA_REP, HEAD_DIM).transpose(0, 2, 3, 1, 4)
    o += NSA_Q_W
    nkv = h3[..., o:o + 6 * NSA_KV_W].reshape(b, t, 6, NSA_GROUPS, HEAD_DIM)
    gates = h3[..., o + 6 * NSA_KV_W:o + 6 * NSA_KV_W + NSA_GATE_W]
    gates = gates.reshape(b, t, NSA_GROUPS, NSA_REP * 3).transpose(0, 2, 3, 1)
    gates = jnp.pad(gates, ((0, 0), (0, 0), (0, NSA_REP), (0, 0)))
    n_pages = t // PAGE
    table = jnp.arange(b * n_pages, dtype=jnp.int32).reshape(b, n_pages)
    kcvc = compress_pages(table, h.reshape(b * n_pages, 1, PAGE, h.shape[-1]), 0, o // (2 * HEAD_DIM), *cmp_w)
    nc = t // CMP_BLOCK
    ncp = -(-nc // LANES) * LANES
    kcvc = kcvc.reshape(b, nc, 2, NSA_GROUPS, HEAD_DIM).astype(BF16)
    kc = jnp.pad(kcvc[:, :, 0].transpose(0, 2, 1, 3), ((0, 0), (0, 0), (0, ncp - nc), (0, 0)))
    vct = jnp.pad(kcvc[:, :, 1].transpose(0, 2, 3, 1), ((0, 0), (0, 0), (0, 0), (0, ncp - nc)))
    grp = lambda i: _front_pad(nkv[:, :, i].transpose(0, 2, 1, 3).astype(BF16), 2)
    grp_t = lambda i: _front_pad(nkv[:, :, i].transpose(0, 2, 3, 1).astype(BF16), 3)
    o_nsa = nsa_prompt(nq.astype(BF16), kc, vct, grp(2), grp_t(3), grp(4), grp_t(5),
                       bias_c, bias0, c31, gates, -(-t // SEL_BLOCK))
    sb_rows = h3[..., SB_W:3 * SB_W].reshape(b, t, 2, SB_HEADS, HEAD_DIM)
    keep = min(WINDOW, t)
    return (o_sb, o_nsa.reshape(b * t, NSA_Q_W), sb_rows, nkv[:, :, :4], nkv[:, t - keep:, 4:])


def _even_sample(h, db, page_table, cache_sb, cache_nsa, win_state, layer, cmp_w, tabs):
    h3 = h.reshape(db, DEC_S, -1)
    past = page_table.shape[1] * PAGE
    o = 3 * SB_W
    q = h3[..., :SB_W].reshape(db, DEC_S, SB_HEADS, HEAD_DIM).transpose(0, 2, 1, 3) * ATT_SCALE
    eye = jnp.eye(SB_HEADS, dtype=F32)
    qbd = (q[:, :, :, None, :] * eye[None, :, None, :, None]).reshape(db, SB_HEADS * DEC_S, SB_W)
    o_sb = sb_sample(page_table, qbd.astype(BF16), _pad_rows(h3[..., SB_W:o], PAGE).astype(BF16),
                     cache_sb, layer)
    q = h3[..., o:o + NSA_Q_W].reshape(db, DEC_S, NSA_GROUPS, NSA_REP, HEAD_DIM).transpose(0, 3, 2, 1, 4)
    eye = jnp.eye(NSA_GROUPS, dtype=F32)
    qbd = (q[:, :, :, :, None, :] * ATT_SCALE * eye[None, None, :, None, :, None])
    qbd = qbd.reshape(db, NSA_ROWS, NSA_KV_W).astype(BF16)
    o += NSA_Q_W
    nkv = h3[..., o:o + 6 * NSA_KV_W]
    gates = h3[..., o + 6 * NSA_KV_W:o + 6 * NSA_KV_W + NSA_GATE_W]
    gates = gates.reshape(db, DEC_S, NSA_GROUPS, NSA_REP, 3).transpose(0, 3, 2, 1, 4).reshape(db, NSA_ROWS, 3)
    gates = jnp.pad(gates, ((0, 0), (0, 0), (0, LANES - 3)))
    kcvc = compress_pages(page_table, cache_nsa, layer, 0, *cmp_w, positions_on_lanes=True)
    win_len = win_state.shape[2]
    new_win = nkv[..., 4 * NSA_KV_W:]
    win_all = jnp.concatenate([win_state[:, layer].reshape(db, win_len, 2 * NSA_KV_W), new_win], axis=1)
    wl = tabs["bias_w"].shape[1]
    n_blocks = -(-(past + DEC_S) // SEL_BLOCK)
    o_c, o_w, sel = nsa_sample_cw(qbd, kcvc, tabs["bias_c"], _pad_rows(win_all, wl).astype(BF16),
                                  tabs["bias_w"], win_len, min(N_SEL, n_blocks) - 1)
    o_nsa = nsa_sample_sel(page_table, qbd, _pad_rows(nkv[..., 2 * NSA_KV_W:4 * NSA_KV_W], PAGE).astype(BF16),
                           sel, tabs["cvec"], tabs["blast"], tabs["bnew"], o_c, o_w, gates, cache_nsa, layer)
    sb_rows = h3[..., SB_W:3 * SB_W].reshape(db, DEC_S, 2, SB_HEADS, HEAD_DIM)
    nsa_rows = nkv[..., :4 * NSA_KV_W].reshape(db, DEC_S, 4, NSA_GROUPS, HEAD_DIM)
    win_rows = win_all[:, max(0, win_len + DEC_S - WINDOW):].reshape(db, -1, 2, NSA_GROUPS, HEAD_DIM)
    return (o_sb.reshape(db * DEC_S, SB_W), o_nsa.reshape(db * DEC_S, NSA_Q_W), sb_rows, nsa_rows, win_rows)


def _odd_prompt(h, b, t, bias0, c31, lam_rows, sub_g, lam_init):
    h3 = h.reshape(b, t, -1)
    q = h3[..., :DIFF_Q_W].reshape(b, t, DIFF_KV_HEADS, DIFF_REP, 2, HEAD_DIM).transpose(0, 2, 3, 4, 1, 5)
    kv = h3[..., DIFF_Q_W:].reshape(b, t, 2, DIFF_KV_HEADS, 2 * HEAD_DIM)
    k = kv[:, :, 0].reshape(b, t, DIFF_KV_HEADS, 2, HEAD_DIM).transpose(0, 2, 3, 1, 4)
    vt = kv[:, :, 1].transpose(0, 2, 3, 1)
    kpad = bias0.shape[1] - QB
    o = diff_prompt(q.astype(BF16), _front_pad(k.astype(BF16), 3, kpad), _front_pad(vt.astype(BF16), 3, kpad),
                    bias0, c31, lam_rows, sub_g.T, lam_init)
    return o.reshape(b * t, DIFF_Q_W), kv


def _odd_sample(h, db, page_table, cache, layer, tabs, lam_rows, sub_g, lam_init):
    h3 = h.reshape(db, DEC_S, -1)
    q = h3[..., :DIFF_Q_W].reshape(db, DEC_S, DIFF_KV_HEADS, DIFF_REP, 2, HEAD_DIM).transpose(0, 2, 3, 4, 1, 5)
    eye_c = jnp.eye(2, dtype=F32)
    qc = q[..., None, :] * ATT_SCALE * eye_c[None, None, None, :, None, :, None]
    qc = qc.reshape(db, DIFF_HEADS * 2 * DEC_S, 2 * HEAD_DIM).astype(BF16)
    kv = h3[..., DIFF_Q_W:].reshape(db, DEC_S, 2, DIFF_KV_HEADS, 2 * HEAD_DIM)
    new_kv = _pad_rows(kv.reshape(db, DEC_S * DIFF_ROW_STRIDE, 2 * HEAD_DIM), PAGE * DIFF_ROW_STRIDE)
    o = diff_sample(page_table, qc, new_kv, tabs["cvec"], tabs["blast"], tabs["bnew"], lam_rows, sub_g,
                    cache, layer, lam_init)
    return o.reshape(db * DEC_S, DIFF_Q_W), kv


def _sample_tables(tab, past, arrange, with_cmp):
    def table(first, n, step=1):
        rows = [jnp.flip(_window(tab, first(t) - step * (n - 1), n, step), axis=1) for t in range(DEC_S)]
        return arrange(jnp.stack(rows, axis=1))

    tabs = {
        "cvec": arrange(jnp.broadcast_to(tab[:, -1][:, None, None], (tab.shape[0], DEC_S, 1))),
        "blast": table(lambda t: t + PAGE, PAGE),
        "bnew": table(lambda t: t, PAGE),
    }
    if with_cmp:
        nc = past // CMP_BLOCK
        tabs["bias_c"] = table(lambda t: past + t - (CMP_BLOCK - 1), nc, CMP_BLOCK)
        win_len = min(WINDOW, past)
        wl = -(-(win_len + DEC_S) // LANES) * LANES
        tabs["bias_w"] = table(lambda t: t + win_len, wl)
    return tabs


def kernel(x_prompt, x_sample, cache_sb_kv, cache_nsa_kv, cache_diff_kv, state_nsa_win, page_table, rel_bias,
           even_w_in, even_cmp_pe, even_cmp_wk, even_cmp_wv, even_w_out, odd_w_in, odd_lambda, odd_subln_g,
           odd_w_out, ln_mix_g, ln_mix_b, ln_ffn_g, ln_ffn_b, moe_w_router, moe_b_router, moe_w_up, moe_b_up,
           moe_w_down, moe_b_down):
    b, t, d = x_prompt.shape
    db, s, _ = x_sample.shape
    depth = ln_mix_g.shape[0]
    n_pool = cache_sb_kv.shape[0]
    past = page_table.shape[1] * PAGE
    assert s == DEC_S and d == D_MODEL and t % QB == 0 and (b * t) % MOE_TILE == 0
    assert past % (SEL_BLOCK * LANES) == 0 and state_nsa_win.shape[2] == WINDOW

    params = {
        "alpha": (2.0 * depth) ** 0.25,
        "ln_mix_g": ln_mix_g.astype(F32), "ln_mix_b": ln_mix_b.astype(F32),
        "ln_ffn_g": ln_ffn_g.astype(F32), "ln_ffn_b": ln_ffn_b.astype(F32),
        "w_router": moe_w_router, "b_router": moe_b_router,
        "experts": [(split_pairs(moe_w_up[l]), _split_pairs_bias(moe_b_up[l]),
                     moe_w_down[l].astype(BF16), moe_b_down[l][:, None, :].astype(F32)) for l in range(depth)],
    }
    tab = _distance_table(rel_bias)
    c31 = tab[:, -1]
    bias0 = _first_tile_bias(tab, KT)
    bias0_diff = _first_tile_bias(tab, DIFF_KT)
    bias_c = _cmp_band_table(tab)
    nsa_arrange = lambda x: x.reshape(NSA_GROUPS, NSA_REP, DEC_S, -1).transpose(1, 0, 2, 3).reshape(NSA_ROWS, -1)
    diff_arrange = lambda x: jnp.broadcast_to(x[:, None], (DIFF_HEADS, 2) + x.shape[1:]).reshape(
        DIFF_HEADS * 2 * DEC_S, -1)
    nsa_tabs = _sample_tables(tab, past, nsa_arrange, True)
    diff_tabs = _sample_tables(tab, past, diff_arrange, False)

    cache_sb = _positions_on_lanes(cache_sb_kv)
    cache_nsa = _positions_on_lanes(cache_nsa_kv)
    cache_diff = cache_diff_kv.reshape(n_pool, -1, PAGE * DIFF_ROW_STRIDE, 2 * HEAD_DIM)

    xp = x_prompt.reshape(b * t, d).astype(F32)
    xs = x_sample.reshape(db * s, d).astype(F32)
    xpb, xsb = xp.astype(BF16), xs.astype(BF16)
    outs = {k: [] for k in ("sb_p", "sb_s", "nsa_p", "nsa_s", "win_p", "win_s", "diff_p", "diff_s")}
    for l in range(depth):
        j = l // 2
        if l % 2 == 0:
            w_in = jnp.pad(even_w_in[j], ((0, 0), (0, EVEN_IN_PAD - EVEN_IN))).astype(BF16)
            cmp_w = _compress_weights(even_cmp_pe[j], even_cmp_wk[j], even_cmp_wv[j])
            a1, a2, r_sb, r_nsa, r_win = _even_prompt(matmul(xpb, w_in, 512), b, t, cmp_w, bias_c, bias0, c31)
            s1, s2, s_sb, s_nsa, s_win = _even_sample(matmul(xsb, w_in, 512), db, page_table, cache_sb,
                                                      cache_nsa, state_nsa_win, j, cmp_w, nsa_tabs)
            w_out = even_w_out[j]
            for key, val in (("sb_p", r_sb), ("sb_s", s_sb), ("nsa_p", r_nsa), ("nsa_s", s_nsa),
                             ("win_p", r_win), ("win_s", s_win)):
                outs[key].append(val)
        else:
            lam_init = 0.8 - 0.6 * math.exp(-0.3 * l)
            w_in = odd_w_in[j].astype(BF16)
            lam_rows = odd_lambda[j].astype(F32)
            sub_g = odd_subln_g[j][None].astype(F32)
            ap, r_diff = _odd_prompt(matmul(xpb, w_in, 512), b, t, bias0_diff, c31, lam_rows, sub_g, lam_init)
            as_, s_diff = _odd_sample(matmul(xsb, w_in, 512), db, page_table, cache_diff, j, diff_tabs,
                                      lam_rows, sub_g, lam_init)
            half = DIFF_Q_W // 2
            a1, a2, s1, s2 = ap[:, :half], ap[:, half:], as_[:, :half], as_[:, half:]
            w_out = odd_w_out[j]
            outs["diff_p"].append(r_diff)
            outs["diff_s"].append(s_diff)
        xp, xpb = _channel_mixer(l, a1, a2, w_out, xp, params, MOE_TILE)
        xs, xsb = _channel_mixer(l, s1, s2, w_out, xs, params, db * s)
    stack = lambda key: jnp.stack(outs[key], axis=1)
    return (xp.reshape(b, t, d), xs.reshape(db, s, d), stack("sb_p"), stack("sb_s"), stack("nsa_p"),
            stack("nsa_s"), stack("diff_p"), stack("diff_s"), stack("win_p"), stack("win_s"))
```

```python
import functools
import math

import numpy as np
import jax
import jax.numpy as jnp
from jax import lax
from jax.experimental import pallas as pl
from jax.experimental.pallas import tpu as pltpu

F32 = jnp.float32
BF16 = jnp.bfloat16

D_MODEL = 1024
HEAD_DIM = 64
SB_HEADS = 8
NSA_HEADS = 8
NSA_GROUPS = 2
NSA_REP = NSA_HEADS // NSA_GROUPS
CMP_BLOCK = 32
SEL_BLOCK = 64
N_SEL = 16
WINDOW = 512
DIFF_HEADS = 8
DIFF_KV_HEADS = 4
DIFF_REP = DIFF_HEADS // DIFF_KV_HEADS
N_BUCKETS = 32
MAX_DISTANCE = 128
N_EXPERTS = 32
TOP_K = 4
SWIGLU_ALPHA = 1.702
SWIGLU_LIMIT = 7.0
LN_EPS = 1e-5
PAGE = 128
NEG = -1e30
FORCE = 1e9
ATT_SCALE = HEAD_DIM ** -0.5

SB_W = SB_HEADS * HEAD_DIM
NSA_Q_W = NSA_HEADS * HEAD_DIM
NSA_KV_W = NSA_GROUPS * HEAD_DIM
NSA_GATE_W = NSA_HEADS * 3
EVEN_IN = 3 * SB_W + NSA_Q_W + 6 * NSA_KV_W + NSA_GATE_W
EVEN_IN_PAD = -(-EVEN_IN // 128) * 128
DIFF_Q_W = DIFF_HEADS * 2 * HEAD_DIM
DIFF_KV_W = DIFF_KV_HEADS * 2 * HEAD_DIM
ODD_IN = DIFF_Q_W + 2 * DIFF_KV_W

LANES = 128
SUBLANES = 8
VMEM_LIMIT = 52 * 1024 * 1024

QB = 128
KT = 512
KPAD = KT - QB
DIFF_KT = 1024
MOE_CH = 160
DEC_S = 8


def _dot(a, b):
    return jnp.dot(a, b, preferred_element_type=F32)


def _dot_nt(a, b):
    return lax.dot_general(a, b, (((1,), (1,)), ((), ())), preferred_element_type=F32)


def _dot_tn(a, b):
    return lax.dot_general(a, b, (((0,), (0,)), ((), ())), preferred_element_type=F32)


def _iota(shape, dim):
    return lax.broadcasted_iota(jnp.int32, shape, dim)


def _params(sem, vmem=VMEM_LIMIT):
    return pltpu.CompilerParams(dimension_semantics=sem, vmem_limit_bytes=vmem)


def _bucket_np(dist):
    n = np.maximum(dist, 0)
    exact = N_BUCKETS // 2
    nf = np.maximum(n, 1).astype(np.float32)
    large = exact + (np.log(nf / np.float32(exact)) / np.float32(math.log(MAX_DISTANCE / exact))
                     * np.float32(N_BUCKETS - exact)).astype(np.int32)
    return np.where(n < exact, n, np.minimum(large, N_BUCKETS - 1)).astype(np.int32)


DIST_TABLE = 256


def _distance_table(rel_bias):
    onehot = np.eye(N_BUCKETS, dtype=np.float32)[_bucket_np(np.arange(DIST_TABLE))]
    return jnp.dot(jnp.asarray(onehot), rel_bias.astype(F32), precision=lax.Precision.HIGHEST).T


def _window(tab, start, n, step=1):
    stop = start + step * (n - 1)
    lo, hi = max(0, -start), max(0, stop + 1 - tab.shape[1])
    ext = jnp.pad(tab, ((0, 0), (lo, hi)), mode="edge")
    return ext[:, start + lo:stop + lo + 1:step]


def _hankel(v, rows, cols):
    n = rows + cols - 1
    flat = jnp.tile(v, (1, rows + 1))[:, :rows * (n + 1)]
    return flat.reshape(v.shape[0], rows, n + 1)[:, :, :cols]


def _mm_kernel(x_ref, w_ref, o_ref):
    o_ref[...] = _dot(x_ref[...], w_ref[...])


def matmul(x, w, tm):
    m, k = x.shape
    n = w.shape[1]
    tm = min(tm, m)
    return pl.pallas_call(
        _mm_kernel, grid=(m // tm,),
        in_specs=[pl.BlockSpec((tm, k), lambda i: (i, 0)), pl.BlockSpec((k, n), lambda i: (0, 0))],
        out_specs=pl.BlockSpec((tm, n), lambda i: (i, 0)),
        out_shape=jax.ShapeDtypeStruct((m, n), F32),
        compiler_params=_params(("arbitrary",)), name="in_proj")(x, w)


def _softmax_step(s, valid, m, l, acc, v, v_on_lanes=False):
    if valid is not None:
        s = jnp.where(valid, s, NEG)
    m_new = jnp.maximum(m, jnp.max(s, axis=-1, keepdims=True))
    p = jnp.exp(s - m_new)
    if valid is not None:
        p = jnp.where(valid, p, 0.0)
    alpha = jnp.exp(m - m_new)
    l = alpha * l + jnp.sum(p, axis=-1, keepdims=True)
    pv = _dot_nt(p.astype(BF16), v) if v_on_lanes else _dot(p.astype(BF16), v)
    return m_new, l, alpha * acc + pv


def _softmax_full(s, valid):
    if valid is not None:
        s = jnp.where(valid, s, NEG)
    m = jnp.max(s, axis=-1, keepdims=True)
    e = jnp.exp(s - m)
    if valid is not None:
        e = jnp.where(valid, e, 0.0)
    return e / jnp.maximum(jnp.sum(e, axis=-1, keepdims=True), 1e-30)


def _sb_step(z, valid, carry, acc, v, u, v_on_lanes=False):
    sp = jnp.maximum(z, 0.0) + jnp.log(1.0 + jnp.exp(-jnp.abs(z)))
    lk = -sp if valid is None else jnp.where(valid, -sp, 0.0)
    hi = lk.astype(BF16)
    lo = (lk - hi.astype(F32)).astype(BF16)
    later = _dot(hi, u) + _dot(lo, u) + carry
    w = jnp.exp(z - sp + later)
    if valid is not None:
        w = jnp.where(valid, w, 0.0)
    acc = acc + (_dot_nt(w.astype(BF16), v) if v_on_lanes else _dot(w.astype(BF16), v))
    carry = carry + jnp.sum(lk, axis=-1, keepdims=True)
    return carry, acc


def _softmax_step_t(s, valid, m, l, acc, vt):
    if valid is not None:
        s = jnp.where(valid, s, NEG)
    m_new = jnp.maximum(m, jnp.max(s, axis=0, keepdims=True))
    p = jnp.exp(s - m_new)
    if valid is not None:
        p = jnp.where(valid, p, 0.0)
    alpha = jnp.exp(m - m_new)
    l = alpha * l + jnp.sum(p, axis=0, keepdims=True)
    acc = alpha * acc + _dot(vt, p.astype(BF16))
    return m_new, l, acc


def _softmax_full_t(s, valid):
    s = jnp.where(valid, s, NEG)
    m = jnp.max(s, axis=0, keepdims=True)
    e = jnp.where(valid, jnp.exp(s - m), 0.0)
    return e / jnp.maximum(jnp.sum(e, axis=0, keepdims=True), 1e-30)


def _sb_step_t(z, valid, carry, acc, vt, lm):
    nq = z.shape[1]
    sp = jnp.maximum(z, 0.0) + jnp.log(1.0 + jnp.exp(-jnp.abs(z)))
    lk = jnp.where(valid, -sp, 0.0)
    hi = lk.astype(BF16)
    lo = (lk - hi.astype(F32)).astype(BF16)
    both = _dot(lm, jnp.concatenate([hi, lo], axis=1))
    later = both[:, :nq] + both[:, nq:] + carry
    w = jnp.where(valid, jnp.exp(z - sp + later), 0.0)
    acc = acc + _dot(vt, w.astype(BF16))
    carry = carry + jnp.sum(lk, axis=0, keepdims=True)
    return carry, acc


SB_DEAD = -104.0


def _later_matrix(n):
    j = np.arange(n)
    return jnp.asarray((j[:, None] > j[None, :]).astype(np.float32), dtype=BF16)


def _sb_prompt_kernel(q_ref, k_ref, vt_ref, lm_ref, o_ref):
    qi = pl.program_id(2)
    q0 = qi * QB
    q = (q_ref[0, 0].astype(F32) * ATT_SCALE).astype(BF16)
    lm = lm_ref[...]
    key = _iota((KT, QB), 0)
    qry = _iota((KT, QB), 1)

    def tile(j, carry, acc, first):
        start = pl.multiple_of(q0 - j * KT, QB)
        k = k_ref[0, 0, pl.ds(start, KT), :]
        vt = vt_ref[0, 0, :, pl.ds(start, KT)]
        z = _dot_nt(k, q)
        valid = key >= KPAD - start
        if first:
            valid = valid & (qry + KPAD - key > 0)
        return _sb_step_t(z, valid, carry, acc, vt, lm)

    carry, acc = tile(0, jnp.zeros((1, QB), F32), jnp.zeros((HEAD_DIM, QB), F32), True)
    n_tiles = qi // (KT // QB) + 1

    def more(state):
        j, carry, _ = state
        return (j < n_tiles) & (jnp.max(carry) > SB_DEAD)

    def step(state):
        j, carry, acc = state
        carry, acc = tile(j, carry, acc, False)
        return j + 1, carry, acc

    _, _, acc = lax.while_loop(more, step, (jnp.int32(1), carry, acc))
    o_ref[0, 0] = acc.T.astype(o_ref.dtype)


def sb_prompt(q, k, vt):
    b, h, t, _ = q.shape
    tp = k.shape[2]
    return pl.pallas_call(
        _sb_prompt_kernel, grid=(b, h, t // QB),
        in_specs=[pl.BlockSpec((1, 1, QB, HEAD_DIM), lambda b_, h_, i: (b_, h_, i, 0)),
                  pl.BlockSpec((1, 1, tp, HEAD_DIM), lambda b_, h_, i: (b_, h_, 0, 0)),
                  pl.BlockSpec((1, 1, HEAD_DIM, tp), lambda b_, h_, i: (b_, h_, 0, 0)),
                  pl.BlockSpec((KT, KT), lambda b_, h_, i: (0, 0))],
        out_specs=pl.BlockSpec((1, 1, QB, HEAD_DIM), lambda b_, h_, i: (b_, h_, i, 0)),
        out_shape=jax.ShapeDtypeStruct((b, h, t, HEAD_DIM), BF16),
        compiler_params=_params(("arbitrary",) * 3), name="sb_prompt")(q, k, vt, _later_matrix(KT).T)


def _diff_lambda(lam_ref, lam_init):
    lv = lam_ref[...].astype(F32)
    a = jnp.sum(lv[0:1] * lv[1:2], axis=-1, keepdims=True)
    b = jnp.sum(lv[2:3] * lv[3:4], axis=-1, keepdims=True)
    return jnp.exp(a) - jnp.exp(b) + lam_init


def _diff_finish(o1, l1, o2, l2, lam, sub_g, lam_init):
    a = o1 / jnp.maximum(l1, 1e-30) - lam * (o2 / jnp.maximum(l2, 1e-30))
    a = a * lax.rsqrt(jnp.mean(jnp.square(a), axis=-1, keepdims=True) + LN_EPS)
    return a * sub_g * (1.0 - lam_init)


def _diff_prompt_kernel(q_ref, k_ref, vt_ref, b0_ref, c31_ref, lam_ref, sg_ref, o_ref, *, lam_init):
    qi = pl.program_id(2)
    q0 = qi * QB
    w = DIFF_REP * QB
    kt = b0_ref.shape[1]
    kpad = kt - QB
    key = _iota((kt, w), 0)
    qry = _iota((kt, w), 1) % QB
    qcat = [jnp.concatenate([(q_ref[0, 0, r, c].astype(F32) * ATT_SCALE).astype(BF16)
                             for r in range(DIFF_REP)], axis=0) for c in range(2)]

    def tile(j, state, first, padded):
        start = pl.multiple_of(q0 - j * kt, QB)
        vt = vt_ref[0, 0, :, pl.ds(start, kt)]
        valid = (key >= kpad - start) if padded else None
        if first:
            valid = valid & (qry + kpad - key >= 0)
        out = []
        for c in range(2):
            k = k_ref[0, 0, c, pl.ds(start, kt), :]
            s = _dot_nt(k, qcat[c]) + (b0_ref[0] if first else c31_ref[0])
            out.append(_softmax_step_t(s, valid, *state[c], vt))
        return tuple(out)

    init = tuple((jnp.full((1, w), NEG, F32), jnp.zeros((1, w), F32),
                  jnp.zeros((2 * HEAD_DIM, w), F32)) for _ in range(2))
    state = tile(0, init, True, True)
    last = qi // (kt // QB)
    state = lax.fori_loop(1, last, lambda j, st: tile(j, st, False, False), state)
    state = lax.cond(last >= 1, lambda st: tile(last, st, False, True), lambda st: st, state)
    lam = _diff_lambda(lam_ref, lam_init)
    (_, l1, o1), (_, l2, o2) = state
    a = o1 / jnp.maximum(l1, 1e-30) - lam * (o2 / jnp.maximum(l2, 1e-30))
    a = a * lax.rsqrt(jnp.mean(jnp.square(a), axis=0, keepdims=True) + LN_EPS)
    a = a * sg_ref[...] * (1.0 - lam_init)
    o_ref[0] = jnp.concatenate([a[:, r * QB:(r + 1) * QB].T for r in range(DIFF_REP)],
                               axis=-1).astype(o_ref.dtype)


def _heads_on_lanes(table, group):
    h, k, q = table.shape
    return table.reshape(h // group, group, k, q).transpose(0, 2, 1, 3).reshape(h // group, k, group * q)


def diff_prompt(q, k, vt, bias0_t, c31, lam_rows, sub_g_col, lam_init):
    b, gk, _, _, t, _ = q.shape
    tp = k.shape[3]
    w = DIFF_REP * QB
    b0 = _heads_on_lanes(bias0_t, DIFF_REP)
    c31 = _heads_on_lanes(jnp.broadcast_to(c31[:, None, None], (DIFF_HEADS, 1, QB)), DIFF_REP)
    return pl.pallas_call(
        functools.partial(_diff_prompt_kernel, lam_init=lam_init), grid=(b, gk, t // QB),
        in_specs=[pl.BlockSpec((1, 1, DIFF_REP, 2, QB, HEAD_DIM), lambda b_, g_, i: (b_, g_, 0, 0, i, 0)),
                  pl.BlockSpec((1, 1, 2, tp, HEAD_DIM), lambda b_, g_, i: (b_, g_, 0, 0, 0)),
                  pl.BlockSpec((1, 1, 2 * HEAD_DIM, tp), lambda b_, g_, i: (b_, g_, 0, 0)),
                  pl.BlockSpec((1, b0.shape[1], w), lambda b_, g_, i: (g_, 0, 0)),
                  pl.BlockSpec((1, 1, w), lambda b_, g_, i: (g_, 0, 0)),
                  pl.BlockSpec((4, HEAD_DIM), lambda b_, g_, i: (0, 0)),
                  pl.BlockSpec((2 * HEAD_DIM, 1), lambda b_, g_, i: (0, 0))],
        out_specs=pl.BlockSpec((1, QB, DIFF_REP * 2 * HEAD_DIM), lambda b_, g_, i: (b_, i, g_)),
        out_shape=jax.ShapeDtypeStruct((b, t, DIFF_HEADS * 2 * HEAD_DIM), BF16),
        compiler_params=_params(("arbitrary",) * 3), name="diff_prompt")(
            q, k, vt, b0, c31, lam_rows, sub_g_col)


CMP_BAND_BACK = 4
CMP_BAND = 16


def _cmp_band_table(tab):
    first = [-((m - CMP_BAND_BACK) * CMP_BLOCK + CMP_BLOCK - 1) for m in range(CMP_BAND)]
    delta = jnp.stack([_window(tab, d0, QB) for d0 in first], axis=1) - tab[:, -1][:, None, None]
    hi = delta.astype(BF16)
    lo = (delta - hi.astype(F32)).astype(BF16)
    return jnp.stack([hi, lo], axis=1)


def _select_blocks(score_ref, n_rows, n_keep):
    sc = score_ref[...]
    blk = _iota(sc.shape, 0)

    def body(i, rank):
        r = score_ref[pl.ds(i, 1), :]
        better = (r > sc) | ((r == sc) & (i < blk))
        return rank + jnp.where(better, 1.0, 0.0)

    rank = lax.fori_loop(0, n_rows, body, jnp.zeros(sc.shape, F32), unroll=8)
    return rank < n_keep


def _nsa_prompt_kernel(q_ref, kc_ref, vct_ref, ks_ref, vst_ref, kw_ref, vwt_ref, bc_ref, b0_ref, c31_ref,
                       gate_ref, o_ref, impt_ref, score_ref, *, n_sel_blocks, n_keep):
    qi = pl.program_id(2)
    q0 = qi * QB
    n_tiles = qi // (KT // QB) + 1
    nc = kc_ref.shape[2]
    ns = nc // 2
    heads = range(NSA_REP)
    w = NSA_REP * QB
    qcat = jnp.concatenate([(q_ref[0, 0, r].astype(F32) * ATT_SCALE).astype(BF16) for r in heads], axis=0)
    on_lanes = lambda rows: jnp.concatenate(rows, axis=1)

    valid_c = _iota((nc, w), 1) % QB + q0 - (_iota((nc, w), 0) * CMP_BLOCK + (CMP_BLOCK - 1)) >= 0
    first_blk = qi * (QB // CMP_BLOCK) - CMP_BAND_BACK
    place = jnp.where(_iota((nc, CMP_BAND), 0) == _iota((nc, CMP_BAND), 1) + first_blk, 1.0, 0.0).astype(BF16)
    bias = c31_ref[0] + _dot(place, bc_ref[0, 0]) + _dot(place, bc_ref[0, 1])
    p = _softmax_full_t(_dot_nt(kc_ref[0, 0], qcat) + bias, valid_c)
    o_c = _dot(vct_ref[0, 0], p.astype(BF16))
    imp = p[:, :QB]
    for r in range(1, NSA_REP):
        imp = imp + p[:, r * QB:(r + 1) * QB]
    impt_ref[...] = imp
    imp_s = impt_ref[pl.ds(0, ns, stride=2), :] + impt_ref[pl.ds(1, ns, stride=2), :]
    blk = _iota((ns, QB), 0)
    qpos = _iota((ns, QB), 1) + q0
    forced = (blk == qpos // SEL_BLOCK) | (blk == 0)
    future = blk * SEL_BLOCK > qpos
    score = jnp.where(forced, FORCE, jnp.where(future, -FORCE, imp_s))
    score_ref[...] = jnp.where(blk < n_sel_blocks, score, -3e38)
    sel_t = jnp.where(_select_blocks(score_ref, ns, n_keep), 1.0, 0.0).astype(BF16)

    key = _iota((KT, w), 0)
    qry = _iota((KT, w), 1) % QB
    causal0 = qry + KPAD - key >= 0
    key_blk = _iota((KT, ns), 0) // SEL_BLOCK
    blk_e = _iota((KT, ns), 1)

    def branch(k_ref, vt_ref, hi, valid_fn):
        def tile(j, state, first):
            start = pl.multiple_of(q0 - j * KT, QB)
            k = k_ref[0, 0, pl.ds(start, KT), :]
            vt = vt_ref[0, 0, :, pl.ds(start, KT)]
            s = _dot_nt(k, qcat) + (b0_ref[0] if first else c31_ref[0])
            return _softmax_step_t(s, valid_fn(j, start, first), *state, vt)

        init = (jnp.full((1, w), NEG, F32), jnp.zeros((1, w), F32), jnp.zeros((HEAD_DIM, w), F32))
        state = tile(0, init, True)
        _, l, acc = lax.fori_loop(1, hi, lambda j, st: tile(j, st, False), state)
        return acc / jnp.maximum(l, 1e-30)

    def valid_sel(j, start, first):
        base = start // SEL_BLOCK - KPAD // SEL_BLOCK
        expand = jnp.where(blk_e == key_blk + base, 1.0, 0.0).astype(BF16)
        chosen = on_lanes([_dot(expand, sel_t)] * NSA_REP) > 0.5
        return (chosen & causal0) if first else chosen

    def valid_win(j, start, first):
        ok = key >= KPAD - start
        if first:
            return ok & causal0
        return ok & (qry + KPAD - key + j * KT < WINDOW)

    o_s = branch(ks_ref, vst_ref, n_tiles, valid_sel)
    o_w = branch(kw_ref, vwt_ref, jnp.minimum(n_tiles, (WINDOW + QB - 1) // KT + 1), valid_win)

    gate = jax.nn.sigmoid(gate_ref[0, 0].astype(F32))
    g0, g1, g2 = (on_lanes([gate[3 * r + i:3 * r + i + 1, :] for r in heads]) for i in range(3))
    o = g0 * o_c + g1 * o_s + g2 * o_w
    o_ref[0] = jnp.concatenate([o[:, r * QB:(r + 1) * QB] for r in heads], axis=0).T.astype(o_ref.dtype)


def nsa_prompt(q, kc, vct, ks, vst, kw, vwt, bias_c, bias0_t, c31, gates_t, n_sel_blocks):
    b, g, _, t, _ = q.shape
    tp = ks.shape[2]
    nc = kc.shape[2]
    k_spec = pl.BlockSpec((1, 1, tp, HEAD_DIM), lambda b_, g_, i: (b_, g_, 0, 0))
    vt_spec = pl.BlockSpec((1, 1, HEAD_DIM, tp), lambda b_, g_, i: (b_, g_, 0, 0))
    kern = functools.partial(_nsa_prompt_kernel, n_sel_blocks=n_sel_blocks,
                             n_keep=min(N_SEL, n_sel_blocks))
    w = NSA_REP * QB
    b0 = _heads_on_lanes(bias0_t, NSA_REP)
    c31 = _heads_on_lanes(jnp.broadcast_to(c31[:, None, None], (NSA_HEADS, 1, QB)), NSA_REP)
    band = _heads_on_lanes(bias_c.reshape(NSA_HEADS, 2 * CMP_BAND, QB), NSA_REP).reshape(g, 2, CMP_BAND, w)
    return pl.pallas_call(
        kern, grid=(b, g, t // QB),
        in_specs=[pl.BlockSpec((1, 1, NSA_REP, QB, HEAD_DIM), lambda b_, g_, i: (b_, g_, 0, i, 0)),
                  pl.BlockSpec((1, 1, nc, HEAD_DIM), lambda b_, g_, i: (b_, g_, 0, 0)),
                  pl.BlockSpec((1, 1, HEAD_DIM, nc), lambda b_, g_, i: (b_, g_, 0, 0)),
                  k_spec, vt_spec, k_spec, vt_spec,
                  pl.BlockSpec((1, 2, CMP_BAND, w), lambda b_, g_, i: (g_, 0, 0, 0)),
                  pl.BlockSpec((1, KT, w), lambda b_, g_, i: (g_, 0, 0)),
                  pl.BlockSpec((1, 1, w), lambda b_, g_, i: (g_, 0, 0)),
                  pl.BlockSpec((1, 1, 4 * NSA_REP, QB), lambda b_, g_, i: (b_, g_, 0, i))],
        out_specs=pl.BlockSpec((1, QB, NSA_REP * HEAD_DIM), lambda b_, g_, i: (b_, i, g_)),
        out_shape=jax.ShapeDtypeStruct((b, t, NSA_HEADS * HEAD_DIM), BF16),
        scratch_shapes=[pltpu.VMEM((nc, QB), F32), pltpu.VMEM((nc // 2, QB), F32)],
        compiler_params=_params(("arbitrary",) * 3), name="nsa_prompt")(
            q, kc, vct, ks, vst, kw, vwt, band, b0, c31, gates_t)


CMP_PP = 16
CMP_HALF = CMP_BLOCK // 2


def _compress_kernel(pt_ref, w_ref, pe_ref, *rest, pp, positions_on_lanes):
    pages, o_ref, lo_ref, hi_ref = rest[:pp], rest[pp], rest[pp + 1], rest[pp + 2]
    if positions_on_lanes:
        stage_ref = rest[pp + 3]
        for p in range(pp):
            stage_ref[p] = pages[p][...].T
        pages = [stage_ref.at[p] for p in range(pp)]
    width = 2 * HEAD_DIM
    acc = jnp.zeros((pp * SUBLANES, 2 * width), F32)
    for i in range(CMP_HALF):
        rows = [pages[p][pl.ds(i, SUBLANES, stride=CMP_HALF), :] + pe_ref[i] for p in range(pp)]
        acc = acc + _dot(jnp.concatenate(rows, axis=0).astype(BF16), w_ref[0, i])
    lo_ref[...] = acc[:, :width]
    hi_ref[...] = acc[:, width:]
    n = pp * SUBLANES // 2
    o_ref[0] = lo_ref[pl.ds(0, n, stride=2), :] + hi_ref[pl.ds(1, n, stride=2), :]


def compress_pages(page_table, src, layer, block, w_cat, pe_tiles, positions_on_lanes=False):
    nseq, n_pages = page_table.shape
    pp = min(CMP_PP, n_pages)
    width = 2 * HEAD_DIM

    def page_spec(p):
        def index(b, s, kv, pt):
            where = (block + kv, 0) if positions_on_lanes else (0, block + kv)
            return (pt[b, s * pp + p], layer) + where
        return pl.BlockSpec((None, None, PAGE, width), index)

    scratch = [pltpu.VMEM((pp * SUBLANES, width), F32), pltpu.VMEM((pp * SUBLANES, width), F32)]
    if positions_on_lanes:
        scratch.append(pltpu.VMEM((pp, PAGE, width), F32))
    grid_spec = pltpu.PrefetchScalarGridSpec(
        num_scalar_prefetch=1, grid=(nseq, n_pages // pp, 2),
        in_specs=[pl.BlockSpec((1, CMP_HALF, width, 2 * width), lambda b, s, kv, pt: (kv, 0, 0, 0)),
                  pl.BlockSpec((CMP_HALF, SUBLANES, width), lambda b, s, kv, pt: (0, 0, 0))]
        + [page_spec(p) for p in range(pp)],
        out_specs=pl.BlockSpec((1, pp * 4, width), lambda b, s, kv, pt: (b, s, kv)),
        scratch_shapes=scratch)
    return pl.pallas_call(
        functools.partial(_compress_kernel, pp=pp, positions_on_lanes=positions_on_lanes), grid_spec=grid_spec,
        out_shape=jax.ShapeDtypeStruct((nseq, n_pages * 4, 2 * width), F32),
        compiler_params=_params(("arbitrary",) * 3), name="compress")(
            page_table, w_cat, pe_tiles, *([src] * pp))


def _compress_weights(pe, wk, wv):
    def cat(w):
        w = w.reshape(CMP_BLOCK, HEAD_DIM, HEAD_DIM)
        z = jnp.zeros_like(w)
        full = jnp.concatenate([jnp.concatenate([w, z], -1), jnp.concatenate([z, w], -1)], axis=1)
        return jnp.concatenate([full[:CMP_HALF], full[CMP_HALF:]], axis=-1)

    w_cat = jnp.stack([cat(wk), cat(wv)]).astype(BF16)
    pe2 = jnp.tile(pe.astype(F32), (1, 2))
    pe_tiles = jnp.stack([pe2[:CMP_HALF], pe2[CMP_HALF:]], axis=1)
    pe_tiles = jnp.tile(pe_tiles, (1, SUBLANES // 2, 1))
    return w_cat, pe_tiles


SB_PP = 8
DIFF_PP = 8
SEL_PP = 16


def _page_specs(pp, n_pages, rows, row_block, layer):
    def spec(p):
        def index(b, s, pt):
            return (pt[b, n_pages - 1 - (s * pp + p)], layer, row_block, 0)
        return pl.BlockSpec((None, None, rows, LANES), index)
    return [spec(p) for p in range(pp)]


def _positions_on_lanes(cache):
    pool, layers = cache.shape[:2]
    return jnp.transpose(cache, (0, 1, 3, 4, 5, 2)).reshape(pool, layers, -1, PAGE)


def _token_of_row(shape):
    return _iota(shape, 0) % DEC_S


def _sb_sample_kernel(pt_ref, q_ref, new_ref, u_ref, cin_ref, ain_ref, *rest, pp, first):
    pages, (o_ref, cout_ref, aout_ref, carry_ref, acc_ref) = rest[:pp], rest[pp:]
    s = pl.program_id(1)
    q = q_ref[0]
    u = u_ref[...]
    rows = SB_HEADS * DEC_S

    @pl.when(s == 0)
    def _():
        if first:
            k = new_ref[0, :, :SB_W]
            v = new_ref[0, :, SB_W:]
            valid = _iota((rows, PAGE), 1) < _token_of_row((rows, PAGE))
            carry, acc = _sb_step(_dot_nt(q, k), valid, jnp.zeros((rows, 1), F32),
                                  jnp.zeros((rows, SB_W), F32), v, u)
        else:
            carry, acc = cin_ref[0], ain_ref[0]
        carry_ref[...] = carry
        acc_ref[...] = acc

    for p in range(pp):
        @pl.when(jnp.max(carry_ref[...]) > SB_DEAD)
        def _(p=p):
            kt = pages[p][:SB_W, :].astype(BF16)
            vt = pages[p][SB_W:, :].astype(BF16)
            carry, acc = _sb_step(_dot(q, kt), None, carry_ref[...], acc_ref[...], vt, u, v_on_lanes=True)
            carry_ref[...] = carry
            acc_ref[...] = acc

    @pl.when(s == pl.num_programs(1) - 1)
    def _():
        a = acc_ref[...]
        o_ref[0] = jnp.concatenate(
            [a[h * DEC_S:(h + 1) * DEC_S, h * HEAD_DIM:(h + 1) * HEAD_DIM] for h in range(SB_HEADS)],
            axis=-1).astype(o_ref.dtype)
        cout_ref[0] = carry_ref[...]
        aout_ref[0] = a


def _sb_sample_call(page_table, qbd, new_kv, cache, layer, carry, acc, skip, count, first):
    db = page_table.shape[0]
    pp = min(SB_PP, count)
    rows = SB_HEADS * DEC_S
    per_seq = lambda shape: pl.BlockSpec((1,) + shape, lambda b, s, pt: (b,) + (0,) * len(shape))
    grid_spec = pltpu.PrefetchScalarGridSpec(
        num_scalar_prefetch=1, grid=(db, count // pp),
        in_specs=[per_seq((rows, SB_W)), per_seq((PAGE, 2 * SB_W)),
                  pl.BlockSpec((PAGE, PAGE), lambda b, s, pt: (0, 0)), per_seq((rows, 1)), per_seq((rows, SB_W))]
        + _page_specs(pp, page_table.shape[1] - skip, 2 * SB_W, 0, layer),
        out_specs=[per_seq((DEC_S, SB_W)), per_seq((rows, 1)), per_seq((rows, SB_W))],
        scratch_shapes=[pltpu.VMEM((rows, 1), F32), pltpu.VMEM((rows, SB_W), F32)])
    return pl.pallas_call(
        functools.partial(_sb_sample_kernel, pp=pp, first=first), grid_spec=grid_spec,
        out_shape=[jax.ShapeDtypeStruct((db, DEC_S, SB_W), BF16), jax.ShapeDtypeStruct((db, rows, 1), F32),
                   jax.ShapeDtypeStruct((db, rows, SB_W), F32)],
        compiler_params=_params(("arbitrary",) * 2), name="sb_sample")(
            page_table, qbd, new_kv, _later_matrix(PAGE), carry, acc, *([cache] * pp))


def sb_sample(page_table, qbd, new_kv, cache, layer):
    db, n_pages = page_table.shape
    rows = SB_HEADS * DEC_S
    head = min(SB_PP, n_pages)
    zeros = (jnp.zeros((db, rows, 1), F32), jnp.zeros((db, rows, SB_W), F32))
    out, carry, acc = _sb_sample_call(page_table, qbd, new_kv, cache, layer, *zeros, 0, head, True)
    if n_pages == head:
        return out
    earlier = lambda: _sb_sample_call(page_table, qbd, new_kv, cache, layer, carry, acc, head,
                                      n_pages - head, False)[0]
    return lax.cond(jnp.max(carry) > SB_DEAD, earlier, lambda: out)


DIFF_GROUP_ROWS = DIFF_REP * 2 * DEC_S
DIFF_ROW_STRIDE = 2 * DIFF_KV_HEADS


def _diff_sample_tiles(q, loads, bias, valid, m, l, acc):
    heads = range(DIFF_KV_HEADS)
    rows = lambda g: slice(g * DIFF_GROUP_ROWS, (g + 1) * DIFF_GROUP_ROWS)
    s = jnp.concatenate(
        [jnp.concatenate([_dot_nt(q[rows(g)], load(g).astype(BF16)) for g in heads], axis=0)
         for load in loads], axis=1) + bias
    if valid is not None:
        s = jnp.where(valid, s, NEG)
    m_new = jnp.maximum(m, jnp.max(s, axis=-1, keepdims=True))
    p = jnp.exp(s - m_new)
    if valid is not None:
        p = jnp.where(valid, p, 0.0)
    alpha = jnp.exp(m - m_new)
    l = alpha * l + jnp.sum(p, axis=-1, keepdims=True)
    p = p.astype(BF16)
    pv = []
    for g in heads:
        terms = [_dot(p[rows(g), i * PAGE:(i + 1) * PAGE], load(DIFF_KV_HEADS + g).astype(BF16))
                 for i, load in enumerate(loads)]
        pv.append(functools.reduce(lambda a, b: a + b, terms))
    return m_new, l, alpha * acc + jnp.concatenate(pv, axis=0)


def _diff_sample_kernel(pt_ref, q_ref, new_ref, cvec_ref, blast_ref, bnew_ref, lam_ref, sg_ref, *rest,
                        pp, lam_init):
    pages, o_ref, m_ref, l_ref, acc_ref = rest[:pp], rest[pp], rest[pp + 1], rest[pp + 2], rest[pp + 3]
    s = pl.program_id(1)
    q = q_ref[0]
    rows = DIFF_HEADS * 2 * DEC_S
    strided = lambda ref: (lambda off: ref[pl.ds(off, PAGE, stride=DIFF_ROW_STRIDE), :])

    @pl.when(s == 0)
    def _():
        valid = _iota((rows, PAGE), 1) <= _token_of_row((rows, PAGE))
        m, l, acc = _diff_sample_tiles(q, [strided(new_ref)], bnew_ref[...], valid,
                                       jnp.full((rows, 1), NEG, F32), jnp.zeros((rows, 1), F32),
                                       jnp.zeros((rows, 2 * HEAD_DIM), F32))
        m_ref[...] = m
        l_ref[...] = l
        acc_ref[...] = acc

    bias = jnp.broadcast_to(cvec_ref[...], (rows, PAGE))
    bias = jnp.concatenate([jnp.where(s == 0, blast_ref[...], bias)] + [bias] * (pp - 1), axis=1)
    m, l, acc = _diff_sample_tiles(q, [strided(ref) for ref in pages], bias, None,
                                   m_ref[...], l_ref[...], acc_ref[...])
    m_ref[...] = m
    l_ref[...] = l
    acc_ref[...] = acc

    @pl.when(s == pl.num_programs(1) - 1)
    def _():
        a = acc_ref[...]
        ls = l_ref[...]
        lam = _diff_lambda(lam_ref, lam_init)
        outs = []
        for head in range(DIFF_HEADS):
            r0 = head * 2 * DEC_S
            outs.append(_diff_finish(a[r0:r0 + DEC_S], ls[r0:r0 + DEC_S], a[r0 + DEC_S:r0 + 2 * DEC_S],
                                     ls[r0 + DEC_S:r0 + 2 * DEC_S], lam, sg_ref[...], lam_init))
        o_ref[0] = jnp.concatenate(outs, axis=-1).astype(o_ref.dtype)


def diff_sample(page_table, qc, new_kv, cvec, blast, bnew, lam_rows, sub_g, cache, layer, lam_init):
    db, n_pages = page_table.shape
    pp = min(DIFF_PP, n_pages)
    rows = DIFF_HEADS * 2 * DEC_S
    page_rows = PAGE * DIFF_ROW_STRIDE
    const = lambda shape: pl.BlockSpec(shape, lambda b, s, pt: (0,) * len(shape))
    grid_spec = pltpu.PrefetchScalarGridSpec(
        num_scalar_prefetch=1, grid=(db, n_pages // pp),
        in_specs=[pl.BlockSpec((1, rows, 2 * HEAD_DIM), lambda b, s, pt: (b, 0, 0)),
                  pl.BlockSpec((None, page_rows, LANES), lambda b, s, pt: (b, 0, 0)),
                  const((rows, 1)), const((rows, PAGE)), const((rows, PAGE)),
                  const((4, HEAD_DIM)), const((1, 2 * HEAD_DIM))]
        + _page_specs(pp, n_pages, page_rows, 0, layer),
        out_specs=pl.BlockSpec((1, DEC_S, DIFF_HEADS * 2 * HEAD_DIM), lambda b, s, pt: (b, 0, 0)),
        scratch_shapes=[pltpu.VMEM((rows, 1), F32), pltpu.VMEM((rows, 1), F32),
                        pltpu.VMEM((rows, 2 * HEAD_DIM), F32)])
    return pl.pallas_call(
        functools.partial(_diff_sample_kernel, pp=pp, lam_init=lam_init), grid_spec=grid_spec,
        out_shape=jax.ShapeDtypeStruct((db, DEC_S, DIFF_HEADS * 2 * HEAD_DIM), BF16),
        compiler_params=_params(("arbitrary",) * 2), name="diff_sample")(
            page_table, qc, new_kv, cvec, blast, bnew, lam_rows, sub_g, *([cache] * pp))


NSA_ROWS = NSA_HEADS * DEC_S
NSA_GT = NSA_GROUPS * DEC_S


def _nsa_diag(a):
    pieces = []
    for r in range(NSA_REP):
        for g in range(NSA_GROUPS):
            r0 = r * NSA_GT + g * DEC_S
            pieces.append(a[r0:r0 + DEC_S, g * HEAD_DIM:(g + 1) * HEAD_DIM])
    return jnp.concatenate(pieces, axis=0)


def _nsa_sample_cw_kernel(q_ref, kcvc_ref, bc_ref, win_ref, bw_ref, oc_ref, ow_ref, sel_ref,
                          impt_ref, score_ref, *, win_len, n_keep):
    q = q_ref[0]
    nc = kcvc_ref.shape[1]
    ns = nc // 2
    kc = kcvc_ref[0, :, :NSA_KV_W].astype(BF16)
    vc = kcvc_ref[0, :, NSA_KV_W:].astype(BF16)
    p = _softmax_full(_dot_nt(q, kc) + bc_ref[...], None)
    oc_ref[0] = _nsa_diag(_dot(p.astype(BF16), vc))
    imp = p[0:NSA_GT]
    for r in range(1, NSA_REP):
        imp = imp + p[r * NSA_GT:(r + 1) * NSA_GT]
    imp = jnp.concatenate([imp, jnp.zeros((LANES - NSA_GT, nc), F32)], axis=0)
    impt_ref[...] = imp.T
    imp_s = impt_ref[pl.ds(0, ns, stride=2), :] + impt_ref[pl.ds(1, ns, stride=2), :]
    blk = _iota((ns, LANES), 0)
    score_ref[...] = jnp.where(blk == 0, FORCE, imp_s)
    sel_t = jnp.where(_select_blocks(score_ref, ns, n_keep), 1.0, 0.0)
    sel_ref[0] = sel_t.T[:NSA_GT].astype(sel_ref.dtype)

    wl = win_ref.shape[1]
    kw = win_ref[0, :, :NSA_KV_W]
    vw = win_ref[0, :, NSA_KV_W:]
    tok = _token_of_row((NSA_ROWS, wl))
    col = _iota((NSA_ROWS, wl), 1)
    valid = (col > tok + (win_len - WINDOW)) & (col <= tok + win_len) & (col < win_len + DEC_S)
    pw = _softmax_full(_dot_nt(q, kw) + bw_ref[...], valid)
    ow_ref[0] = _nsa_diag(_dot(pw.astype(BF16), vw))


def nsa_sample_cw(qbd, kcvc, bias_c, win_all, bias_w, win_len, n_keep):
    db = qbd.shape[0]
    nc = kcvc.shape[1]
    wl = win_all.shape[1]
    const = lambda shape: pl.BlockSpec(shape, lambda b: (0,) * len(shape))
    return pl.pallas_call(
        functools.partial(_nsa_sample_cw_kernel, win_len=win_len, n_keep=n_keep), grid=(db,),
        in_specs=[pl.BlockSpec((1, NSA_ROWS, NSA_KV_W), lambda b: (b, 0, 0)),
                  pl.BlockSpec((1, nc, 2 * NSA_KV_W), lambda b: (b, 0, 0)),
                  const((NSA_ROWS, nc)),
                  pl.BlockSpec((1, wl, 2 * NSA_KV_W), lambda b: (b, 0, 0)),
                  const((NSA_ROWS, wl))],
        out_specs=[pl.BlockSpec((1, NSA_ROWS, HEAD_DIM), lambda b: (b, 0, 0)),
                   pl.BlockSpec((1, NSA_ROWS, HEAD_DIM), lambda b: (b, 0, 0)),
                   pl.BlockSpec((1, NSA_GT, nc // 2), lambda b: (b, 0, 0))],
        out_shape=[jax.ShapeDtypeStruct((db, NSA_ROWS, HEAD_DIM), F32),
                   jax.ShapeDtypeStruct((db, NSA_ROWS, HEAD_DIM), F32),
                   jax.ShapeDtypeStruct((db, NSA_GT, nc // 2), BF16)],
        scratch_shapes=[pltpu.VMEM((nc, LANES), F32), pltpu.VMEM((nc // 2, LANES), F32)],
        compiler_params=_params(("arbitrary",)), name="nsa_sample_cw")(qbd, kcvc, bias_c, win_all, bias_w)


def _nsa_sample_sel_kernel(pt_ref, q_ref, new_ref, sel_ref, cvec_ref, blast_ref, bnew_ref, oc_ref, ow_ref,
                           gate_ref, *rest, pp, n_pages):
    pages, o_ref, m_ref, l_ref, acc_ref = rest[:pp], rest[pp], rest[pp + 1], rest[pp + 2], rest[pp + 3]
    s = pl.program_id(1)
    q = q_ref[0]
    sel = sel_ref[0]
    ns = sel.shape[1]

    @pl.when(s == 0)
    def _():
        k = new_ref[0, :, :NSA_KV_W]
        v = new_ref[0, :, NSA_KV_W:]
        valid = _iota((NSA_ROWS, PAGE), 1) <= _token_of_row((NSA_ROWS, PAGE))
        m, l, acc = _softmax_step(_dot_nt(q, k) + bnew_ref[...], valid, jnp.full((NSA_ROWS, 1), NEG, F32),
                                  jnp.zeros((NSA_ROWS, 1), F32), jnp.zeros((NSA_ROWS, NSA_KV_W), F32), v)
        m_ref[...] = m
        l_ref[...] = l
        acc_ref[...] = acc

    latest = n_pages - 1 - s * pp
    lane = _iota((ns, pp * PAGE), 1)
    blk_of_lane = (latest - lane // PAGE) * (PAGE // SEL_BLOCK) + (lane % PAGE) // SEL_BLOCK
    expand = jnp.where(_iota((ns, pp * PAGE), 0) == blk_of_lane, 1.0, 0.0).astype(BF16)
    valid = jnp.concatenate([_dot(sel, expand)] * NSA_REP, axis=0) > 0.5
    bias = jnp.broadcast_to(cvec_ref[...], (NSA_ROWS, PAGE))
    bias = jnp.concatenate([jnp.where(s == 0, blast_ref[...], bias)] + [bias] * (pp - 1), axis=1)
    scores = jnp.concatenate([_dot(q, pages[p][:NSA_KV_W, :].astype(BF16)) for p in range(pp)], axis=1)
    scores = jnp.where(valid, scores + bias, NEG)
    m = m_ref[...]
    m_new = jnp.maximum(m, jnp.max(scores, axis=-1, keepdims=True))
    prob = jnp.where(valid, jnp.exp(scores - m_new), 0.0)
    alpha = jnp.exp(m - m_new)
    m_ref[...] = m_new
    l_ref[...] = alpha * l_ref[...] + jnp.sum(prob, axis=-1, keepdims=True)
    prob = prob.astype(BF16)
    pv = [_dot_nt(prob[:, p * PAGE:(p + 1) * PAGE], pages[p][NSA_KV_W:, :].astype(BF16)) for p in range(pp)]
    acc_ref[...] = alpha * acc_ref[...] + functools.reduce(lambda a, b: a + b, pv)

    @pl.when(s == pl.num_programs(1) - 1)
    def _():
        o_s = _nsa_diag(acc_ref[...] / jnp.maximum(l_ref[...], 1e-30))
        gate = jax.nn.sigmoid(gate_ref[0].astype(F32))
        o = gate[:, 0:1] * oc_ref[0] + gate[:, 1:2] * o_s + gate[:, 2:3] * ow_ref[0]
        pieces = []
        for g in range(NSA_GROUPS):
            for r in range(NSA_REP):
                r0 = r * NSA_GT + g * DEC_S
                pieces.append(o[r0:r0 + DEC_S])
        o_ref[0] = jnp.concatenate(pieces, axis=-1).astype(o_ref.dtype)


def nsa_sample_sel(page_table, qbd, new_kv, sel, cvec, blast, bnew, o_c, o_w, gates, cache, layer):
    db, n_pages = page_table.shape
    pp = min(SEL_PP, n_pages)
    ns = sel.shape[2]
    const = lambda shape: pl.BlockSpec(shape, lambda b, s, pt: (0,) * len(shape))
    per_seq = lambda shape: pl.BlockSpec((1,) + shape, lambda b, s, pt: (b,) + (0,) * len(shape))
    grid_spec = pltpu.PrefetchScalarGridSpec(
        num_scalar_prefetch=1, grid=(db, n_pages // pp),
        in_specs=[per_seq((NSA_ROWS, NSA_KV_W)), per_seq((PAGE, 2 * NSA_KV_W)), per_seq((NSA_GT, ns)),
                  const((NSA_ROWS, 1)), const((NSA_ROWS, PAGE)), const((NSA_ROWS, PAGE)),
                  per_seq((NSA_ROWS, HEAD_DIM)), per_seq((NSA_ROWS, HEAD_DIM)), per_seq((NSA_ROWS, LANES))]
        + _page_specs(pp, n_pages, 2 * NSA_KV_W, 1, layer),
        out_specs=pl.BlockSpec((1, DEC_S, NSA_HEADS * HEAD_DIM), lambda b, s, pt: (b, 0, 0)),
        scratch_shapes=[pltpu.VMEM((NSA_ROWS, 1), F32), pltpu.VMEM((NSA_ROWS, 1), F32),
                        pltpu.VMEM((NSA_ROWS, NSA_KV_W), F32)])
    return pl.pallas_call(
        functools.partial(_nsa_sample_sel_kernel, pp=pp, n_pages=n_pages), grid_spec=grid_spec,
        out_shape=jax.ShapeDtypeStruct((db, DEC_S, NSA_HEADS * HEAD_DIM), BF16),
        compiler_params=_params(("arbitrary",) * 2), name="nsa_sample_sel")(
            page_table, qbd, new_kv, sel, cvec, blast, bnew, o_c, o_w, gates, *([cache] * pp))


def _layer_norm(y, g, b):
    mu = jnp.mean(y, axis=-1, keepdims=True)
    d = y - mu
    var = jnp.mean(d * d, axis=-1, keepdims=True)
    return d * lax.rsqrt(var + LN_EPS) * g + b


def _split_bf16(x):
    hi = x.astype(BF16)
    return hi, (x - hi.astype(F32)).astype(BF16)


def _mix_router_kernel(a1_ref, a2_ref, w1_ref, w2_ref, x_ref, g_ref, b_ref, wr_ref, br_ref, u_ref,
                       xn_ref, xb_ref, rt_ref, gt_ref, cnt_ref, *, alpha):
    mix = _dot(a1_ref[...], w1_ref[...]) + _dot(a2_ref[...], w2_ref[...])
    xn = _layer_norm(alpha * x_ref[...] + mix, g_ref[...], b_ref[...])
    xn_ref[...] = xn
    xb_ref[...] = xn.astype(BF16)
    xh, xl = _split_bf16(xn)
    wh, wl = _split_bf16(wr_ref[...])
    logits = _dot_nt(wh, xh) + _dot_nt(wh, xl) + _dot_nt(wl, xh) + br_ref[...]
    e_iota = _iota(logits.shape, 0)
    work = logits
    vals, sels = [], []
    for _ in range(TOP_K):
        top = jnp.max(work, axis=0, keepdims=True)
        idx = jnp.min(jnp.where(work == top, e_iota, N_EXPERTS), axis=0, keepdims=True)
        sel = e_iota == idx
        vals.append(top)
        sels.append(sel)
        work = jnp.where(sel, -jnp.inf, work)
    ex = [jnp.exp(v - vals[0]) for v in vals]
    den = ex[0] + ex[1] + ex[2] + ex[3]
    gate = jnp.zeros(logits.shape, F32)
    chosen = sels[0]
    for k in range(TOP_K):
        gate = jnp.where(sels[k], ex[k] / den, gate)
        chosen = chosen | sels[k]
    cf = jnp.where(chosen, 1.0, 0.0)
    rank = _dot(cf.astype(BF16), u_ref[...])
    rt_ref[0] = jnp.where(chosen, rank, -1.0)
    gt_ref[0] = gate
    cnt = jnp.sum(cf, axis=1, keepdims=True)
    cnt_ref[0] = jnp.broadcast_to(cnt, (N_EXPERTS, LANES)).astype(jnp.int32)


def mix_router(a1, a2, w1, w2, x, ln_g, ln_b, w_router_t, b_router, tt, alpha):
    n = x.shape[0]
    half = a1.shape[1]
    j = np.arange(tt)
    before = jnp.asarray((j[:, None] < j[None, :]).astype(np.float32), dtype=BF16)
    tile = lambda w: pl.BlockSpec((tt, w), lambda i: (i, 0))
    const = lambda shape: pl.BlockSpec(shape, lambda i: (0,) * len(shape))
    route = pl.BlockSpec((1, N_EXPERTS, tt), lambda i: (i, 0, 0))
    return pl.pallas_call(
        functools.partial(_mix_router_kernel, alpha=alpha), grid=(n // tt,),
        in_specs=[tile(half), tile(half), const((half, D_MODEL)), const((half, D_MODEL)), tile(D_MODEL),
                  const((1, D_MODEL)), const((1, D_MODEL)), const((N_EXPERTS, D_MODEL)),
                  const((N_EXPERTS, 1)), const((tt, tt))],
        out_specs=[tile(D_MODEL), tile(D_MODEL), route, route,
                   pl.BlockSpec((1, N_EXPERTS, LANES), lambda i: (i, 0, 0))],
        out_shape=[jax.ShapeDtypeStruct((n, D_MODEL), F32), jax.ShapeDtypeStruct((n, D_MODEL), BF16),
                   jax.ShapeDtypeStruct((n // tt, N_EXPERTS, tt), F32),
                   jax.ShapeDtypeStruct((n // tt, N_EXPERTS, tt), F32),
                   jax.ShapeDtypeStruct((n // tt, N_EXPERTS, LANES), jnp.int32)],
        compiler_params=_params(("arbitrary",)), name="mix_router")(
            a1, a2, w1, w2, x, ln_g, ln_b, w_router_t, b_router, before)


def _pair_permutation():
    c = np.arange(2 * LANES)
    dest = np.where(c % 2 == 0, c // 2, LANES + c // 2)
    return jnp.asarray((dest[:, None] == c[None, :]).astype(np.float32), dtype=BF16)


def _split_pairs_kernel(w_ref, s_ref, o_ref):
    s = s_ref[...]
    for blk in range(w_ref.shape[2] // (2 * LANES)):
        cols = slice(blk * 2 * LANES, (blk + 1) * 2 * LANES)
        o_ref[0, :, cols] = _dot(w_ref[0, :, cols].astype(BF16), s).astype(BF16)


def split_pairs(w_up):
    e, d, f2 = w_up.shape
    return pl.pallas_call(
        _split_pairs_kernel, grid=(e,),
        in_specs=[pl.BlockSpec((1, d, f2), lambda i: (i, 0, 0)),
                  pl.BlockSpec((2 * LANES, 2 * LANES), lambda i: (0, 0))],
        out_specs=pl.BlockSpec((1, d, f2), lambda i: (i, 0, 0)),
        out_shape=jax.ShapeDtypeStruct((e, d, f2), BF16),
        compiler_params=_params(("arbitrary",)), name="split_pairs")(w_up, _pair_permutation())


def _moe_kernel(cnt_ref, xb_ref, xn_ref, rt_ref, gt_ref, wu_ref, bu_ref, wd_ref, bd_ref,
                lg_ref, lb_ref, y_ref, yb_ref, acc_ref, *, alpha):
    i = pl.program_id(0)
    e = pl.program_id(1)
    tt = xb_ref.shape[0]

    @pl.when(e == 0)
    def _():
        acc_ref[...] = jnp.zeros_like(acc_ref)

    n_chunks = (cnt_ref[i, e] + MOE_CH - 1) // MOE_CH
    slot = rt_ref[0, pl.ds(e, 1), :]
    gate = gt_ref[0, pl.ds(e, 1), :]

    def chunk(c):
        want = (_iota((MOE_CH, tt), 0) + c * MOE_CH).astype(F32)
        hit = slot == want
        xe = _dot(jnp.where(hit, 1.0, 0.0).astype(BF16), xb_ref[...]).astype(BF16)
        h = _dot(xe, wu_ref[0]) + bu_ref[0]
        acts = []
        for blk in range(h.shape[1] // (2 * LANES)):
            h_glu = jnp.minimum(h[:, blk * 2 * LANES:blk * 2 * LANES + LANES], SWIGLU_LIMIT)
            h_lin = jnp.clip(h[:, blk * 2 * LANES + LANES:(blk + 1) * 2 * LANES], -SWIGLU_LIMIT, SWIGLU_LIMIT)
            acts.append((h_glu * jax.nn.sigmoid(SWIGLU_ALPHA * h_glu) * (h_lin + 1.0)).astype(BF16))
        y = _dot(jnp.concatenate(acts, axis=-1), wd_ref[0]) + bd_ref[0]
        return y.astype(BF16), jnp.where(hit, gate, 0.0).astype(BF16)

    def scatter_back(c, _):
        y, back = chunk(c)
        acc_ref[...] += _dot_tn(back, y)
        return 0

    lax.fori_loop(0, n_chunks, scatter_back, 0)

    @pl.when(e == pl.num_programs(1) - 1)
    def _():
        y = _layer_norm(alpha * xn_ref[...] + acc_ref[...], lg_ref[...], lb_ref[...])
        y_ref[...] = y
        yb_ref[...] = y.astype(BF16)


def moe_ln(cnt, xb, xn, rt, gt, w_up, b_up, w_down, b_down, ln_g, ln_b, tt, alpha):
    n = xn.shape[0]
    ff = w_down.shape[1]
    tile = pl.BlockSpec((tt, D_MODEL), lambda i, e, c: (i, 0))
    route = pl.BlockSpec((1, N_EXPERTS, tt), lambda i, e, c: (i, 0, 0))
    per_e = lambda shape: pl.BlockSpec((1,) + shape, lambda i, e, c: (e, 0, 0))
    const = pl.BlockSpec((1, D_MODEL), lambda i, e, c: (0, 0))
    grid_spec = pltpu.PrefetchScalarGridSpec(
        num_scalar_prefetch=1, grid=(n // tt, N_EXPERTS),
        in_specs=[tile, tile, route, route, per_e((D_MODEL, 2 * ff)), per_e((1, 2 * ff)),
                  per_e((ff, D_MODEL)), per_e((1, D_MODEL)), const, const],
        out_specs=[tile, tile],
        scratch_shapes=[pltpu.VMEM((tt, D_MODEL), F32)])
    return pl.pallas_call(
        functools.partial(_moe_kernel, alpha=alpha), grid_spec=grid_spec,
        out_shape=[jax.ShapeDtypeStruct((n, D_MODEL), F32), jax.ShapeDtypeStruct((n, D_MODEL), BF16)],
        compiler_params=_params(("arbitrary",) * 2), name="moe_ln")(
            cnt, xb, xn, rt, gt, w_up, b_up, w_down, b_down, ln_g, ln_b)


MOE_TILE = 1024


def _split_pairs_bias(b_up):
    e, f2 = b_up.shape
    b = b_up.astype(F32).reshape(e, f2 // (2 * LANES), LANES, 2).transpose(0, 1, 3, 2)
    return b.reshape(e, 1, f2)


def _front_pad(a, axis, rows=KPAD):
    pad = [(0, 0)] * a.ndim
    pad[axis] = (rows, 0)
    return jnp.pad(a, pad)


def _first_tile_bias(tab, kt):
    by_u = _hankel(_window(tab, -(QB - 1), QB + kt - 1), QB, kt)
    return jnp.flip(by_u, axis=2).transpose(0, 2, 1)


def _pad_rows(a, rows):
    return jnp.pad(a, ((0, 0), (0, rows - a.shape[1]), (0, 0)))


def _channel_mixer(layer, a1, a2, w_out, x, p, tt):
    alpha = p["alpha"]
    half = a1.shape[1]
    w_out = w_out.astype(BF16)
    xn, xb, rt, gt, cnt = mix_router(
        a1, a2, w_out[:half], w_out[half:], x, p["ln_mix_g"][layer][None], p["ln_mix_b"][layer][None],
        p["w_router"][layer].T.astype(F32), p["b_router"][layer][:, None].astype(F32), tt, alpha)
    return moe_ln(cnt[:, :, 0], xb, xn, rt, gt, *p["experts"][layer],
                  p["ln_ffn_g"][layer][None], p["ln_ffn_b"][layer][None], tt, alpha)


def _even_prompt(h, b, t, cmp_w, bias_c, bias0, c31):
    h3 = h.reshape(b, t, -1)
    heads = lambda a, nh: a.reshape(b, t, nh, HEAD_DIM).transpose(0, 2, 1, 3).astype(BF16)
    heads_t = lambda a, nh: a.reshape(b, t, nh, HEAD_DIM).transpose(0, 2, 3, 1).astype(BF16)
    o = 3 * SB_W
    o_sb = sb_prompt(heads(h3[..., :SB_W], SB_HEADS),
                     _front_pad(heads(h3[..., SB_W:2 * SB_W], SB_HEADS), 2),
                     _front_pad(heads_t(h3[..., 2 * SB_W:o], SB_HEADS), 3))
    o_sb = o_sb.transpose(0, 2, 1, 3).reshape(b * t, SB_W)
    nq = h3[..., o:o + NSA_Q_W].reshape(b, t, NSA_GROUPS, NSA_REP, HEAD_DIM).transpose(0, 2, 3, 1, 4)
    o += NSA_Q_W
    nkv = h3[..., o:o + 6 * NSA_KV_W].reshape(b, t, 6, NSA_GROUPS, HEAD_DIM)
    gates = h3[..., o + 6 * NSA_KV_W:o + 6 * NSA_KV_W + NSA_GATE_W]
    gates = gates.reshape(b, t, NSA_GROUPS, NSA_REP * 3).transpose(0, 2, 3, 1)
    gates = jnp.pad(gates, ((0, 0), (0, 0), (0, NSA_REP), (0, 0)))
    n_pages = t // PAGE
    table = jnp.arange(b * n_pages, dtype=jnp.int32).reshape(b, n_pages)
    kcvc = compress_pages(table, h.reshape(b * n_pages, 1, PAGE, h.shape[-1]), 0, o // (2 * HEAD_DIM), *cmp_w)
    nc = t // CMP_BLOCK
    ncp = -(-nc // LANES) * LANES
    kcvc = kcvc.reshape(b, nc, 2, NSA_GROUPS, HEAD_DIM).astype(BF16)
    kc = jnp.pad(kcvc[:, :, 0].transpose(0, 2, 1, 3), ((0, 0), (0, 0), (0, ncp - nc), (0, 0)))
    vct = jnp.pad(kcvc[:, :, 1].transpose(0, 2, 3, 1), ((0, 0), (0, 0), (0, 0), (0, ncp - nc)))
    grp = lambda i: _front_pad(nkv[:, :, i].transpose(0, 2, 1, 3).astype(BF16), 2)
    grp_t = lambda i: _front_pad(nkv[:, :, i].transpose(0, 2, 3, 1).astype(BF16), 3)
    o_nsa = nsa_prompt(nq.astype(BF16), kc, vct, grp(2), grp_t(3), grp(4), grp_t(5),
                       bias_c, bias0, c31, gates, -(-t // SEL_BLOCK))
    sb_rows = h3[..., SB_W:3 * SB_W].reshape(b, t, 2, SB_HEADS, HEAD_DIM)
    keep = min(WINDOW, t)
    return (o_sb, o_nsa.reshape(b * t, NSA_Q_W), sb_rows, nkv[:, :, :4], nkv[:, t - keep:, 4:])


def _even_sample(h, db, page_table, cache_sb, cache_nsa, win_state, layer, cmp_w, tabs):
    h3 = h.reshape(db, DEC_S, -1)
    past = page_table.shape[1] * PAGE
    o = 3 * SB_W
    q = h3[..., :SB_W].reshape(db, DEC_S, SB_HEADS, HEAD_DIM).transpose(0, 2, 1, 3) * ATT_SCALE
    eye = jnp.eye(SB_HEADS, dtype=F32)
    qbd = (q[:, :, :, None, :] * eye[None, :, None, :, None]).reshape(db, SB_HEADS * DEC_S, SB_W)
    o_sb = sb_sample(page_table, qbd.astype(BF16), _pad_rows(h3[..., SB_W:o], PAGE).astype(BF16),
                     cache_sb, layer)
    q = h3[..., o:o + NSA_Q_W].reshape(db, DEC_S, NSA_GROUPS, NSA_REP, HEAD_DIM).transpose(0, 3, 2, 1, 4)
    eye = jnp.eye(NSA_GROUPS, dtype=F32)
    qbd = (q[:, :, :, :, None, :] * ATT_SCALE * eye[None, None, :, None, :, None])
    qbd = qbd.reshape(db, NSA_ROWS, NSA_KV_W).astype(BF16)
    o += NSA_Q_W
    nkv = h3[..., o:o + 6 * NSA_KV_W]
    gates = h3[..., o + 6 * NSA_KV_W:o + 6 * NSA_KV_W + NSA_GATE_W]
    gates = gates.reshape(db, DEC_S, NSA_GROUPS, NSA_REP, 3).transpose(0, 3, 2, 1, 4).reshape(db, NSA_ROWS, 3)
    gates = jnp.pad(gates, ((0, 0), (0, 0), (0, LANES - 3)))
    kcvc = compress_pages(page_table, cache_nsa, layer, 0, *cmp_w, positions_on_lanes=True)
    win_len = win_state.shape[2]
    new_win = nkv[..., 4 * NSA_KV_W:]
    win_all = jnp.concatenate([win_state[:, layer].reshape(db, win_len, 2 * NSA_KV_W), new_win], axis=1)
    wl = tabs["bias_w"].shape[1]
    n_blocks = -(-(past + DEC_S) // SEL_BLOCK)
    o_c, o_w, sel = nsa_sample_cw(qbd, kcvc, tabs["bias_c"], _pad_rows(win_all, wl).astype(BF16),
                                  tabs["bias_w"], win_len, min(N_SEL, n_blocks) - 1)
    o_nsa = nsa_sample_sel(page_table, qbd, _pad_rows(nkv[..., 2 * NSA_KV_W:4 * NSA_KV_W], PAGE).astype(BF16),
                           sel, tabs["cvec"], tabs["blast"], tabs["bnew"], o_c, o_w, gates, cache_nsa, layer)
    sb_rows = h3[..., SB_W:3 * SB_W].reshape(db, DEC_S, 2, SB_HEADS, HEAD_DIM)
    nsa_rows = nkv[..., :4 * NSA_KV_W].reshape(db, DEC_S, 4, NSA_GROUPS, HEAD_DIM)
    win_rows = win_all[:, max(0, win_len + DEC_S - WINDOW):].reshape(db, -1, 2, NSA_GROUPS, HEAD_DIM)
    return (o_sb.reshape(db * DEC_S, SB_W), o_nsa.reshape(db * DEC_S, NSA_Q_W), sb_rows, nsa_rows, win_rows)


def _odd_prompt(h, b, t, bias0, c31, lam_rows, sub_g, lam_init):
    h3 = h.reshape(b, t, -1)
    q = h3[..., :DIFF_Q_W].reshape(b, t, DIFF_KV_HEADS, DIFF_REP, 2, HEAD_DIM).transpose(0, 2, 3, 4, 1, 5)
    kv = h3[..., DIFF_Q_W:].reshape(b, t, 2, DIFF_KV_HEADS, 2 * HEAD_DIM)
    k = kv[:, :, 0].reshape(b, t, DIFF_KV_HEADS, 2, HEAD_DIM).transpose(0, 2, 3, 1, 4)
    vt = kv[:, :, 1].transpose(0, 2, 3, 1)
    kpad = bias0.shape[1] - QB
    o = diff_prompt(q.astype(BF16), _front_pad(k.astype(BF16), 3, kpad), _front_pad(vt.astype(BF16), 3, kpad),
                    bias0, c31, lam_rows, sub_g.T, lam_init)
    return o.reshape(b * t, DIFF_Q_W), kv


def _odd_sample(h, db, page_table, cache, layer, tabs, lam_rows, sub_g, lam_init):
    h3 = h.reshape(db, DEC_S, -1)
    q = h3[..., :DIFF_Q_W].reshape(db, DEC_S, DIFF_KV_HEADS, DIFF_REP, 2, HEAD_DIM).transpose(0, 2, 3, 4, 1, 5)
    eye_c = jnp.eye(2, dtype=F32)
    qc = q[..., None, :] * ATT_SCALE * eye_c[None, None, None, :, None, :, None]
    qc = qc.reshape(db, DIFF_HEADS * 2 * DEC_S, 2 * HEAD_DIM).astype(BF16)
    kv = h3[..., DIFF_Q_W:].reshape(db, DEC_S, 2, DIFF_KV_HEADS, 2 * HEAD_DIM)
    new_kv = _pad_rows(kv.reshape(db, DEC_S * DIFF_ROW_STRIDE, 2 * HEAD_DIM), PAGE * DIFF_ROW_STRIDE)
    o = diff_sample(page_table, qc, new_kv, tabs["cvec"], tabs["blast"], tabs["bnew"], lam_rows, sub_g,
                    cache, layer, lam_init)
    return o.reshape(db * DEC_S, DIFF_Q_W), kv


def _sample_tables(tab, past, arrange, with_cmp):
    def table(first, n, step=1):
        rows = [jnp.flip(_window(tab, first(t) - step * (n - 1), n, step), axis=1) for t in range(DEC_S)]
        return arrange(jnp.stack(rows, axis=1))

    tabs = {
        "cvec": arrange(jnp.broadcast_to(tab[:, -1][:, None, None], (tab.shape[0], DEC_S, 1))),
        "blast": table(lambda t: t + PAGE, PAGE),
        "bnew": table(lambda t: t, PAGE),
    }
    if with_cmp:
        nc = past // CMP_BLOCK
        tabs["bias_c"] = table(lambda t: past + t - (CMP_BLOCK - 1), nc, CMP_BLOCK)
        win_len = min(WINDOW, past)
        wl = -(-(win_len + DEC_S) // LANES) * LANES
        tabs["bias_w"] = table(lambda t: t + win_len, wl)
    return tabs


def kernel(x_prompt, x_sample, cache_sb_kv, cache_nsa_kv, cache_diff_kv, state_nsa_win, page_table, rel_bias,
           even_w_in, even_cmp_pe, even_cmp_wk, even_cmp_wv, even_w_out, odd_w_in, odd_lambda, odd_subln_g,
           odd_w_out, ln_mix_g, ln_mix_b, ln_ffn_g, ln_ffn_b, moe_w_router, moe_b_router, moe_w_up, moe_b_up,
           moe_w_down, moe_b_down):
    b, t, d = x_prompt.shape
    db, s, _ = x_sample.shape
    depth = ln_mix_g.shape[0]
    n_pool = cache_sb_kv.shape[0]
    past = page_table.shape[1] * PAGE
    assert s == DEC_S and d == D_MODEL and t % QB == 0 and (b * t) % MOE_TILE == 0
    assert past % (SEL_BLOCK * LANES) == 0 and state_nsa_win.shape[2] == WINDOW

    params = {
        "alpha": (2.0 * depth) ** 0.25,
        "ln_mix_g": ln_mix_g.astype(F32), "ln_mix_b": ln_mix_b.astype(F32),
        "ln_ffn_g": ln_ffn_g.astype(F32), "ln_ffn_b": ln_ffn_b.astype(F32),
        "w_router": moe_w_router, "b_router": moe_b_router,
        "experts": [(split_pairs(moe_w_up[l]), _split_pairs_bias(moe_b_up[l]),
                     moe_w_down[l].astype(BF16), moe_b_down[l][:, None, :].astype(F32)) for l in range(depth)],
    }
    tab = _distance_table(rel_bias)
    c31 = tab[:, -1]
    bias0 = _first_tile_bias(tab, KT)
    bias0_diff = _first_tile_bias(tab, DIFF_KT)
    bias_c = _cmp_band_table(tab)
    nsa_arrange = lambda x: x.reshape(NSA_GROUPS, NSA_REP, DEC_S, -1).transpose(1, 0, 2, 3).reshape(NSA_ROWS, -1)
    diff_arrange = lambda x: jnp.broadcast_to(x[:, None], (DIFF_HEADS, 2) + x.shape[1:]).reshape(
        DIFF_HEADS * 2 * DEC_S, -1)
    nsa_tabs = _sample_tables(tab, past, nsa_arrange, True)
    diff_tabs = _sample_tables(tab, past, diff_arrange, False)

    cache_sb = _positions_on_lanes(cache_sb_kv)
    cache_nsa = _positions_on_lanes(cache_nsa_kv)
    cache_diff = cache_diff_kv.reshape(n_pool, -1, PAGE * DIFF_ROW_STRIDE, 2 * HEAD_DIM)

    xp = x_prompt.reshape(b * t, d).astype(F32)
    xs = x_sample.reshape(db * s, d).astype(F32)
    xpb, xsb = xp.astype(BF16), xs.astype(BF16)
    outs = {k: [] for k in ("sb_p", "sb_s", "nsa_p", "nsa_s", "win_p", "win_s", "diff_p", "diff_s")}
    for l in range(depth):
        j = l // 2
        if l % 2 == 0:
            w_in = jnp.pad(even_w_in[j], ((0, 0), (0, EVEN_IN_PAD - EVEN_IN))).astype(BF16)
            cmp_w = _compress_weights(even_cmp_pe[j], even_cmp_wk[j], even_cmp_wv[j])
            a1, a2, r_sb, r_nsa, r_win = _even_prompt(matmul(xpb, w_in, 512), b, t, cmp_w, bias_c, bias0, c31)
            s1, s2, s_sb, s_nsa, s_win = _even_sample(matmul(xsb, w_in, 512), db, page_table, cache_sb,
                                                      cache_nsa, state_nsa_win, j, cmp_w, nsa_tabs)
            w_out = even_w_out[j]
            for key, val in (("sb_p", r_sb), ("sb_s", s_sb), ("nsa_p", r_nsa), ("nsa_s", s_nsa),
                             ("win_p", r_win), ("win_s", s_win)):
                outs[key].append(val)
        else:
            lam_init = 0.8 - 0.6 * math.exp(-0.3 * l)
            w_in = odd_w_in[j].astype(BF16)
            lam_rows = odd_lambda[j].astype(F32)
            sub_g = odd_subln_g[j][None].astype(F32)
            ap, r_diff = _odd_prompt(matmul(xpb, w_in, 512), b, t, bias0_diff, c31, lam_rows, sub_g, lam_init)
            as_, s_diff = _odd_sample(matmul(xsb, w_in, 512), db, page_table, cache_diff, j, diff_tabs,
                                      lam_rows, sub_g, lam_init)
            half = DIFF_Q_W // 2
            a1, a2, s1, s2 = ap[:, :half], ap[:, half:], as_[:, :half], as_[:, half:]
            w_out = odd_w_out[j]
            outs["diff_p"].append(r_diff)
            outs["diff_s"].append(s_diff)
        xp, xpb = _channel_mixer(l, a1, a2, w_out, xp, params, MOE_TILE)
        xs, xsb = _channel_mixer(l, s1, s2, w_out, xs, params, db * s)
    stack = lambda key: jnp.stack(outs[key], axis=1)
    return (xp.reshape(b, t, d), xs.reshape(db, s, d), stack("sb_p"), stack("sb_s"), stack("nsa_p"),
            stack("nsa_s"), stack("diff_p"), stack("diff_s"), stack("win_p"), stack("win_s"))
```
